```python
import math
import jax, jax.numpy as jnp
from jax import lax
import numpy as np

D_MODEL = 1024
BATCH = 32
SEQ = 2048
DEPTH = 1

HEAD_DIM = 64
N_ATTN_HEADS = 12
ATTN_WIDTH = N_ATTN_HEADS * HEAD_DIM
CONV_WIDTH = D_MODEL - ATTN_WIDTH
CONV_GROUPS = 4
CONV_K = 3
PROJ_WIDTH = 3 * ATTN_WIDTH + 3 * CONV_WIDTH
SPLIT_POINTS = [ATTN_WIDTH, 2 * ATTN_WIDTH, 3 * ATTN_WIDTH,
                3 * ATTN_WIDTH + CONV_WIDTH, 3 * ATTN_WIDTH + 2 * CONV_WIDTH]

DILATED_BRANCHES = ((128, 1), (512, 4), (2048, 16))
ATTN_BLOCK = 128
N_REL_BUCKETS = 32
REL_MAX_DISTANCE = 2048

N_EXPERTS = 256
TOP_K = 8
N_EXPERT_GROUPS = 8
TOPK_GROUPS = 4
EXPERT_DIM = 256
SHARED_DIM = 256
ROUTED_SCALE = 2.5
MOE_BLOCK = 128

DEEPNORM_ALPHA = (2.0 * DEPTH) ** 0.25
DEEPNORM_BETA = (8.0 * DEPTH) ** -0.25
LN_EPS = 1e-5

kernel_name = "hymba_dilated_conv_moe_deepnorm"


def _layer_norm(x, g, b):
    xf = x.astype(jnp.float32)
    mu = xf.mean(-1, keepdims=True)
    var = jnp.square(xf - mu).mean(-1, keepdims=True)
    return ((xf - mu) * lax.rsqrt(var + LN_EPS) * g + b).astype(x.dtype)


def _t5_bucket(dist):
    max_exact = N_REL_BUCKETS // 2
    dist = np.asarray(dist)
    log_part = np.log(np.maximum(dist, 1) / max_exact) / math.log(REL_MAX_DISTANCE / max_exact)
    large = max_exact + (log_part * (N_REL_BUCKETS - max_exact)).astype(np.int32)
    large = np.minimum(large, N_REL_BUCKETS - 1)
    return np.where(dist < max_exact, dist, large).astype(np.int32)


def _branch_pattern(window, dilation, n_blocks):
    w_sub = window // dilation
    assert w_sub <= ATTN_BLOCK
    qi = np.arange(ATTN_BLOCK)[:, None]
    ki = np.arange(2 * ATTN_BLOCK)[None, :]
    steps = qi + ATTN_BLOCK - ki
    band = (steps >= 0) & (steps <= w_sub)
    bucket = _t5_bucket(np.clip(steps, 0, w_sub) * dilation)
    key_exists = (np.arange(n_blocks)[:, None, None] > 0) | (ki[None] >= ATTN_BLOCK)
    return bucket, band[None] & key_exists


def _dilated_branch(q, k, v, rel_bias, window, dilation):
    b, h, s, e = q.shape
    seg = s // dilation
    n_blocks = -(-seg // ATTN_BLOCK)
    seg_pad = n_blocks * ATTN_BLOCK

    def to_blocks(t):
        t = t.reshape(b, h, seg, dilation, e).transpose(0, 1, 3, 2, 4)
        t = jnp.pad(t, ((0, 0), (0, 0), (0, 0), (0, seg_pad - seg), (0, 0)))
        return t.reshape(b, h, dilation, n_blocks, ATTN_BLOCK, e)

    def with_prev(t):
        prev = jnp.pad(t[:, :, :, :-1], ((0, 0), (0, 0), (0, 0), (1, 0), (0, 0), (0, 0)))
        return jnp.concatenate([prev, t], axis=-2)

    qb = to_blocks(q)
    kk = with_prev(to_blocks(k))
    vv = with_prev(to_blocks(v)).astype(jnp.float32)
    bucket, mask = _branch_pattern(window, dilation, n_blocks)
    bias = rel_bias[bucket].astype(jnp.float32).transpose(2, 0, 1)
    logits = jnp.einsum('bhrnqe,bhrnke->bhrnqk', qb, kk).astype(jnp.float32)
    logits = logits * (HEAD_DIM ** -0.5) + bias[None, :, None, None]
    logits = jnp.where(mask[None, None, None], logits, -jnp.inf)
    m = logits.max(-1, keepdims=True)
    p = jnp.exp(logits - m)
    den = p.sum(-1)
    o = jnp.einsum('bhrnqk,bhrnke->bhrnqe', p, vv) / den[..., None]
    lse = m[..., 0] + jnp.log(den)
    o = o.reshape(b, h, dilation, seg_pad, e)[:, :, :, :seg].transpose(0, 1, 3, 2, 4).reshape(b, h, s, e)
    lse = lse.reshape(b, h, dilation, seg_pad)[:, :, :, :seg].transpose(0, 1, 3, 2).reshape(b, h, s)
    return o, lse


def _dilated_attention(q, k, v, rel_bias):
    b, h, s, e = q.shape
    outs, lses = [], []
    for window, dilation in DILATED_BRANCHES:
        o, lse = _dilated_branch(q, k, v, rel_bias, window, dilation)
        outs.append(o)
        lses.append(lse)
    wts = jax.nn.softmax(jnp.stack(lses), axis=0)
    o = jnp.einsum('gbhs,gbhse->bhse', wts, jnp.stack(outs))
    return o.transpose(0, 2, 1, 3).reshape(b, s, h * e)


def _short_conv(cx, gate_b, gate_c, conv_w):
    u = gate_c * cx
    y = lax.conv_general_dilated(u, conv_w[:, None, :].astype(u.dtype), window_strides=(1,),
                                 padding=((CONV_K - 1, 0),),
                                 dimension_numbers=('NWC', 'WIO', 'NWC'),
                                 feature_group_count=CONV_WIDTH)
    return gate_b * y


def _swiglu(x, wg, wu, wd):
    return (jax.nn.silu(x @ wg) * (x @ wu)) @ wd


def _route(xf, w_router, router_bias):
    n = xf.shape[0]
    scores = jax.nn.sigmoid((xf @ w_router).astype(jnp.float32))
    biased = scores + router_bias.astype(jnp.float32)
    grouped = biased.reshape(n, N_EXPERT_GROUPS, N_EXPERTS // N_EXPERT_GROUPS)
    group_score = lax.top_k(grouped, 2)[0].sum(-1)
    _, top_groups = lax.top_k(group_score, TOPK_GROUPS)
    group_keep = jax.nn.one_hot(top_groups, N_EXPERT_GROUPS, dtype=jnp.float32).sum(-2) > 0
    expert_keep = jnp.repeat(group_keep, N_EXPERTS // N_EXPERT_GROUPS, axis=-1)
    _, top_idx = lax.top_k(jnp.where(expert_keep, biased, -jnp.inf), TOP_K)
    gate = jnp.take_along_axis(scores, top_idx, axis=-1)
    gate = gate / gate.sum(-1, keepdims=True) * ROUTED_SCALE
    return top_idx, gate


def _routed_experts(xf, top_idx, gate, w_gate, w_up, w_down):
    n_tok, d = xf.shape
    m = n_tok * TOP_K
    flat_e = top_idx.reshape(-1).astype(jnp.int32)
    flat_tok = jnp.arange(m, dtype=jnp.int32) // TOP_K
    flat_w = gate.reshape(-1).astype(xf.dtype)
    order = jnp.argsort(flat_e)
    se, stok, sw = flat_e[order], flat_tok[order], flat_w[order]
    counts = jnp.bincount(flat_e, length=N_EXPERTS).astype(jnp.int32)
    starts = jnp.cumsum(counts) - counts
    padded = (counts + MOE_BLOCK - 1) // MOE_BLOCK * MOE_BLOCK
    pends = jnp.cumsum(padded)
    pstarts = pends - padded
    pos = pstarts[se] + jnp.arange(m, dtype=jnp.int32) - starts[se]
    n_blocks = (m + N_EXPERTS * (MOE_BLOCK - 1) + MOE_BLOCK - 1) // MOE_BLOCK
    rows = n_blocks * MOE_BLOCK
    buf_tok = jnp.full((rows,), n_tok, jnp.int32).at[pos].set(stok)
    buf_w = jnp.zeros((rows,), xf.dtype).at[pos].set(sw)
    block_e = jnp.searchsorted(pends, jnp.arange(n_blocks, dtype=jnp.int32) * MOE_BLOCK, side='right')
    block_e = jnp.minimum(block_e, N_EXPERTS - 1)
    x_pad = jnp.concatenate([xf, jnp.zeros((1, d), xf.dtype)], axis=0)

    def step(y, blk):
        tok, wgt, e = blk
        xb = x_pad[tok]
        out = _swiglu(xb, w_gate[e], w_up[e], w_down[e]) * wgt[:, None]
        return y.at[tok].add(out.astype(y.dtype)), None

    y, _ = lax.scan(step, jnp.zeros_like(x_pad),
                    (buf_tok.reshape(n_blocks, MOE_BLOCK), buf_w.reshape(n_blocks, MOE_BLOCK), block_e))
    return y[:n_tok]


def setup_inputs(seed: int = 0) -> dict:
    key = jax.random.key(seed)
    ks = jax.random.split(key, 20)
    nrm = jax.random.normal
    L = DEPTH
    return {
        "x": nrm(ks[0], (BATCH, SEQ, D_MODEL), jnp.float32),
        "w_in": nrm(ks[1], (L, D_MODEL, PROJ_WIDTH), jnp.float32) * D_MODEL ** -0.5,
        "conv_w": nrm(ks[2], (L, CONV_K, CONV_WIDTH), jnp.float32) * CONV_K ** -0.5,
        "w_o": nrm(ks[3], (L, D_MODEL, D_MODEL), jnp.float32) * D_MODEL ** -0.5 * DEEPNORM_BETA,
        "ln1_g": 1.0 + 0.05 * nrm(ks[4], (L, D_MODEL), jnp.float32),
        "ln1_b": 0.02 * nrm(ks[5], (L, D_MODEL), jnp.float32),
        "rel_bias": 0.2 * nrm(ks[6], (N_REL_BUCKETS, N_ATTN_HEADS), jnp.float32),
        "w_router": nrm(ks[7], (L, D_MODEL, N_EXPERTS), jnp.float32) * D_MODEL ** -0.5,
        "router_bias": 0.01 * nrm(ks[8], (L, N_EXPERTS), jnp.float32),
        "exp_w_gate": nrm(ks[9], (L, N_EXPERTS, D_MODEL, EXPERT_DIM), jnp.float32) * D_MODEL ** -0.5,
        "exp_w_up": nrm(ks[10], (L, N_EXPERTS, D_MODEL, EXPERT_DIM), jnp.float32) * D_MODEL ** -0.5,
        "exp_w_down": nrm(ks[11], (L, N_EXPERTS, EXPERT_DIM, D_MODEL), jnp.float32) * EXPERT_DIM ** -0.5 * DEEPNORM_BETA,
        "sh_w_gate": nrm(ks[12], (L, D_MODEL, SHARED_DIM), jnp.float32) * D_MODEL ** -0.5,
        "sh_w_up": nrm(ks[13], (L, D_MODEL, SHARED_DIM), jnp.float32) * D_MODEL ** -0.5,
        "sh_w_down": nrm(ks[14], (L, SHARED_DIM, D_MODEL), jnp.float32) * SHARED_DIM ** -0.5 * DEEPNORM_BETA,
        "ln2_g": 1.0 + 0.05 * nrm(ks[15], (L, D_MODEL), jnp.float32),
        "ln2_b": 0.02 * nrm(ks[16], (L, D_MODEL), jnp.float32),
    }


def reference(x, w_in, conv_w, w_o, ln1_g, ln1_b, rel_bias, w_router, router_bias,
              exp_w_gate, exp_w_up, exp_w_down, sh_w_gate, sh_w_up, sh_w_down, ln2_g, ln2_b):
    b, s, d = x.shape
    for layer in range(DEPTH):
        proj = x @ w_in[layer]
        q, k, v, cx, cb, cc = jnp.split(proj, SPLIT_POINTS, axis=-1)
        heads = lambda t: t.reshape(b, s, N_ATTN_HEADS, HEAD_DIM).transpose(0, 2, 1, 3)
        attn = _dilated_attention(heads(q), heads(k), heads(v), rel_bias)
        conv = _short_conv(cx, cb, cc, conv_w[layer])
        mix = jnp.concatenate([attn.astype(x.dtype), conv], axis=-1) @ w_o[layer]
        x = _layer_norm(DEEPNORM_ALPHA * x + mix, ln1_g[layer], ln1_b[layer])
        xf = x.reshape(b * s, d)
        top_idx, gate = _route(xf, w_router[layer], router_bias[layer])
        routed = _routed_experts(xf, top_idx, gate, exp_w_gate[layer], exp_w_up[layer], exp_w_down[layer])
        shared = _swiglu(xf, sh_w_gate[layer], sh_w_up[layer], sh_w_down[layer])
        ffn = (routed + shared).reshape(b, s, d)
        x = _layer_norm(DEEPNORM_ALPHA * x + ffn, ln2_g[layer], ln2_b[layer])
    return x
```

```python
import functools
import math

import numpy as np
import jax
import jax.numpy as jnp
from jax import lax
from jax.experimental import pallas as pl
from jax.experimental.pallas import tpu as pltpu

F32 = jnp.float32
BF16 = jnp.bfloat16
I32 = jnp.int32

D_MODEL = 1024
HEAD_DIM = 64
N_ATTN_HEADS = 12
ATTN_WIDTH = N_ATTN_HEADS * HEAD_DIM
CONV_WIDTH = D_MODEL - ATTN_WIDTH
CONV_K = 3
DILATED_BRANCHES = ((128, 1), (512, 4), (2048, 16))
ATTN_BLOCK = 128
N_REL_BUCKETS = 32
REL_MAX_DISTANCE = 2048
N_EXPERTS = 256
TOP_K = 8
N_EXPERT_GROUPS = 8
TOPK_GROUPS = 4
GROUP_SIZE = N_EXPERTS // N_EXPERT_GROUPS
EXPERT_DIM = 256
SHARED_DIM = 256
ROUTED_SCALE = 2.5
DEPTH = 1
DEEPNORM_ALPHA = (2.0 * DEPTH) ** 0.25
LN_EPS = 1e-5

LANES = 128
GROUP_WIDTH = 2 * LANES
HEADS_PER_GROUP = GROUP_WIDTH // HEAD_DIM
N_GROUPS = D_MODEL // GROUP_WIDTH
N_ATTN_GROUPS = ATTN_WIDTH // GROUP_WIDTH
VMEM_LIMIT_BYTES = 56 * 1024 * 1024

MASK_VALUE = -1e30

ROW_CHUNK = 256
POST_TILE = 512
MOE_BLOCK = 256
DISPATCH_TILE = 512
COMBINE_TILE = 256


def _t5_bucket(dist):
    max_exact = N_REL_BUCKETS // 2
    dist = np.asarray(dist)
    log_part = np.log(np.maximum(dist, 1) / max_exact) / math.log(REL_MAX_DISTANCE / max_exact)
    large = max_exact + (log_part * (N_REL_BUCKETS - max_exact)).astype(np.int32)
    large = np.minimum(large, N_REL_BUCKETS - 1)
    return np.where(dist < max_exact, dist, large).astype(np.int32)


def _branch_bucket_and_band(window, dilation):
    w_sub = window // dilation
    qi = np.arange(ATTN_BLOCK)[:, None]
    ki = np.arange(2 * ATTN_BLOCK)[None, :]
    steps = qi + ATTN_BLOCK - ki
    band = (steps >= 0) & (steps <= w_sub)
    bucket = _t5_bucket(np.clip(steps, 0, w_sub) * dilation)
    return bucket, band


def _bias_table(rel_bias):
    tabs = []
    for window, dilation in DILATED_BRANCHES:
        bucket, band = _branch_bucket_and_band(window, dilation)
        b = rel_bias[bucket].astype(F32)
        b = jnp.where(band[:, :, None], b, MASK_VALUE)
        tabs.append(b.transpose(2, 0, 1))
    t = jnp.stack(tabs, axis=1)
    t = t.reshape(N_ATTN_GROUPS, HEADS_PER_GROUP, len(DILATED_BRANCHES), ATTN_BLOCK, 2 * ATTN_BLOCK)
    t = t.transpose(0, 2, 1, 3, 4)
    return t.reshape(N_ATTN_GROUPS, len(DILATED_BRANCHES), HEADS_PER_GROUP * ATTN_BLOCK, 2 * ATTN_BLOCK)


def _lane_head():
    return lax.shift_right_logical(lax.broadcasted_iota(I32, (ATTN_BLOCK, GROUP_WIDTH), 1), 6)


def _attn_step(it, bi, dilation, first, q_s, k_s, v_s, bias_ref, o_s, lse_s):
    logd = int(math.log2(dilation))
    r = jnp.bitwise_and(it, dilation - 1)
    n = lax.shift_right_logical(it, logd)
    start = r + (dilation * ATTN_BLOCK) * n

    def rows(st):
        return pl.ds(st, ATTN_BLOCK) if dilation == 1 else pl.ds(st, ATTN_BLOCK, stride=dilation)

    def ld(ref, st):
        return jnp.concatenate([ref[0, rows(st), :], ref[1, rows(st), :]], axis=1)

    qf = ld(q_s, start)
    if first:
        kk = ld(k_s, start).astype(BF16)
        vv = ld(v_s, start).astype(BF16)
        bias = bias_ref[0, bi, :, ATTN_BLOCK:]
    else:
        prev = start - dilation * ATTN_BLOCK
        kk = jnp.concatenate([ld(k_s, prev), ld(k_s, start)], axis=0).astype(BF16)
        vv = jnp.concatenate([ld(v_s, prev), ld(v_s, start)], axis=0).astype(BF16)
        bias = bias_ref[0, bi]
    lane_head = _lane_head()
    q4 = jnp.concatenate(
        [jnp.where(lane_head == h, qf, 0.0) for h in range(HEADS_PER_GROUP)], axis=0).astype(BF16)
    s = lax.dot_general(q4, kk, (((1,), (1,)), ((), ())), preferred_element_type=F32) + bias
    m = jnp.max(s, axis=-1, keepdims=True)
    p = jnp.exp(s - m)
    l = jnp.sum(p, axis=-1, keepdims=True)
    pv = jnp.dot(p.astype(BF16), vv, preferred_element_type=F32)
    pvn = pv * (1.0 / l)
    lse = m + jnp.log(l)
    o = jnp.zeros((ATTN_BLOCK, GROUP_WIDTH), F32)
    lb = jnp.zeros((ATTN_BLOCK, GROUP_WIDTH), F32)
    for h in range(HEADS_PER_GROUP):
        sel = lane_head == h
        o = jnp.where(sel, pvn[h * ATTN_BLOCK:(h + 1) * ATTN_BLOCK], o)
        lb = jnp.where(sel, lse[h * ATTN_BLOCK:(h + 1) * ATTN_BLOCK], lb)
    for sl in range(2):
        o_s[bi, sl, rows(start), :] = o[:, sl * LANES:(sl + 1) * LANES]
        lse_s[bi, sl, rows(start), :] = lb[:, sl * LANES:(sl + 1) * LANES]


def _mix_kernel(x_ref, w_ref, bias_ref, cw_ref, out_ref, xb_s, q_s, k_s, v_s, o_s, lse_s, u_s):
    g = pl.program_id(1)
    seq = x_ref.shape[1]
    n_chunks = seq // ROW_CHUNK

    @pl.when(g == 0)
    def _cast_x():
        def body(c, carry):
            rows = pl.ds(pl.multiple_of(c * ROW_CHUNK, ROW_CHUNK), ROW_CHUNK)
            xb_s[rows, :] = x_ref[0, rows, :].astype(BF16)
            return carry
        lax.fori_loop(0, n_chunks, body, 0)

    def proj(c, carry):
        rows = pl.ds(pl.multiple_of(c * ROW_CHUNK, ROW_CHUNK), ROW_CHUNK)
        res = jnp.dot(xb_s[rows, :], w_ref[0], preferred_element_type=F32)
        for j, dst in enumerate((q_s, k_s, v_s)):
            for sl in range(2):
                lo = j * GROUP_WIDTH + sl * LANES
                dst[sl, rows, :] = res[:, lo:lo + LANES]
        return carry
    lax.fori_loop(0, n_chunks, proj, 0)

    @pl.when(g < N_ATTN_GROUPS)
    def _attention():
        for bi, (window, dilation) in enumerate(DILATED_BRANCHES):
            n_blocks = seq // dilation // ATTN_BLOCK
            n_steps = n_blocks * dilation
            step = functools.partial(_attn_step, bi=bi, dilation=dilation, q_s=q_s, k_s=k_s, v_s=v_s,
                                     bias_ref=bias_ref, o_s=o_s, lse_s=lse_s)

            def first_body(it, carry, step=step):
                step(it, first=True)
                return carry

            def rest_body(it, carry, step=step):
                step(it, first=False)
                return carry
            lax.fori_loop(0, dilation, first_body, 0)
            if n_steps > dilation:
                lax.fori_loop(dilation, n_steps, rest_body, 0)

        def combine(c, carry):
            rows = pl.ds(pl.multiple_of(c * ROW_CHUNK, ROW_CHUNK), ROW_CHUNK)
            for sl in range(2):
                ls = [lse_s[bi, sl, rows, :] for bi in range(len(DILATED_BRANCHES))]
                mx = jnp.maximum(jnp.maximum(ls[0], ls[1]), ls[2])
                ws = [jnp.exp(v - mx) for v in ls]
                den = ws[0] + ws[1] + ws[2]
                num = ws[0] * o_s[0, sl, rows, :] + ws[1] * o_s[1, sl, rows, :] + ws[2] * o_s[2, sl, rows, :]
                out_ref[0, rows, sl * LANES:(sl + 1) * LANES] = (num / den).astype(BF16)
            return carry
        lax.fori_loop(0, n_chunks, combine, 0)

    @pl.when(g == N_ATTN_GROUPS)
    def _short_conv():
        pad = 8
        for sl in range(2):
            u_s[sl, 0:pad, :] = jnp.zeros((pad, LANES), F32)
            for c in range(n_chunks):
                lo = c * ROW_CHUNK
                u_s[sl, pad + lo:pad + lo + ROW_CHUNK, :] = v_s[sl, lo:lo + ROW_CHUNK, :] * q_s[sl, lo:lo + ROW_CHUNK, :]
            w = [cw_ref[kk:kk + 1, sl * LANES:(sl + 1) * LANES] for kk in range(CONV_K)]
            for c in range(n_chunks):
                lo = c * ROW_CHUNK
                y = w[2] * u_s[sl, pad + lo:pad + lo + ROW_CHUNK, :]
                y = y + w[1] * u_s[sl, pad + lo - 1:pad + lo - 1 + ROW_CHUNK, :]
                y = y + w[0] * u_s[sl, pad + lo - 2:pad + lo - 2 + ROW_CHUNK, :]
                out_ref[0, lo:lo + ROW_CHUNK, sl * LANES:(sl + 1) * LANES] = (
                    k_s[sl, lo:lo + ROW_CHUNK, :] * y).astype(BF16)


def _mix_call(x, w_groups, bias_tbl, conv_w):
    batch, seq, d = x.shape
    return pl.pallas_call(
        _mix_kernel,
        out_shape=jax.ShapeDtypeStruct((batch, seq, d), BF16),
        grid=(batch, N_GROUPS),
        in_specs=[
            pl.BlockSpec((1, seq, d), lambda b, g: (b, 0, 0)),
            pl.BlockSpec((1, d, 3 * GROUP_WIDTH), lambda b, g: (g, 0, 0)),
            pl.BlockSpec((1,) + bias_tbl.shape[1:], lambda b, g: (jnp.minimum(g, N_ATTN_GROUPS - 1), 0, 0, 0)),
            pl.BlockSpec(conv_w.shape, lambda b, g: (0, 0)),
        ],
        out_specs=pl.BlockSpec((1, seq, GROUP_WIDTH), lambda b, g: (b, 0, g)),
        scratch_shapes=[
            pltpu.VMEM((seq, d), BF16),
            pltpu.VMEM((2, seq, LANES), F32),
            pltpu.VMEM((2, seq, LANES), F32),
            pltpu.VMEM((2, seq, LANES), F32),
            pltpu.VMEM((len(DILATED_BRANCHES), 2, seq, LANES), F32),
            pltpu.VMEM((len(DILATED_BRANCHES), 2, seq, LANES), F32),
            pltpu.VMEM((2, seq + 8, LANES), F32),
        ],
        compiler_params=pltpu.CompilerParams(
            dimension_semantics=("arbitrary", "arbitrary"), vmem_limit_bytes=VMEM_LIMIT_BYTES),
        name="mix",
    )(x, w_groups, bias_tbl, conv_w)


def _layer_norm(h, g, b):
    mu = jnp.mean(h, axis=-1, keepdims=True)
    c = h - mu
    var = jnp.mean(c * c, axis=-1, keepdims=True)
    return c * lax.rsqrt(var + LN_EPS) * g + b


def _first_argmax_rows(v, row_ids, n_rows):
    m = jnp.max(v, axis=0, keepdims=True)
    idx = jnp.min(jnp.where(v == m, row_ids, n_rows), axis=0, keepdims=True)
    return m, idx


def _post_kernel(mix_ref, x_ref, wo_ref, g_ref, b_ref, wrh_ref, wrl_ref, rb_ref,
                 x1_ref, e_ref, rank_ref, gate_ref, cnt_ref, carry_s):
    i = pl.program_id(0)
    tm = x_ref.shape[0]

    @pl.when(i == 0)
    def _init():
        carry_s[...] = jnp.zeros_like(carry_s)

    h = DEEPNORM_ALPHA * x_ref[...] + jnp.dot(mix_ref[...], wo_ref[...], preferred_element_type=F32)
    x1 = _layer_norm(h, g_ref[...], b_ref[...])
    x1_ref[...] = x1

    x_hi = x1.astype(BF16)
    x_lo = (x1 - x_hi.astype(F32)).astype(BF16)
    dn = (((1,), (1,)), ((), ()))
    logits = lax.dot_general(wrh_ref[...], x_hi, dn, preferred_element_type=F32)
    logits = logits + lax.dot_general(wrh_ref[...], x_lo, dn, preferred_element_type=F32)
    logits = logits + lax.dot_general(wrl_ref[...], x_hi, dn, preferred_element_type=F32)
    scores = jax.nn.sigmoid(logits)
    biased = scores + rb_ref[...]

    neg_inf = -jnp.inf
    sub_ids = lax.broadcasted_iota(I32, (GROUP_SIZE, tm), 0)
    gs_rows = []
    for gi in range(N_EXPERT_GROUPS):
        bg = biased[gi * GROUP_SIZE:(gi + 1) * GROUP_SIZE]
        m1, i1 = _first_argmax_rows(bg, sub_ids, GROUP_SIZE)
        m2 = jnp.max(jnp.where(sub_ids == i1, neg_inf, bg), axis=0, keepdims=True)
        gs_rows.append(m1 + m2)
    gscore = jnp.concatenate(gs_rows, axis=0)
    grp_ids = lax.broadcasted_iota(I32, (N_EXPERT_GROUPS, tm), 0)
    keep_g = jnp.zeros((N_EXPERT_GROUPS, tm), jnp.bool_)
    for _ in range(TOPK_GROUPS):
        _, gi1 = _first_argmax_rows(gscore, grp_ids, N_EXPERT_GROUPS)
        hit = grp_ids == gi1
        keep_g = jnp.logical_or(keep_g, hit)
        gscore = jnp.where(hit, neg_inf, gscore)
    keep_f = keep_g.astype(F32)
    keep_e = jnp.concatenate(
        [jnp.broadcast_to(keep_f[gi:gi + 1], (GROUP_SIZE, tm)) for gi in range(N_EXPERT_GROUPS)], axis=0)
    masked = jnp.where(keep_e > 0.5, biased, neg_inf)

    exp_ids = lax.broadcasted_iota(I32, (N_EXPERTS, tm), 0)
    sel = jnp.zeros((N_EXPERTS, tm), F32)
    e_rows, g_rows = [], []
    for _ in range(TOP_K):
        _, e1 = _first_argmax_rows(masked, exp_ids, N_EXPERTS)
        hit = exp_ids == e1
        sel = jnp.where(hit, 1.0, sel)
        g_rows.append(jnp.sum(jnp.where(hit, scores, 0.0), axis=0, keepdims=True))
        e_rows.append(e1)
        masked = jnp.where(hit, neg_inf, masked)
    gates = jnp.concatenate(g_rows, axis=0)
    gates = gates / jnp.sum(gates, axis=0, keepdims=True) * ROUTED_SCALE
    gate_ref[...] = gates
    e_ref[...] = jnp.concatenate(e_rows, axis=0)

    t_row = lax.broadcasted_iota(I32, (tm, tm), 0)
    t_col = lax.broadcasted_iota(I32, (tm, tm), 1)
    upper = (t_row < t_col).astype(BF16)
    excl = jnp.dot(sel.astype(BF16), upper, preferred_element_type=F32)
    base = carry_s[...] + excl
    r_rows = [jnp.sum(jnp.where(exp_ids == e1, base, 0.0), axis=0, keepdims=True) for e1 in e_rows]
    rank_ref[...] = jnp.concatenate(r_rows, axis=0).astype(I32)
    carry = carry_s[...] + jnp.sum(sel, axis=1, keepdims=True)
    carry_s[...] = carry
    cnt_ref[...] = carry.astype(I32)


def _post_call(mix2d, x2d, wo, ln_g, ln_b, wr_hi, wr_lo, rb):
    n_tok, d = x2d.shape
    tm = POST_TILE
    tok_spec = pl.BlockSpec((tm, d), lambda i: (i, 0))
    k_spec = pl.BlockSpec((TOP_K, tm), lambda i: (0, i))
    full = lambda a: pl.BlockSpec(a.shape, lambda i: (0,) * a.ndim)
    return pl.pallas_call(
        _post_kernel,
        out_shape=(
            jax.ShapeDtypeStruct((n_tok, d), F32),
            jax.ShapeDtypeStruct((TOP_K, n_tok), I32),
            jax.ShapeDtypeStruct((TOP_K, n_tok), I32),
            jax.ShapeDtypeStruct((TOP_K, n_tok), F32),
            jax.ShapeDtypeStruct((N_EXPERTS, 1), I32),
        ),
        grid=(n_tok // tm,),
        in_specs=[tok_spec, tok_spec, full(wo), full(ln_g), full(ln_b), full(wr_hi), full(wr_lo), full(rb)],
        out_specs=(tok_spec, k_spec, k_spec, k_spec, pl.BlockSpec((N_EXPERTS, 1), lambda i: (0, 0))),
        scratch_shapes=[pltpu.VMEM((N_EXPERTS, 1), F32)],
        compiler_params=pltpu.CompilerParams(
            dimension_semantics=("arbitrary",), vmem_limit_bytes=VMEM_LIMIT_BYTES),
        name="post",
    )(mix2d, x2d, wo, ln_g, ln_b, wr_hi, wr_lo, rb)


def _dispatch_kernel(pos_ref, x_ref, xs_in_ref, xs_ref, sem):
    del xs_in_ref
    tt = x_ref.shape[0]

    def row_copy(t, k):
        return pltpu.make_async_copy(x_ref.at[pl.ds(t, 1)], xs_ref.at[pl.ds(pos_ref[k, t], 1)], sem)

    def issue(t, carry):
        for k in range(TOP_K):
            row_copy(t, k).start()
        return carry
    lax.fori_loop(0, tt, issue, 0)

    def drain(t, carry):
        for k in range(TOP_K):
            row_copy(t, k).wait()
        return carry
    lax.fori_loop(0, tt, drain, 0)


def _dispatch_call(pos, x1, xs_init):
    n_tok, d = x1.shape
    tt = DISPATCH_TILE
    return pl.pallas_call(
        _dispatch_kernel,
        out_shape=jax.ShapeDtypeStruct(xs_init.shape, xs_init.dtype),
        grid=(n_tok // tt,),
        in_specs=[
            pl.BlockSpec((TOP_K, tt), lambda i: (0, i), memory_space=pltpu.SMEM),
            pl.BlockSpec((tt, d), lambda i: (i, 0)),
            pl.BlockSpec(memory_space=pl.ANY),
        ],
        out_specs=pl.BlockSpec(memory_space=pl.ANY),
        scratch_shapes=[pltpu.SemaphoreType.DMA(())],
        input_output_aliases={2: 0},
        compiler_params=pltpu.CompilerParams(
            dimension_semantics=("arbitrary",), vmem_limit_bytes=VMEM_LIMIT_BYTES),
        name="dispatch",
    )(pos, x1, xs_init)


def _experts_kernel(be_ref, nb_ref, xs_ref, wg_ref, wu_ref, wd_ref, y_ref):
    i = pl.program_id(0)

    @pl.when(i < nb_ref[0])
    def _compute():
        xb = xs_ref[...].astype(BF16)
        hg = jnp.dot(xb, wg_ref[0].astype(BF16), preferred_element_type=F32)
        hu = jnp.dot(xb, wu_ref[0].astype(BF16), preferred_element_type=F32)
        hh = (hg * jax.nn.sigmoid(hg) * hu).astype(BF16)
        y_ref[...] = jnp.dot(hh, wd_ref[0].astype(BF16), preferred_element_type=F32)


def _experts_call(block_e, n_used, xs, w_gate, w_up, w_down):
    rows, d = xs.shape
    n_blocks = rows // MOE_BLOCK

    def blk(i, be, nb):
        return (jnp.minimum(i, nb[0] - 1), 0)

    def wsel(i, be, nb):
        return (be[jnp.minimum(i, nb[0] - 1)], 0, 0)
    grid_spec = pltpu.PrefetchScalarGridSpec(
        num_scalar_prefetch=2,
        grid=(n_blocks,),
        in_specs=[
            pl.BlockSpec((MOE_BLOCK, d), blk),
            pl.BlockSpec((1, d, EXPERT_DIM), wsel),
            pl.BlockSpec((1, d, EXPERT_DIM), wsel),
            pl.BlockSpec((1, EXPERT_DIM, d), wsel),
        ],
        out_specs=pl.BlockSpec((MOE_BLOCK, d), blk),
    )
    return pl.pallas_call(
        _experts_kernel,
        out_shape=jax.ShapeDtypeStruct((rows, d), F32),
        grid_spec=grid_spec,
        compiler_params=pltpu.CompilerParams(
            dimension_semantics=("arbitrary",), vmem_limit_bytes=VMEM_LIMIT_BYTES),
        name="experts",
    )(block_e, n_used, xs, w_gate, w_up, w_down)


def _combine_kernel(pos_ref, gate_ref, x1_ref, y_ref, sg_ref, su_ref, sd_ref, g_ref, b_ref, out_ref, buf, sem):
    tt = x1_ref.shape[0]

    def row_copy(t, k):
        return pltpu.make_async_copy(y_ref.at[pl.ds(pos_ref[k, t], 1)], buf.at[k, pl.ds(t, 1)], sem)

    def issue(t, carry):
        for k in range(TOP_K):
            row_copy(t, k).start()
        return carry
    lax.fori_loop(0, tt, issue, 0)

    x1 = x1_ref[...]
    xb = x1.astype(BF16)
    hg = jnp.dot(xb, sg_ref[...], preferred_element_type=F32)
    hu = jnp.dot(xb, su_ref[...], preferred_element_type=F32)
    hh = (hg * jax.nn.sigmoid(hg) * hu).astype(BF16)
    acc = DEEPNORM_ALPHA * x1 + jnp.dot(hh, sd_ref[...], preferred_element_type=F32)

    def drain(t, carry):
        for k in range(TOP_K):
            row_copy(t, k).wait()
        return carry
    lax.fori_loop(0, tt, drain, 0)

    gates = gate_ref[...]
    for k in range(TOP_K):
        acc = acc + gates[:, k:k + 1] * buf[k]
    out_ref[...] = _layer_norm(acc, g_ref[...], b_ref[...])


def _combine_call(pos, gates_t, x1, y, sg, su, sd, ln_g, ln_b):
    n_tok, d = x1.shape
    tt = COMBINE_TILE
    full = lambda a: pl.BlockSpec(a.shape, lambda i: (0,) * a.ndim)
    return pl.pallas_call(
        _combine_kernel,
        out_shape=jax.ShapeDtypeStruct((n_tok, d), F32),
        grid=(n_tok // tt,),
        in_specs=[
            pl.BlockSpec((TOP_K, tt), lambda i: (0, i), memory_space=pltpu.SMEM),
            pl.BlockSpec((tt, TOP_K), lambda i: (i, 0)),
            pl.BlockSpec((tt, d), lambda i: (i, 0)),
            pl.BlockSpec(memory_space=pl.ANY),
            full(sg), full(su), full(sd), full(ln_g), full(ln_b),
        ],
        out_specs=pl.BlockSpec((tt, d), lambda i: (i, 0)),
        scratch_shapes=[pltpu.VMEM((TOP_K, tt, d), F32), pltpu.SemaphoreType.DMA(())],
        compiler_params=pltpu.CompilerParams(
            dimension_semantics=("arbitrary",), vmem_limit_bytes=VMEM_LIMIT_BYTES),
        name="combine",
    )(pos, gates_t, x1, y, sg, su, sd, ln_g, ln_b)


def _group_weights(w_in):
    blocks = w_in.reshape(D_MODEL, 12, GROUP_WIDTH)
    scale = HEAD_DIM ** -0.5
    groups = []
    for g in range(N_ATTN_GROUPS):
        groups.append(jnp.concatenate(
            [blocks[:, g] * scale, blocks[:, N_ATTN_GROUPS + g], blocks[:, 2 * N_ATTN_GROUPS + g]], axis=1))
    groups.append(jnp.concatenate([blocks[:, 9], blocks[:, 10], blocks[:, 11]], axis=1))
    return jnp.stack(groups).astype(BF16)


def kernel(x, w_in, conv_w, w_o, ln1_g, ln1_b, rel_bias, w_router, router_bias, exp_w_gate, exp_w_up,
           exp_w_down, sh_w_gate, sh_w_up, sh_w_down, ln2_g, ln2_b):
    batch, seq, d = x.shape
    assert d == D_MODEL and seq == DILATED_BRANCHES[-1][0] and w_in.shape[0] == DEPTH
    n_tok = batch * seq
    assert n_tok % POST_TILE == 0 and n_tok % DISPATCH_TILE == 0 and n_tok % COMBINE_TILE == 0

    mix = _mix_call(x, _group_weights(w_in[0]), _bias_table(rel_bias), conv_w[0])

    wr = w_router[0]
    wr_hi = wr.astype(BF16)
    wr_lo = (wr - wr_hi.astype(F32)).astype(BF16)
    x2d = x.reshape(n_tok, d)
    x1, e_idx, rank, gates, counts = _post_call(
        mix.reshape(n_tok, d), x2d, w_o[0].astype(BF16), ln1_g, ln1_b,
        wr_hi.T, wr_lo.T, router_bias[0].reshape(N_EXPERTS, 1))

    counts = counts[:, 0]
    padded = (counts + MOE_BLOCK - 1) // MOE_BLOCK * MOE_BLOCK
    pends = jnp.cumsum(padded)
    pstarts = pends - padded
    n_blocks = n_tok * TOP_K // MOE_BLOCK + N_EXPERTS
    block_e = jnp.searchsorted(pends, jnp.arange(n_blocks, dtype=I32) * MOE_BLOCK, side="right")
    block_e = jnp.minimum(block_e, N_EXPERTS - 1).astype(I32)
    n_used = (pends[-1:] // MOE_BLOCK).astype(I32)
    pos = pstarts[e_idx].astype(I32) + rank

    xs = _dispatch_call(pos, x1, jnp.zeros((n_blocks * MOE_BLOCK, d), F32))
    y = _experts_call(block_e, n_used, xs, exp_w_gate[0], exp_w_up[0], exp_w_down[0])

    out = _combine_call(pos, gates.T, x1, y, sh_w_gate[0].astype(BF16), sh_w_up[0].astype(BF16),
                        sh_w_down[0].astype(BF16), ln2_g, ln2_b)
    return out.reshape(batch, seq, d)
```

```python
import functools
import math

import numpy as np
import jax
import jax.numpy as jnp
from jax import lax
from jax.experimental import pallas as pl
from jax.experimental.pallas import tpu as pltpu

F32 = jnp.float32
BF16 = jnp.bfloat16
I32 = jnp.int32

D_MODEL = 1024
HEAD_DIM = 64
N_ATTN_HEADS = 12
ATTN_WIDTH = N_ATTN_HEADS * HEAD_DIM
CONV_WIDTH = D_MODEL - ATTN_WIDTH
CONV_K = 3
DILATED_BRANCHES = ((128, 1), (512, 4), (2048, 16))
ATTN_BLOCK = 128
N_REL_BUCKETS = 32
REL_MAX_DISTANCE = 2048
N_EXPERTS = 256
TOP_K = 8
N_EXPERT_GROUPS = 8
TOPK_GROUPS = 4
GROUP_SIZE = N_EXPERTS // N_EXPERT_GROUPS
EXPERT_DIM = 256
SHARED_DIM = 256
ROUTED_SCALE = 2.5
DEPTH = 1
DEEPNORM_ALPHA = (2.0 * DEPTH) ** 0.25
LN_EPS = 1e-5

LANES = 128
GROUP_WIDTH = 2 * LANES
HEADS_PER_GROUP = GROUP_WIDTH // HEAD_DIM
N_GROUPS = D_MODEL // GROUP_WIDTH
N_ATTN_GROUPS = ATTN_WIDTH // GROUP_WIDTH
VMEM_LIMIT_BYTES = 56 * 1024 * 1024

MASK_VALUE = -1e30

ROW_CHUNK = 256
ATTN_UNROLL = 2
ROW_TILE = 8
POS_TILE = 2048
POST_TILE = 512
MOE_BLOCK = 256
DISPATCH_TILE = 512
COMBINE_TILE = 256


def _t5_bucket(dist):
    max_exact = N_REL_BUCKETS // 2
    dist = np.asarray(dist)
    log_part = np.log(np.maximum(dist, 1) / max_exact) / math.log(REL_MAX_DISTANCE / max_exact)
    large = max_exact + (log_part * (N_REL_BUCKETS - max_exact)).astype(np.int32)
    large = np.minimum(large, N_REL_BUCKETS - 1)
    return np.where(dist < max_exact, dist, large).astype(np.int32)


def _branch_bucket_and_band(window, dilation):
    w_sub = window // dilation
    qi = np.arange(ATTN_BLOCK)[:, None]
    ki = np.arange(2 * ATTN_BLOCK)[None, :]
    steps = qi + ATTN_BLOCK - ki
    band = (steps >= 0) & (steps <= w_sub)
    bucket = _t5_bucket(np.clip(steps, 0, w_sub) * dilation)
    return bucket, band


def _bias_table(rel_bias):
    tabs = []
    for window, dilation in DILATED_BRANCHES:
        bucket, band = _branch_bucket_and_band(window, dilation)
        onehot = (bucket[:, :, None] == np.arange(N_REL_BUCKETS)).astype(np.float32)
        b = jnp.einsum("qkb,bh->qkh", onehot, rel_bias.astype(F32), precision=lax.Precision.HIGHEST)
        b = jnp.where(band[:, :, None], b, MASK_VALUE)
        tabs.append(b.transpose(2, 0, 1))
    t = jnp.stack(tabs, axis=1)
    t = t.reshape(N_ATTN_GROUPS, HEADS_PER_GROUP, len(DILATED_BRANCHES), ATTN_BLOCK, 2 * ATTN_BLOCK)
    t = t.transpose(0, 2, 1, 3, 4)
    return t.reshape(N_ATTN_GROUPS, len(DILATED_BRANCHES), HEADS_PER_GROUP * ATTN_BLOCK, 2 * ATTN_BLOCK)


def _lane_head():
    return lax.shift_right_logical(lax.broadcasted_iota(I32, (ATTN_BLOCK, GROUP_WIDTH), 1), 6)


def _attn_step(it, bi, dilation, first, q_s, k_s, v_s, bias_ref, o_s, lse_s):
    logd = int(math.log2(dilation))
    r = jnp.bitwise_and(it, dilation - 1)
    n = lax.shift_right_logical(it, logd)
    start = r + (dilation * ATTN_BLOCK) * n

    def rows(st):
        return pl.ds(st, ATTN_BLOCK) if dilation == 1 else pl.ds(st, ATTN_BLOCK, stride=dilation)

    def ld(ref, st):
        return jnp.concatenate([ref[0, rows(st), :], ref[1, rows(st), :]], axis=1)

    qf = ld(q_s, start)
    if first:
        kk = ld(k_s, start).astype(BF16)
        vv = ld(v_s, start).astype(BF16)
        bias = bias_ref[0, bi, :, ATTN_BLOCK:]
    else:
        prev = start - dilation * ATTN_BLOCK
        kk = jnp.concatenate([ld(k_s, prev), ld(k_s, start)], axis=0).astype(BF16)
        vv = jnp.concatenate([ld(v_s, prev), ld(v_s, start)], axis=0).astype(BF16)
        bias = bias_ref[0, bi]
    lane_head = _lane_head()
    q4 = jnp.concatenate(
        [jnp.where(lane_head == h, qf, 0.0) for h in range(HEADS_PER_GROUP)], axis=0).astype(BF16)
    s = lax.dot_general(q4, kk, (((1,), (1,)), ((), ())), preferred_element_type=F32) + bias
    m = jnp.max(s, axis=-1, keepdims=True)
    p = jnp.exp(s - m)
    l = jnp.sum(p, axis=-1, keepdims=True)
    pv = jnp.dot(p.astype(BF16), vv, preferred_element_type=F32)
    pvn = pv * (1.0 / l)
    lse = m + jnp.log(l)
    o = jnp.zeros((ATTN_BLOCK, GROUP_WIDTH), F32)
    lb = jnp.zeros((ATTN_BLOCK, GROUP_WIDTH), F32)
    for h in range(HEADS_PER_GROUP):
        sel = lane_head == h
        o = jnp.where(sel, pvn[h * ATTN_BLOCK:(h + 1) * ATTN_BLOCK], o)
        lb = jnp.where(sel, lse[h * ATTN_BLOCK:(h + 1) * ATTN_BLOCK], lb)
    for sl in range(2):
        o_s[bi, sl, rows(start), :] = o[:, sl * LANES:(sl + 1) * LANES]
        lse_s[bi, sl, rows(start), :] = lb[:, sl * LANES:(sl + 1) * LANES]


def _mix_kernel(x_ref, w_ref, bias_ref, cw_ref, out_ref, xb_s, q_s, k_s, v_s, o_s, lse_s, u_s):
    g = pl.program_id(1)
    seq = x_ref.shape[1]
    n_chunks = seq // ROW_CHUNK

    @pl.when(g == 0)
    def _cast_x():
        def body(c, carry):
            rows = pl.ds(pl.multiple_of(c * ROW_CHUNK, ROW_CHUNK), ROW_CHUNK)
            xb_s[rows, :] = x_ref[0, rows, :].astype(BF16)
            return carry
        lax.fori_loop(0, n_chunks, body, 0)

    def proj(c, carry):
        rows = pl.ds(pl.multiple_of(c * ROW_CHUNK, ROW_CHUNK), ROW_CHUNK)
        res = jnp.dot(xb_s[rows, :], w_ref[0], preferred_element_type=F32)
        for j, dst in enumerate((q_s, k_s, v_s)):
            for sl in range(2):
                lo = j * GROUP_WIDTH + sl * LANES
                dst[sl, rows, :] = res[:, lo:lo + LANES]
        return carry
    lax.fori_loop(0, n_chunks, proj, 0)

    @pl.when(g < N_ATTN_GROUPS)
    def _attention():
        for bi, (window, dilation) in enumerate(DILATED_BRANCHES):
            n_blocks = seq // dilation // ATTN_BLOCK
            n_steps = n_blocks * dilation
            step = functools.partial(_attn_step, bi=bi, dilation=dilation, q_s=q_s, k_s=k_s, v_s=v_s,
                                     bias_ref=bias_ref, o_s=o_s, lse_s=lse_s)

            def first_body(it, carry, step=step):
                step(it, first=True)
                return carry

            def rest_body(it, carry, step=step):
                step(it, first=False)
                return carry
            lax.fori_loop(0, dilation, first_body, 0, unroll=ATTN_UNROLL)
            if n_steps > dilation:
                lax.fori_loop(dilation, n_steps, rest_body, 0, unroll=ATTN_UNROLL)

        def combine(c, carry):
            rows = pl.ds(pl.multiple_of(c * ROW_CHUNK, ROW_CHUNK), ROW_CHUNK)
            for sl in range(2):
                ls = [lse_s[bi, sl, rows, :] for bi in range(len(DILATED_BRANCHES))]
                mx = jnp.maximum(jnp.maximum(ls[0], ls[1]), ls[2])
                ws = [jnp.exp(v - mx) for v in ls]
                den = ws[0] + ws[1] + ws[2]
                num = ws[0] * o_s[0, sl, rows, :] + ws[1] * o_s[1, sl, rows, :] + ws[2] * o_s[2, sl, rows, :]
                out_ref[0, rows, sl * LANES:(sl + 1) * LANES] = (num / den).astype(BF16)
            return carry
        lax.fori_loop(0, n_chunks, combine, 0)

    @pl.when(g == N_ATTN_GROUPS)
    def _short_conv():
        pad = 8
        for sl in range(2):
            u_s[sl, 0:pad, :] = jnp.zeros((pad, LANES), F32)
            for c in range(n_chunks):
                lo = c * ROW_CHUNK
                u_s[sl, pad + lo:pad + lo + ROW_CHUNK, :] = v_s[sl, lo:lo + ROW_CHUNK, :] * q_s[sl, lo:lo + ROW_CHUNK, :]
            w = [cw_ref[kk:kk + 1, sl * LANES:(sl + 1) * LANES] for kk in range(CONV_K)]
            for c in range(n_chunks):
                lo = c * ROW_CHUNK
                y = w[2] * u_s[sl, pad + lo:pad + lo + ROW_CHUNK, :]
                y = y + w[1] * u_s[sl, pad + lo - 1:pad + lo - 1 + ROW_CHUNK, :]
                y = y + w[0] * u_s[sl, pad + lo - 2:pad + lo - 2 + ROW_CHUNK, :]
                out_ref[0, lo:lo + ROW_CHUNK, sl * LANES:(sl + 1) * LANES] = (
                    k_s[sl, lo:lo + ROW_CHUNK, :] * y).astype(BF16)


def _mix_call(x, w_groups, bias_tbl, conv_w):
    batch, seq, d = x.shape
    return pl.pallas_call(
        _mix_kernel,
        out_shape=jax.ShapeDtypeStruct((batch, seq, d), BF16),
        grid=(batch, N_GROUPS),
        in_specs=[
            pl.BlockSpec((1, seq, d), lambda b, g: (b, 0, 0)),
            pl.BlockSpec((1, d, 3 * GROUP_WIDTH), lambda b, g: (g, 0, 0)),
            pl.BlockSpec((1,) + bias_tbl.shape[1:], lambda b, g: (jnp.minimum(g, N_ATTN_GROUPS - 1), 0, 0, 0)),
            pl.BlockSpec(conv_w.shape, lambda b, g: (0, 0)),
        ],
        out_specs=pl.BlockSpec((1, seq, GROUP_WIDTH), lambda b, g: (b, 0, g)),
        scratch_shapes=[
            pltpu.VMEM((seq, d), BF16),
            pltpu.VMEM((2, seq, LANES), F32),
            pltpu.VMEM((2, seq, LANES), F32),
            pltpu.VMEM((2, seq, LANES), F32),
            pltpu.VMEM((len(DILATED_BRANCHES), 2, seq, LANES), F32),
            pltpu.VMEM((len(DILATED_BRANCHES), 2, seq, LANES), F32),
            pltpu.VMEM((2, seq + 8, LANES), F32),
        ],
        compiler_params=pltpu.CompilerParams(
            dimension_semantics=("arbitrary", "arbitrary"), vmem_limit_bytes=VMEM_LIMIT_BYTES),
        name="mix",
    )(x, w_groups, bias_tbl, conv_w)


def _layer_norm(h, g, b):
    mu = jnp.mean(h, axis=-1, keepdims=True)
    c = h - mu
    var = jnp.mean(c * c, axis=-1, keepdims=True)
    return c * lax.rsqrt(var + LN_EPS) * g + b


def _store_row_tiles(ref, val):
    n = val.shape[0]
    for s in range(ROW_TILE):
        ref[pl.ds(s, n, stride=ROW_TILE), :] = val[:, s * LANES:(s + 1) * LANES]


def _load_row_tiles(ref, n):
    return jnp.concatenate([ref[pl.ds(s, n, stride=ROW_TILE), :] for s in range(ROW_TILE)], axis=1)


def _first_argmax_rows(v, row_ids, n_rows):
    m = jnp.max(v, axis=0, keepdims=True)
    idx = jnp.min(jnp.where(v == m, row_ids, n_rows), axis=0, keepdims=True)
    return m, idx


def _post_kernel(mix_ref, x_ref, wo_ref, g_ref, b_ref, wrh_ref, wrl_ref, rb_ref,
                 x1_ref, e_ref, rank_ref, gate_ref, cnt_ref, carry_s):
    i = pl.program_id(0)
    tm = x_ref.shape[0]

    @pl.when(i == 0)
    def _init():
        carry_s[...] = jnp.zeros_like(carry_s)

    h = DEEPNORM_ALPHA * x_ref[...] + jnp.dot(mix_ref[...], wo_ref[...], preferred_element_type=F32)
    x1 = _layer_norm(h, g_ref[...], b_ref[...])
    _store_row_tiles(x1_ref, x1)

    x_hi = x1.astype(BF16)
    x_lo = (x1 - x_hi.astype(F32)).astype(BF16)
    dn = (((1,), (1,)), ((), ()))
    logits = lax.dot_general(wrh_ref[...], x_hi, dn, preferred_element_type=F32)
    logits = logits + lax.dot_general(wrh_ref[...], x_lo, dn, preferred_element_type=F32)
    logits = logits + lax.dot_general(wrl_ref[...], x_hi, dn, preferred_element_type=F32)
    scores = jax.nn.sigmoid(logits)
    biased = scores + rb_ref[...]

    neg_inf = -jnp.inf
    sub_ids = lax.broadcasted_iota(I32, (GROUP_SIZE, tm), 0)
    gs_rows = []
    for gi in range(N_EXPERT_GROUPS):
        bg = biased[gi * GROUP_SIZE:(gi + 1) * GROUP_SIZE]
        m1, i1 = _first_argmax_rows(bg, sub_ids, GROUP_SIZE)
        m2 = jnp.max(jnp.where(sub_ids == i1, neg_inf, bg), axis=0, keepdims=True)
        gs_rows.append(m1 + m2)
    gscore = jnp.concatenate(gs_rows, axis=0)
    grp_ids = lax.broadcasted_iota(I32, (N_EXPERT_GROUPS, tm), 0)
    keep_g = jnp.zeros((N_EXPERT_GROUPS, tm), jnp.bool_)
    for _ in range(TOPK_GROUPS):
        _, gi1 = _first_argmax_rows(gscore, grp_ids, N_EXPERT_GROUPS)
        hit = grp_ids == gi1
        keep_g = jnp.logical_or(keep_g, hit)
        gscore = jnp.where(hit, neg_inf, gscore)
    keep_f = keep_g.astype(F32)
    keep_e = jnp.concatenate(
        [jnp.broadcast_to(keep_f[gi:gi + 1], (GROUP_SIZE, tm)) for gi in range(N_EXPERT_GROUPS)], axis=0)
    masked = jnp.where(keep_e > 0.5, biased, neg_inf)

    exp_ids = lax.broadcasted_iota(I32, (N_EXPERTS, tm), 0)
    sel = jnp.zeros((N_EXPERTS, tm), F32)
    e_rows, g_rows = [], []
    for _ in range(TOP_K):
        _, e1 = _first_argmax_rows(masked, exp_ids, N_EXPERTS)
        hit = exp_ids == e1
        sel = jnp.where(hit, 1.0, sel)
        g_rows.append(jnp.sum(jnp.where(hit, scores, 0.0), axis=0, keepdims=True))
        e_rows.append(e1)
        masked = jnp.where(hit, neg_inf, masked)
    gates = jnp.concatenate(g_rows, axis=0)
    gates = gates / jnp.sum(gates, axis=0, keepdims=True) * ROUTED_SCALE
    gate_ref[...] = gates
    e_ref[...] = jnp.concatenate(e_rows, axis=0)

    t_row = lax.broadcasted_iota(I32, (tm, tm), 0)
    t_col = lax.broadcasted_iota(I32, (tm, tm), 1)
    upper = (t_row < t_col).astype(BF16)
    excl = jnp.dot(sel.astype(BF16), upper, preferred_element_type=F32)
    base = carry_s[...] + excl
    r_rows = [jnp.sum(jnp.where(exp_ids == e1, base, 0.0), axis=0, keepdims=True) for e1 in e_rows]
    rank_ref[...] = jnp.concatenate(r_rows, axis=0).astype(I32)
    carry = carry_s[...] + jnp.sum(sel, axis=1, keepdims=True)
    carry_s[...] = carry
    cnt_ref[...] = carry.astype(I32)


def _post_call(mix2d, x2d, wo, ln_g, ln_b, wr_hi, wr_lo, rb):
    n_tok, d = x2d.shape
    tm = POST_TILE
    tok_spec = pl.BlockSpec((tm, d), lambda i: (i, 0))
    k_spec = pl.BlockSpec((TOP_K, tm), lambda i: (0, i))
    full = lambda a: pl.BlockSpec(a.shape, lambda i: (0,) * a.ndim)
    return pl.pallas_call(
        _post_kernel,
        out_shape=(
            jax.ShapeDtypeStruct((n_tok * ROW_TILE, LANES), F32),
            jax.ShapeDtypeStruct((TOP_K, n_tok), I32),
            jax.ShapeDtypeStruct((TOP_K, n_tok), I32),
            jax.ShapeDtypeStruct((TOP_K, n_tok), F32),
            jax.ShapeDtypeStruct((N_EXPERTS, 1), I32),
        ),
        grid=(n_tok // tm,),
        in_specs=[tok_spec, tok_spec, full(wo), full(ln_g), full(ln_b), full(wr_hi), full(wr_lo), full(rb)],
        out_specs=(pl.BlockSpec((tm * ROW_TILE, LANES), lambda i: (i, 0)), k_spec, k_spec, k_spec,
                   pl.BlockSpec((N_EXPERTS, 1), lambda i: (0, 0))),
        scratch_shapes=[pltpu.VMEM((N_EXPERTS, 1), F32)],
        compiler_params=pltpu.CompilerParams(
            dimension_semantics=("arbitrary",), vmem_limit_bytes=VMEM_LIMIT_BYTES),
        name="post",
    )(mix2d, x2d, wo, ln_g, ln_b, wr_hi, wr_lo, rb)


def _pos_kernel(e_ref, rank_ref, ps_ref, pos_ref):
    tp = e_ref.shape[1]
    exp_ids = lax.broadcasted_iota(I32, (N_EXPERTS, tp), 0)
    ps = ps_ref[...]
    rows = []
    for k in range(TOP_K):
        start = jnp.sum(jnp.where(exp_ids == e_ref[k:k + 1, :], ps, 0), axis=0, keepdims=True)
        rows.append((start + rank_ref[k:k + 1, :]) * ROW_TILE)
    pos_ref[...] = jnp.concatenate(rows, axis=0)


def _pos_call(e_idx, rank, pstarts):
    n_tok = e_idx.shape[1]
    tp = POS_TILE
    k_spec = pl.BlockSpec((TOP_K, tp), lambda i: (0, i))
    return pl.pallas_call(
        _pos_kernel,
        out_shape=jax.ShapeDtypeStruct((TOP_K, n_tok), I32),
        grid=(n_tok // tp,),
        in_specs=[k_spec, k_spec, pl.BlockSpec((N_EXPERTS, 1), lambda i: (0, 0))],
        out_specs=k_spec,
        compiler_params=pltpu.CompilerParams(dimension_semantics=("arbitrary",)),
        name="positions",
    )(e_idx, rank, pstarts)


def _dispatch_kernel(cnt_ref, ps_ref, pos_ref, x_ref, xs_ref, zbuf, sem, zsem):
    i = pl.program_id(0)
    tt = x_ref.shape[0] // ROW_TILE

    def pad_copies(e, fn):
        cnt = cnt_ref[e]
        pad = jnp.bitwise_and(-cnt, MOE_BLOCK - 1)
        off = ps_ref[e] + cnt
        for b in range(int(math.log2(MOE_BLOCK))):
            size = 1 << b
            hit = jnp.bitwise_and(pad, size)

            @pl.when(hit != 0)
            def _(off=off, size=size):
                fn(pltpu.make_async_copy(
                    zbuf.at[pl.ds(0, size * ROW_TILE)],
                    xs_ref.at[pl.ds(pl.multiple_of(off * ROW_TILE, ROW_TILE), size * ROW_TILE)], zsem))
            off = off + hit

    @pl.when(i == 0)
    def _zero_padding_rows():
        zbuf[...] = jnp.zeros_like(zbuf)

        def start(e, carry):
            pad_copies(e, lambda cp: cp.start())
            return carry

        def wait(e, carry):
            pad_copies(e, lambda cp: cp.wait())
            return carry
        lax.fori_loop(0, N_EXPERTS, start, 0)
        lax.fori_loop(0, N_EXPERTS, wait, 0)

    def row_copy(t, k):
        src = x_ref.at[pl.ds(pl.multiple_of(t * ROW_TILE, ROW_TILE), ROW_TILE)]
        dst = xs_ref.at[pl.ds(pl.multiple_of(pos_ref[k, t], ROW_TILE), ROW_TILE)]
        return pltpu.make_async_copy(src, dst, sem)

    def issue(t, carry):
        for k in range(TOP_K):
            row_copy(t, k).start(priority=k % 2)
        return carry
    lax.fori_loop(0, tt, issue, 0)

    def drain(t, carry):
        for k in range(TOP_K):
            row_copy(t, k).wait()
        return carry
    lax.fori_loop(0, tt, drain, 0)


def _dispatch_call(counts, pstarts, pos, x1r, n_rows):
    n_tok = pos.shape[1]
    tt = DISPATCH_TILE
    grid_spec = pltpu.PrefetchScalarGridSpec(
        num_scalar_prefetch=2,
        grid=(n_tok // tt,),
        in_specs=[
            pl.BlockSpec((TOP_K, tt), lambda i, c, p: (0, i), memory_space=pltpu.SMEM),
            pl.BlockSpec((tt * ROW_TILE, LANES), lambda i, c, p: (i, 0)),
        ],
        out_specs=pl.BlockSpec(memory_space=pl.ANY),
        scratch_shapes=[
            pltpu.VMEM((MOE_BLOCK // 2 * ROW_TILE, LANES), F32),
            pltpu.SemaphoreType.DMA(()),
            pltpu.SemaphoreType.DMA(()),
        ],
    )
    return pl.pallas_call(
        _dispatch_kernel,
        out_shape=jax.ShapeDtypeStruct((n_rows * ROW_TILE, LANES), F32),
        grid_spec=grid_spec,
        compiler_params=pltpu.CompilerParams(
            dimension_semantics=("arbitrary",), vmem_limit_bytes=VMEM_LIMIT_BYTES),
        name="dispatch",
    )(counts, pstarts, pos, x1r)


def _experts_kernel(be_ref, nb_ref, xs_ref, wg_ref, wu_ref, wd_ref, y_ref):
    i = pl.program_id(0)

    @pl.when(i < nb_ref[0])
    def _compute():
        xb = _load_row_tiles(xs_ref, MOE_BLOCK).astype(BF16)
        hg = jnp.dot(xb, wg_ref[0].astype(BF16), preferred_element_type=F32)
        hu = jnp.dot(xb, wu_ref[0].astype(BF16), preferred_element_type=F32)
        hh = (hg * jax.nn.sigmoid(hg) * hu).astype(BF16)
        _store_row_tiles(y_ref, jnp.dot(hh, wd_ref[0].astype(BF16), preferred_element_type=F32))


def _experts_call(block_e, n_used, xs, w_gate, w_up, w_down):
    d = D_MODEL
    n_blocks = xs.shape[0] // (MOE_BLOCK * ROW_TILE)

    def blk(i, be, nb):
        return (jnp.minimum(i, nb[0] - 1), 0)

    def wsel(i, be, nb):
        return (be[jnp.minimum(i, nb[0] - 1)], 0, 0)
    grid_spec = pltpu.PrefetchScalarGridSpec(
        num_scalar_prefetch=2,
        grid=(n_blocks,),
        in_specs=[
            pl.BlockSpec((MOE_BLOCK * ROW_TILE, LANES), blk),
            pl.BlockSpec((1, d, EXPERT_DIM), wsel),
            pl.BlockSpec((1, d, EXPERT_DIM), wsel),
            pl.BlockSpec((1, EXPERT_DIM, d), wsel),
        ],
        out_specs=pl.BlockSpec((MOE_BLOCK * ROW_TILE, LANES), blk),
    )
    return pl.pallas_call(
        _experts_kernel,
        out_shape=jax.ShapeDtypeStruct(xs.shape, F32),
        grid_spec=grid_spec,
        compiler_params=pltpu.CompilerParams(
            dimension_semantics=("arbitrary",), vmem_limit_bytes=VMEM_LIMIT_BYTES),
        name="experts",
    )(block_e, n_used, xs, w_gate, w_up, w_down)


def _combine_kernel(pos_ref, gate_ref, x1_ref, y_ref, sg_ref, su_ref, sd_ref, g_ref, b_ref, out_ref, buf, sem):
    tt = x1_ref.shape[0] // ROW_TILE

    def row_copy(t, k):
        src = y_ref.at[pl.ds(pl.multiple_of(pos_ref[k, t], ROW_TILE), ROW_TILE)]
        dst = buf.at[k, pl.ds(pl.multiple_of(t * ROW_TILE, ROW_TILE), ROW_TILE)]
        return pltpu.make_async_copy(src, dst, sem)

    def issue(t, carry):
        for k in range(TOP_K):
            row_copy(t, k).start(priority=k % 2)
        return carry
    lax.fori_loop(0, tt, issue, 0)

    x1 = _load_row_tiles(x1_ref, tt)
    xb = x1.astype(BF16)
    hg = jnp.dot(xb, sg_ref[...], preferred_element_type=F32)
    hu = jnp.dot(xb, su_ref[...], preferred_element_type=F32)
    hh = (hg * jax.nn.sigmoid(hg) * hu).astype(BF16)
    acc = DEEPNORM_ALPHA * x1 + jnp.dot(hh, sd_ref[...], preferred_element_type=F32)

    def drain(t, carry):
        for k in range(TOP_K):
            row_copy(t, k).wait()
        return carry
    lax.fori_loop(0, tt, drain, 0)

    gates = gate_ref[...]
    for k in range(TOP_K):
        acc = acc + gates[:, k:k + 1] * _load_row_tiles(buf.at[k], tt)
    out_ref[...] = _layer_norm(acc, g_ref[...], b_ref[...])


def _combine_call(pos, gates_t, x1r, y, sg, su, sd, ln_g, ln_b):
    n_tok, d = pos.shape[1], D_MODEL
    tt = COMBINE_TILE
    full = lambda a: pl.BlockSpec(a.shape, lambda i: (0,) * a.ndim)
    return pl.pallas_call(
        _combine_kernel,
        out_shape=jax.ShapeDtypeStruct((n_tok, d), F32),
        grid=(n_tok // tt,),
        in_specs=[
            pl.BlockSpec((TOP_K, tt), lambda i: (0, i), memory_space=pltpu.SMEM),
            pl.BlockSpec((tt, TOP_K), lambda i: (i, 0)),
            pl.BlockSpec((tt * ROW_TILE, LANES), lambda i: (i, 0)),
            pl.BlockSpec(memory_space=pl.ANY),
            full(sg), full(su), full(sd), full(ln_g), full(ln_b),
        ],
        out_specs=pl.BlockSpec((tt, d), lambda i: (i, 0)),
        scratch_shapes=[pltpu.VMEM((TOP_K, tt * ROW_TILE, LANES), F32), pltpu.SemaphoreType.DMA(())],
        compiler_params=pltpu.CompilerParams(
            dimension_semantics=("arbitrary",), vmem_limit_bytes=VMEM_LIMIT_BYTES),
        name="combine",
    )(pos, gates_t, x1r, y, sg, su, sd, ln_g, ln_b)


def _group_weights(w_in):
    blocks = w_in.reshape(D_MODEL, 12, GROUP_WIDTH)
    scale = HEAD_DIM ** -0.5
    groups = []
    for g in range(N_ATTN_GROUPS):
        groups.append(jnp.concatenate(
            [blocks[:, g] * scale, blocks[:, N_ATTN_GROUPS + g], blocks[:, 2 * N_ATTN_GROUPS + g]], axis=1))
    groups.append(jnp.concatenate([blocks[:, 9], blocks[:, 10], blocks[:, 11]], axis=1))
    return jnp.stack(groups).astype(BF16)


def kernel(x, w_in, conv_w, w_o, ln1_g, ln1_b, rel_bias, w_router, router_bias, exp_w_gate, exp_w_up,
           exp_w_down, sh_w_gate, sh_w_up, sh_w_down, ln2_g, ln2_b):
    batch, seq, d = x.shape
    assert d == D_MODEL and seq == DILATED_BRANCHES[-1][0] and w_in.shape[0] == DEPTH
    n_tok = batch * seq
    assert n_tok % POST_TILE == 0 and n_tok % DISPATCH_TILE == 0 and n_tok % COMBINE_TILE == 0

    mix = _mix_call(x, _group_weights(w_in[0]), _bias_table(rel_bias), conv_w[0])

    wr = w_router[0]
    wr_hi = wr.astype(BF16)
    wr_lo = (wr - wr_hi.astype(F32)).astype(BF16)
    x2d = x.reshape(n_tok, d)
    x1, e_idx, rank, gates, counts = _post_call(
        mix.reshape(n_tok, d), x2d, w_o[0].astype(BF16), ln1_g, ln1_b,
        wr_hi.T, wr_lo.T, router_bias[0].reshape(N_EXPERTS, 1))

    counts = counts[:, 0]
    padded = (counts + MOE_BLOCK - 1) // MOE_BLOCK * MOE_BLOCK
    pends = jnp.cumsum(padded)
    pstarts = (pends - padded).astype(I32)
    n_blocks = n_tok * TOP_K // MOE_BLOCK + N_EXPERTS
    block_start = jnp.arange(n_blocks, dtype=I32) * MOE_BLOCK
    block_e = jnp.sum((pends[None, :] <= block_start[:, None]).astype(I32), axis=1)
    block_e = jnp.minimum(block_e, N_EXPERTS - 1).astype(I32)
    n_used = (pends[-1:] // MOE_BLOCK).astype(I32)
    pos = _pos_call(e_idx, rank, pstarts.reshape(N_EXPERTS, 1))

    xs = _dispatch_call(counts, pstarts, pos, x1, n_blocks * MOE_BLOCK)
    y = _experts_call(block_e, n_used, xs, exp_w_gate[0], exp_w_up[0], exp_w_down[0])

    out = _combine_call(pos, gates.T, x1, y, sh_w_gate[0].astype(BF16), sh_w_up[0].astype(BF16),
                        sh_w_down[0].astype(BF16), ln2_g, ln2_b)
    return out.reshape(batch, seq, d)
```

```python
import functools
import math

import numpy as np
import jax
import jax.numpy as jnp
from jax import lax
from jax.experimental import pallas as pl
from jax.experimental.pallas import tpu as pltpu

F32 = jnp.float32
BF16 = jnp.bfloat16
I32 = jnp.int32
U32 = jnp.uint32

D_MODEL = 1024
HEAD_DIM = 64
N_ATTN_HEADS = 12
ATTN_WIDTH = N_ATTN_HEADS * HEAD_DIM
CONV_WIDTH = D_MODEL - ATTN_WIDTH
CONV_K = 3
DILATED_BRANCHES = ((128, 1), (512, 4), (2048, 16))
ATTN_BLOCK = 128
N_REL_BUCKETS = 32
REL_MAX_DISTANCE = 2048
N_EXPERTS = 256
TOP_K = 8
N_EXPERT_GROUPS = 8
TOPK_GROUPS = 4
GROUP_SIZE = N_EXPERTS // N_EXPERT_GROUPS
EXPERT_DIM = 256
SHARED_DIM = 256
ROUTED_SCALE = 2.5
DEPTH = 1
DEEPNORM_ALPHA = (2.0 * DEPTH) ** 0.25
LN_EPS = 1e-5

LANES = 128
GROUP_WIDTH = 2 * LANES
HEADS_PER_GROUP = GROUP_WIDTH // HEAD_DIM
N_GROUPS = D_MODEL // GROUP_WIDTH
N_ATTN_GROUPS = ATTN_WIDTH // GROUP_WIDTH
VMEM_LIMIT_BYTES = 56 * 1024 * 1024

MASK_VALUE = -1e30

ROW_CHUNK = 256
ATTN_UNROLL = 2
ROW_TILE = 4
POS_TILE = 2048
POST_TILE = 512
MOE_BLOCK = 256
DISPATCH_TILE = 256
COMBINE_TILE = 256


def _t5_bucket(dist):
    max_exact = N_REL_BUCKETS // 2
    dist = np.asarray(dist)
    log_part = np.log(np.maximum(dist, 1) / max_exact) / math.log(REL_MAX_DISTANCE / max_exact)
    large = max_exact + (log_part * (N_REL_BUCKETS - max_exact)).astype(np.int32)
    large = np.minimum(large, N_REL_BUCKETS - 1)
    return np.where(dist < max_exact, dist, large).astype(np.int32)


def _branch_bucket_and_band(window, dilation):
    w_sub = window // dilation
    qi = np.arange(ATTN_BLOCK)[:, None]
    ki = np.arange(2 * ATTN_BLOCK)[None, :]
    steps = qi + ATTN_BLOCK - ki
    band = (steps >= 0) & (steps <= w_sub)
    bucket = _t5_bucket(np.clip(steps, 0, w_sub) * dilation)
    return bucket, band


def _bias_table(rel_bias):
    tabs = []
    for window, dilation in DILATED_BRANCHES:
        bucket, band = _branch_bucket_and_band(window, dilation)
        onehot = (bucket[:, :, None] == np.arange(N_REL_BUCKETS)).astype(np.float32)
        b = jnp.einsum("qkb,bh->qkh", onehot, rel_bias.astype(F32), precision=lax.Precision.HIGHEST)
        b = jnp.where(band[:, :, None], b, MASK_VALUE)
        tabs.append(b.transpose(2, 0, 1))
    t = jnp.stack(tabs, axis=1)
    t = t.reshape(N_ATTN_GROUPS, HEADS_PER_GROUP, len(DILATED_BRANCHES), ATTN_BLOCK, 2 * ATTN_BLOCK)
    t = t.transpose(0, 2, 1, 3, 4)
    return t.reshape(N_ATTN_GROUPS, len(DILATED_BRANCHES), HEADS_PER_GROUP * ATTN_BLOCK, 2 * ATTN_BLOCK)


def _lane_head():
    return lax.shift_right_logical(lax.broadcasted_iota(I32, (ATTN_BLOCK, GROUP_WIDTH), 1), 6)


def _attn_step(it, bi, dilation, first, q_s, k_s, v_s, bias_ref, o_s, lse_s):
    logd = int(math.log2(dilation))
    r = jnp.bitwise_and(it, dilation - 1)
    n = lax.shift_right_logical(it, logd)
    start = r + (dilation * ATTN_BLOCK) * n

    def rows(st):
        return pl.ds(st, ATTN_BLOCK) if dilation == 1 else pl.ds(st, ATTN_BLOCK, stride=dilation)

    def ld(ref, st):
        return jnp.concatenate([ref[0, rows(st), :], ref[1, rows(st), :]], axis=1)

    qf = ld(q_s, start)
    if first:
        kk = ld(k_s, start).astype(BF16)
        vv = ld(v_s, start).astype(BF16)
        bias = bias_ref[0, bi, :, ATTN_BLOCK:]
    else:
        prev = start - dilation * ATTN_BLOCK
        kk = jnp.concatenate([ld(k_s, prev), ld(k_s, start)], axis=0).astype(BF16)
        vv = jnp.concatenate([ld(v_s, prev), ld(v_s, start)], axis=0).astype(BF16)
        bias = bias_ref[0, bi]
    lane_head = _lane_head()
    q4 = jnp.concatenate(
        [jnp.where(lane_head == h, qf, 0.0) for h in range(HEADS_PER_GROUP)], axis=0).astype(BF16)
    s = lax.dot_general(q4, kk, (((1,), (1,)), ((), ())), preferred_element_type=F32) + bias
    m = jnp.max(s, axis=-1, keepdims=True)
    p = jnp.exp(s - m)
    l = jnp.sum(p, axis=-1, keepdims=True)
    pv = jnp.dot(p.astype(BF16), vv, preferred_element_type=F32)
    pvn = pv * (1.0 / l)
    lse = m + jnp.log(l)
    o = jnp.zeros((ATTN_BLOCK, GROUP_WIDTH), F32)
    lb = jnp.zeros((ATTN_BLOCK, GROUP_WIDTH), F32)
    for h in range(HEADS_PER_GROUP):
        sel = lane_head == h
        o = jnp.where(sel, pvn[h * ATTN_BLOCK:(h + 1) * ATTN_BLOCK], o)
        lb = jnp.where(sel, lse[h * ATTN_BLOCK:(h + 1) * ATTN_BLOCK], lb)
    for sl in range(2):
        o_s[bi, sl, rows(start), :] = o[:, sl * LANES:(sl + 1) * LANES]
        lse_s[bi, sl, rows(start), :] = lb[:, sl * LANES:(sl + 1) * LANES]


def _mix_kernel(x_ref, w_ref, bias_ref, cw_ref, out_ref, xb_s, q_s, k_s, v_s, o_s, lse_s, u_s):
    g = pl.program_id(1)
    seq = x_ref.shape[1]
    n_chunks = seq // ROW_CHUNK

    @pl.when(g == 0)
    def _cast_x():
        def body(c, carry):
            rows = pl.ds(pl.multiple_of(c * ROW_CHUNK, ROW_CHUNK), ROW_CHUNK)
            xb_s[rows, :] = x_ref[0, rows, :].astype(BF16)
            return carry
        lax.fori_loop(0, n_chunks, body, 0)

    def proj(c, carry):
        rows = pl.ds(pl.multiple_of(c * ROW_CHUNK, ROW_CHUNK), ROW_CHUNK)
        res = jnp.dot(xb_s[rows, :], w_ref[0], preferred_element_type=F32)
        for j, dst in enumerate((q_s, k_s, v_s)):
            for sl in range(2):
                lo = j * GROUP_WIDTH + sl * LANES
                dst[sl, rows, :] = res[:, lo:lo + LANES]
        return carry
    lax.fori_loop(0, n_chunks, proj, 0)

    @pl.when(g < N_ATTN_GROUPS)
    def _attention():
        for bi, (window, dilation) in enumerate(DILATED_BRANCHES):
            n_blocks = seq // dilation // ATTN_BLOCK
            n_steps = n_blocks * dilation
            step = functools.partial(_attn_step, bi=bi, dilation=dilation, q_s=q_s, k_s=k_s, v_s=v_s,
                                     bias_ref=bias_ref, o_s=o_s, lse_s=lse_s)

            def first_body(it, carry, step=step):
                step(it, first=True)
                return carry

            def rest_body(it, carry, step=step):
                step(it, first=False)
                return carry
            lax.fori_loop(0, dilation, first_body, 0, unroll=ATTN_UNROLL)
            if n_steps > dilation:
                lax.fori_loop(dilation, n_steps, rest_body, 0, unroll=ATTN_UNROLL)

        def combine(c, carry):
            rows = pl.ds(pl.multiple_of(c * ROW_CHUNK, ROW_CHUNK), ROW_CHUNK)
            for sl in range(2):
                ls = [lse_s[bi, sl, rows, :] for bi in range(len(DILATED_BRANCHES))]
                mx = jnp.maximum(jnp.maximum(ls[0], ls[1]), ls[2])
                ws = [jnp.exp(v - mx) for v in ls]
                den = ws[0] + ws[1] + ws[2]
                num = ws[0] * o_s[0, sl, rows, :] + ws[1] * o_s[1, sl, rows, :] + ws[2] * o_s[2, sl, rows, :]
                out_ref[0, rows, sl * LANES:(sl + 1) * LANES] = (num / den).astype(BF16)
            return carry
        lax.fori_loop(0, n_chunks, combine, 0)

    @pl.when(g == N_ATTN_GROUPS)
    def _short_conv():
        pad = 8
        for sl in range(2):
            u_s[sl, 0:pad, :] = jnp.zeros((pad, LANES), F32)
            for c in range(n_chunks):
                lo = c * ROW_CHUNK
                u_s[sl, pad + lo:pad + lo + ROW_CHUNK, :] = v_s[sl, lo:lo + ROW_CHUNK, :] * q_s[sl, lo:lo + ROW_CHUNK, :]
            w = [cw_ref[kk:kk + 1, sl * LANES:(sl + 1) * LANES] for kk in range(CONV_K)]
            for c in range(n_chunks):
                lo = c * ROW_CHUNK
                y = w[2] * u_s[sl, pad + lo:pad + lo + ROW_CHUNK, :]
                y = y + w[1] * u_s[sl, pad + lo - 1:pad + lo - 1 + ROW_CHUNK, :]
                y = y + w[0] * u_s[sl, pad + lo - 2:pad + lo - 2 + ROW_CHUNK, :]
                out_ref[0, lo:lo + ROW_CHUNK, sl * LANES:(sl + 1) * LANES] = (
                    k_s[sl, lo:lo + ROW_CHUNK, :] * y).astype(BF16)


def _mix_call(x, w_groups, bias_tbl, conv_w):
    batch, seq, d = x.shape
    return pl.pallas_call(
        _mix_kernel,
        out_shape=jax.ShapeDtypeStruct((batch, seq, d), BF16),
        grid=(batch, N_GROUPS),
        in_specs=[
            pl.BlockSpec((1, seq, d), lambda b, g: (b, 0, 0)),
            pl.BlockSpec((1, d, 3 * GROUP_WIDTH), lambda b, g: (g, 0, 0)),
            pl.BlockSpec((1,) + bias_tbl.shape[1:], lambda b, g: (jnp.minimum(g, N_ATTN_GROUPS - 1), 0, 0, 0)),
            pl.BlockSpec(conv_w.shape, lambda b, g: (0, 0)),
        ],
        out_specs=pl.BlockSpec((1, seq, GROUP_WIDTH), lambda b, g: (b, 0, g)),
        scratch_shapes=[
            pltpu.VMEM((seq, d), BF16),
            pltpu.VMEM((2, seq, LANES), F32),
            pltpu.VMEM((2, seq, LANES), F32),
            pltpu.VMEM((2, seq, LANES), F32),
            pltpu.VMEM((len(DILATED_BRANCHES), 2, seq, LANES), F32),
            pltpu.VMEM((len(DILATED_BRANCHES), 2, seq, LANES), F32),
            pltpu.VMEM((2, seq + 8, LANES), F32),
        ],
        compiler_params=pltpu.CompilerParams(
            dimension_semantics=("arbitrary", "arbitrary"), vmem_limit_bytes=VMEM_LIMIT_BYTES),
        name="mix",
    )(x, w_groups, bias_tbl, conv_w)


def _layer_norm(h, g, b):
    mu = jnp.mean(h, axis=-1, keepdims=True)
    c = h - mu
    var = jnp.mean(c * c, axis=-1, keepdims=True)
    return c * lax.rsqrt(var + LN_EPS) * g + b


def _bf16_bits(v):
    return lax.bitcast_convert_type(v.astype(BF16).astype(F32), U32)


def _store_row_tiles(ref, val):
    n = val.shape[0]
    for j in range(ROW_TILE):
        lo = _bf16_bits(val[:, (2 * j) * LANES:(2 * j + 1) * LANES])
        hi = _bf16_bits(val[:, (2 * j + 1) * LANES:(2 * j + 2) * LANES])
        ref[pl.ds(j, n, stride=ROW_TILE), :] = jnp.bitwise_or(lax.shift_right_logical(lo, jnp.uint32(16)), hi)


def _load_row_tiles(ref, n):
    pieces = []
    for j in range(ROW_TILE):
        w = ref[pl.ds(j, n, stride=ROW_TILE), :]
        pieces.append(lax.bitcast_convert_type(lax.shift_left(w, jnp.uint32(16)), F32))
        pieces.append(lax.bitcast_convert_type(jnp.bitwise_and(w, jnp.uint32(0xFFFF0000)), F32))
    return jnp.concatenate(pieces, axis=1)


def _first_argmax_rows(v, row_ids, n_rows):
    m = jnp.max(v, axis=0, keepdims=True)
    idx = jnp.min(jnp.where(v == m, row_ids, n_rows), axis=0, keepdims=True)
    return m, idx


def _post_kernel(mix_ref, x_ref, wo_ref, g_ref, b_ref, wrh_ref, wrl_ref, rb_ref,
                 x1_ref, x1p_ref, e_ref, rank_ref, gate_ref, cnt_ref, carry_s):
    i = pl.program_id(0)
    tm = x_ref.shape[0]

    @pl.when(i == 0)
    def _init():
        carry_s[...] = jnp.zeros_like(carry_s)

    h = DEEPNORM_ALPHA * x_ref[...] + jnp.dot(mix_ref[...], wo_ref[...], preferred_element_type=F32)
    x1 = _layer_norm(h, g_ref[...], b_ref[...])
    x1_ref[...] = x1
    _store_row_tiles(x1p_ref, x1)

    x_hi = x1.astype(BF16)
    x_lo = (x1 - x_hi.astype(F32)).astype(BF16)
    dn = (((1,), (1,)), ((), ()))
    logits = lax.dot_general(wrh_ref[...], x_hi, dn, preferred_element_type=F32)
    logits = logits + lax.dot_general(wrh_ref[...], x_lo, dn, preferred_element_type=F32)
    logits = logits + lax.dot_general(wrl_ref[...], x_hi, dn, preferred_element_type=F32)
    scores = jax.nn.sigmoid(logits)
    biased = scores + rb_ref[...]

    neg_inf = -jnp.inf
    sub_ids = lax.broadcasted_iota(I32, (GROUP_SIZE, tm), 0)
    gs_rows = []
    for gi in range(N_EXPERT_GROUPS):
        bg = biased[gi * GROUP_SIZE:(gi + 1) * GROUP_SIZE]
        m1, i1 = _first_argmax_rows(bg, sub_ids, GROUP_SIZE)
        m2 = jnp.max(jnp.where(sub_ids == i1, neg_inf, bg), axis=0, keepdims=True)
        gs_rows.append(m1 + m2)
    gscore = jnp.concatenate(gs_rows, axis=0)
    grp_ids = lax.broadcasted_iota(I32, (N_EXPERT_GROUPS, tm), 0)
    keep_g = jnp.zeros((N_EXPERT_GROUPS, tm), jnp.bool_)
    for _ in range(TOPK_GROUPS):
        _, gi1 = _first_argmax_rows(gscore, grp_ids, N_EXPERT_GROUPS)
        hit = grp_ids == gi1
        keep_g = jnp.logical_or(keep_g, hit)
        gscore = jnp.where(hit, neg_inf, gscore)
    keep_f = keep_g.astype(F32)
    keep_e = jnp.concatenate(
        [jnp.broadcast_to(keep_f[gi:gi + 1], (GROUP_SIZE, tm)) for gi in range(N_EXPERT_GROUPS)], axis=0)
    masked = jnp.where(keep_e > 0.5, biased, neg_inf)

    exp_ids = lax.broadcasted_iota(I32, (N_EXPERTS, tm), 0)
    sel = jnp.zeros((N_EXPERTS, tm), F32)
    e_rows, g_rows = [], []
    for _ in range(TOP_K):
        _, e1 = _first_argmax_rows(masked, exp_ids, N_EXPERTS)
        hit = exp_ids == e1
        sel = jnp.where(hit, 1.0, sel)
        g_rows.append(jnp.sum(jnp.where(hit, scores, 0.0), axis=0, keepdims=True))
        e_rows.append(e1)
        masked = jnp.where(hit, neg_inf, masked)
    gates = jnp.concatenate(g_rows, axis=0)
    gates = gates / jnp.sum(gates, axis=0, keepdims=True) * ROUTED_SCALE
    gate_ref[...] = gates
    e_ref[...] = jnp.concatenate(e_rows, axis=0)

    t_row = lax.broadcasted_iota(I32, (tm, tm), 0)
    t_col = lax.broadcasted_iota(I32, (tm, tm), 1)
    upper = (t_row < t_col).astype(BF16)
    excl = jnp.dot(sel.astype(BF16), upper, preferred_element_type=F32)
    base = carry_s[...] + excl
    r_rows = [jnp.sum(jnp.where(exp_ids == e1, base, 0.0), axis=0, keepdims=True) for e1 in e_rows]
    rank_ref[...] = jnp.concatenate(r_rows, axis=0).astype(I32)
    carry = carry_s[...] + jnp.sum(sel, axis=1, keepdims=True)
    carry_s[...] = carry
    cnt_ref[...] = carry.astype(I32)


def _post_call(mix2d, x2d, wo, ln_g, ln_b, wr_hi, wr_lo, rb):
    n_tok, d = x2d.shape
    tm = POST_TILE
    tok_spec = pl.BlockSpec((tm, d), lambda i: (i, 0))
    k_spec = pl.BlockSpec((TOP_K, tm), lambda i: (0, i))
    full = lambda a: pl.BlockSpec(a.shape, lambda i: (0,) * a.ndim)
    return pl.pallas_call(
        _post_kernel,
        out_shape=(
            jax.ShapeDtypeStruct((n_tok, d), F32),
            jax.ShapeDtypeStruct((n_tok * ROW_TILE, LANES), U32),
            jax.ShapeDtypeStruct((TOP_K, n_tok), I32),
            jax.ShapeDtypeStruct((TOP_K, n_tok), I32),
            jax.ShapeDtypeStruct((TOP_K, n_tok), F32),
            jax.ShapeDtypeStruct((N_EXPERTS, 1), I32),
        ),
        grid=(n_tok // tm,),
        in_specs=[tok_spec, tok_spec, full(wo), full(ln_g), full(ln_b), full(wr_hi), full(wr_lo), full(rb)],
        out_specs=(tok_spec, pl.BlockSpec((tm * ROW_TILE, LANES), lambda i: (i, 0)), k_spec, k_spec, k_spec,
                   pl.BlockSpec((N_EXPERTS, 1), lambda i: (0, 0))),
        scratch_shapes=[pltpu.VMEM((N_EXPERTS, 1), F32)],
        compiler_params=pltpu.CompilerParams(
            dimension_semantics=("arbitrary",), vmem_limit_bytes=VMEM_LIMIT_BYTES),
        name="post",
    )(mix2d, x2d, wo, ln_g, ln_b, wr_hi, wr_lo, rb)


def _pos_kernel(e_ref, rank_ref, ps_ref, pos_ref):
    tp = e_ref.shape[1]
    exp_ids = lax.broadcasted_iota(I32, (N_EXPERTS, tp), 0)
    ps = ps_ref[...]
    rows = []
    for k in range(TOP_K):
        start = jnp.sum(jnp.where(exp_ids == e_ref[k:k + 1, :], ps, 0), axis=0, keepdims=True)
        rows.append((start + rank_ref[k:k + 1, :]) * ROW_TILE)
    pos_ref[...] = jnp.concatenate(rows, axis=0)


def _pos_call(e_idx, rank, pstarts):
    n_tok = e_idx.shape[1]
    tp = POS_TILE
    k_spec = pl.BlockSpec((TOP_K, tp), lambda i: (0, i))
    return pl.pallas_call(
        _pos_kernel,
        out_shape=jax.ShapeDtypeStruct((TOP_K, n_tok), I32),
        grid=(n_tok // tp,),
        in_specs=[k_spec, k_spec, pl.BlockSpec((N_EXPERTS, 1), lambda i: (0, 0))],
        out_specs=k_spec,
        compiler_params=pltpu.CompilerParams(dimension_semantics=("arbitrary",)),
        name="positions",
    )(e_idx, rank, pstarts)


def _dispatch_kernel(cnt_ref, ps_ref, pos_ref, x_ref, xs_ref, zbuf, sem, zsem):
    i = pl.program_id(0)
    tt = x_ref.shape[0] // ROW_TILE

    def pad_copies(e, fn):
        cnt = cnt_ref[e]
        pad = jnp.bitwise_and(-cnt, MOE_BLOCK - 1)
        off = ps_ref[e] + cnt
        for b in range(int(math.log2(MOE_BLOCK))):
            size = 1 << b
            hit = jnp.bitwise_and(pad, size)

            @pl.when(hit != 0)
            def _(off=off, size=size):
                fn(pltpu.make_async_copy(
                    zbuf.at[pl.ds(0, size * ROW_TILE)],
                    xs_ref.at[pl.ds(pl.multiple_of(off * ROW_TILE, ROW_TILE), size * ROW_TILE)], zsem))
            off = off + hit

    @pl.when(i == 0)
    def _zero_padding_rows():
        zbuf[...] = jnp.zeros_like(zbuf)

        def start(e, carry):
            pad_copies(e, lambda cp: cp.start())
            return carry

        def wait(e, carry):
            pad_copies(e, lambda cp: cp.wait())
            return carry
        lax.fori_loop(0, N_EXPERTS, start, 0)
        lax.fori_loop(0, N_EXPERTS, wait, 0)

    def row_copy(t, k):
        src = x_ref.at[pl.ds(t * ROW_TILE, ROW_TILE)]
        dst = xs_ref.at[pl.ds(pl.multiple_of(pos_ref[k, t], ROW_TILE), ROW_TILE)]
        return pltpu.make_async_copy(src, dst, sem)

    for t in range(tt):
        for k in range(TOP_K):
            row_copy(t, k).start(priority=k % 2)
    for t in range(tt):
        for k in range(TOP_K):
            row_copy(t, k).wait()


def _dispatch_call(counts, pstarts, pos, x1r, n_rows):
    n_tok = pos.shape[1]
    tt = DISPATCH_TILE
    grid_spec = pltpu.PrefetchScalarGridSpec(
        num_scalar_prefetch=2,
        grid=(n_tok // tt,),
        in_specs=[
            pl.BlockSpec((TOP_K, tt), lambda i, c, p: (0, i), memory_space=pltpu.SMEM),
            pl.BlockSpec((tt * ROW_TILE, LANES), lambda i, c, p: (i, 0)),
        ],
        out_specs=pl.BlockSpec(memory_space=pl.ANY),
        scratch_shapes=[
            pltpu.VMEM((MOE_BLOCK // 2 * ROW_TILE, LANES), U32),
            pltpu.SemaphoreType.DMA(()),
            pltpu.SemaphoreType.DMA(()),
        ],
    )
    return pl.pallas_call(
        _dispatch_kernel,
        out_shape=jax.ShapeDtypeStruct((n_rows * ROW_TILE, LANES), U32),
        grid_spec=grid_spec,
        compiler_params=pltpu.CompilerParams(
            dimension_semantics=("arbitrary",), vmem_limit_bytes=VMEM_LIMIT_BYTES),
        name="dispatch",
    )(counts, pstarts, pos, x1r)


def _experts_kernel(be_ref, nb_ref, xs_ref, wg_ref, wu_ref, wd_ref, y_ref):
    i = pl.program_id(0)

    @pl.when(i < nb_ref[0])
    def _compute():
        xb = _load_row_tiles(xs_ref, MOE_BLOCK).astype(BF16)
        hg = jnp.dot(xb, wg_ref[0].astype(BF16), preferred_element_type=F32)
        hu = jnp.dot(xb, wu_ref[0].astype(BF16), preferred_element_type=F32)
        hh = (hg * jax.nn.sigmoid(hg) * hu).astype(BF16)
        _store_row_tiles(y_ref, jnp.dot(hh, wd_ref[0].astype(BF16), preferred_element_type=F32))


def _experts_call(block_e, n_used, xs, w_gate, w_up, w_down):
    d = D_MODEL
    n_blocks = xs.shape[0] // (MOE_BLOCK * ROW_TILE)

    def blk(i, be, nb):
        return (jnp.minimum(i, nb[0] - 1), 0)

    def wsel(i, be, nb):
        return (be[jnp.minimum(i, nb[0] - 1)], 0, 0)
    grid_spec = pltpu.PrefetchScalarGridSpec(
        num_scalar_prefetch=2,
        grid=(n_blocks,),
        in_specs=[
            pl.BlockSpec((MOE_BLOCK * ROW_TILE, LANES), blk),
            pl.BlockSpec((1, d, EXPERT_DIM), wsel),
            pl.BlockSpec((1, d, EXPERT_DIM), wsel),
            pl.BlockSpec((1, EXPERT_DIM, d), wsel),
        ],
        out_specs=pl.BlockSpec((MOE_BLOCK * ROW_TILE, LANES), blk),
    )
    return pl.pallas_call(
        _experts_kernel,
        out_shape=jax.ShapeDtypeStruct(xs.shape, U32),
        grid_spec=grid_spec,
        compiler_params=pltpu.CompilerParams(
            dimension_semantics=("arbitrary",), vmem_limit_bytes=VMEM_LIMIT_BYTES),
        name="experts",
    )(block_e, n_used, xs, w_gate, w_up, w_down)


def _combine_kernel(pos_ref, gate_ref, x1_ref, y_ref, sg_ref, su_ref, sd_ref, g_ref, b_ref, out_ref, buf, sem):
    tt = x1_ref.shape[0]

    def row_copy(t, k):
        src = y_ref.at[pl.ds(pl.multiple_of(pos_ref[k, t], ROW_TILE), ROW_TILE)]
        dst = buf.at[k, pl.ds(t * ROW_TILE, ROW_TILE)]
        return pltpu.make_async_copy(src, dst, sem)

    for t in range(tt):
        for k in range(TOP_K):
            row_copy(t, k).start(priority=k % 2)

    x1 = x1_ref[...]
    xb = x1.astype(BF16)
    hg = jnp.dot(xb, sg_ref[...], preferred_element_type=F32)
    hu = jnp.dot(xb, su_ref[...], preferred_element_type=F32)
    hh = (hg * jax.nn.sigmoid(hg) * hu).astype(BF16)
    acc = DEEPNORM_ALPHA * x1 + jnp.dot(hh, sd_ref[...], preferred_element_type=F32)

    for t in range(tt):
        for k in range(TOP_K):
            row_copy(t, k).wait()

    gates = gate_ref[...]
    for k in range(TOP_K):
        acc = acc + gates[:, k:k + 1] * _load_row_tiles(buf.at[k], tt)
    out_ref[...] = _layer_norm(acc, g_ref[...], b_ref[...])


def _combine_call(pos, gates_t, x1, y, sg, su, sd, ln_g, ln_b):
    n_tok, d = x1.shape
    tt = COMBINE_TILE
    full = lambda a: pl.BlockSpec(a.shape, lambda i: (0,) * a.ndim)
    return pl.pallas_call(
        _combine_kernel,
        out_shape=jax.ShapeDtypeStruct((n_tok, d), F32),
        grid=(n_tok // tt,),
        in_specs=[
            pl.BlockSpec((TOP_K, tt), lambda i: (0, i), memory_space=pltpu.SMEM),
            pl.BlockSpec((tt, TOP_K), lambda i: (i, 0)),
            pl.BlockSpec((tt, d), lambda i: (i, 0)),
            pl.BlockSpec(memory_space=pl.ANY),
            full(sg), full(su), full(sd), full(ln_g), full(ln_b),
        ],
        out_specs=pl.BlockSpec((tt, d), lambda i: (i, 0)),
        scratch_shapes=[pltpu.VMEM((TOP_K, tt * ROW_TILE, LANES), U32), pltpu.SemaphoreType.DMA(())],
        compiler_params=pltpu.CompilerParams(
            dimension_semantics=("arbitrary",), vmem_limit_bytes=VMEM_LIMIT_BYTES),
        name="combine",
    )(pos, gates_t, x1, y, sg, su, sd, ln_g, ln_b)


def _group_weights(w_in):
    blocks = w_in.reshape(D_MODEL, 12, GROUP_WIDTH)
    scale = HEAD_DIM ** -0.5
    groups = []
    for g in range(N_ATTN_GROUPS):
        groups.append(jnp.concatenate(
            [blocks[:, g] * scale, blocks[:, N_ATTN_GROUPS + g], blocks[:, 2 * N_ATTN_GROUPS + g]], axis=1))
    groups.append(jnp.concatenate([blocks[:, 9], blocks[:, 10], blocks[:, 11]], axis=1))
    return jnp.stack(groups).astype(BF16)


def kernel(x, w_in, conv_w, w_o, ln1_g, ln1_b, rel_bias, w_router, router_bias, exp_w_gate, exp_w_up,
           exp_w_down, sh_w_gate, sh_w_up, sh_w_down, ln2_g, ln2_b):
    batch, seq, d = x.shape
    assert d == D_MODEL and seq == DILATED_BRANCHES[-1][0] and w_in.shape[0] == DEPTH
    n_tok = batch * seq
    assert n_tok % POST_TILE == 0 and n_tok % DISPATCH_TILE == 0 and n_tok % COMBINE_TILE == 0

    mix = _mix_call(x, _group_weights(w_in[0]), _bias_table(rel_bias), conv_w[0])

    wr = w_router[0]
    wr_hi = wr.astype(BF16)
    wr_lo = (wr - wr_hi.astype(F32)).astype(BF16)
    x2d = x.reshape(n_tok, d)
    x1, x1p, e_idx, rank, gates, counts = _post_call(
        mix.reshape(n_tok, d), x2d, w_o[0].astype(BF16), ln1_g, ln1_b,
        wr_hi.T, wr_lo.T, router_bias[0].reshape(N_EXPERTS, 1))

    counts = counts[:, 0]
    padded = (counts + MOE_BLOCK - 1) // MOE_BLOCK * MOE_BLOCK
    pends = jnp.cumsum(padded)
    pstarts = (pends - padded).astype(I32)
    n_blocks = n_tok * TOP_K // MOE_BLOCK + N_EXPERTS
    block_start = jnp.arange(n_blocks, dtype=I32) * MOE_BLOCK
    block_e = jnp.sum((pends[None, :] <= block_start[:, None]).astype(I32), axis=1)
    block_e = jnp.minimum(block_e, N_EXPERTS - 1).astype(I32)
    n_used = (pends[-1:] // MOE_BLOCK).astype(I32)
    pos = _pos_call(e_idx, rank, pstarts.reshape(N_EXPERTS, 1))

    xs = _dispatch_call(counts, pstarts, pos, x1p, n_blocks * MOE_BLOCK)
    y = _experts_call(block_e, n_used, xs, exp_w_gate[0], exp_w_up[0], exp_w_down[0])

    out = _combine_call(pos, gates.T, x1, y, sh_w_gate[0].astype(BF16), sh_w_up[0].astype(BF16),
                        sh_w_down[0].astype(BF16), ln2_g, ln2_b)
    return out.reshape(batch, seq, d)
```

```python
import functools
import math

import numpy as np
import jax
import jax.numpy as jnp
from jax import lax
from jax.experimental import pallas as pl
from jax.experimental.pallas import tpu as pltpu

F32 = jnp.float32
BF16 = jnp.bfloat16
I32 = jnp.int32
U32 = jnp.uint32

D_MODEL = 1024
HEAD_DIM = 64
N_ATTN_HEADS = 12
ATTN_WIDTH = N_ATTN_HEADS * HEAD_DIM
CONV_WIDTH = D_MODEL - ATTN_WIDTH
CONV_K = 3
DILATED_BRANCHES = ((128, 1), (512, 4), (2048, 16))
ATTN_BLOCK = 128
N_REL_BUCKETS = 32
REL_MAX_DISTANCE = 2048
N_EXPERTS = 256
TOP_K = 8
N_EXPERT_GROUPS = 8
TOPK_GROUPS = 4
GROUP_SIZE = N_EXPERTS // N_EXPERT_GROUPS
EXPERT_DIM = 256
SHARED_DIM = 256
ROUTED_SCALE = 2.5
DEPTH = 1
DEEPNORM_ALPHA = (2.0 * DEPTH) ** 0.25
LN_EPS = 1e-5

LANES = 128
GROUP_WIDTH = 2 * LANES
HEADS_PER_GROUP = GROUP_WIDTH // HEAD_DIM
N_GROUPS = D_MODEL // GROUP_WIDTH
N_ATTN_GROUPS = ATTN_WIDTH // GROUP_WIDTH
VMEM_LIMIT_BYTES = 56 * 1024 * 1024

MASK_VALUE = -1e30

ROW_CHUNK = 256
ATTN_UNROLL = 4
ROW_TILE = 4
POS_TILE = 2048
POST_TILE = 512
MOE_BLOCK = 256
DISPATCH_TILE = 256
COMBINE_TILE = 256


def _t5_bucket(dist):
    max_exact = N_REL_BUCKETS // 2
    dist = np.asarray(dist)
    log_part = np.log(np.maximum(dist, 1) / max_exact) / math.log(REL_MAX_DISTANCE / max_exact)
    large = max_exact + (log_part * (N_REL_BUCKETS - max_exact)).astype(np.int32)
    large = np.minimum(large, N_REL_BUCKETS - 1)
    return np.where(dist < max_exact, dist, large).astype(np.int32)


def _branch_bucket_and_band(window, dilation):
    w_sub = window // dilation
    qi = np.arange(ATTN_BLOCK)[:, None]
    ki = np.arange(2 * ATTN_BLOCK)[None, :]
    steps = qi + ATTN_BLOCK - ki
    band = (steps >= 0) & (steps <= w_sub)
    bucket = _t5_bucket(np.clip(steps, 0, w_sub) * dilation)
    return bucket, band


def _bias_table(rel_bias):
    tabs = []
    for window, dilation in DILATED_BRANCHES:
        bucket, band = _branch_bucket_and_band(window, dilation)
        onehot = (bucket[:, :, None] == np.arange(N_REL_BUCKETS)).astype(np.float32)
        b = jnp.einsum("qkb,bh->qkh", onehot, rel_bias.astype(F32), precision=lax.Precision.HIGHEST)
        b = jnp.where(band[:, :, None], b, MASK_VALUE)
        tabs.append(b.transpose(2, 0, 1))
    t = jnp.stack(tabs, axis=1)
    t = t.reshape(N_ATTN_GROUPS, HEADS_PER_GROUP, len(DILATED_BRANCHES), ATTN_BLOCK, 2 * ATTN_BLOCK)
    t = t.transpose(0, 2, 1, 3, 4)
    return t.reshape(N_ATTN_GROUPS, len(DILATED_BRANCHES), HEADS_PER_GROUP * ATTN_BLOCK, 2 * ATTN_BLOCK)


def _lane_head():
    return lax.shift_right_logical(lax.broadcasted_iota(I32, (ATTN_BLOCK, GROUP_WIDTH), 1), 6)


def _attn_steps(its, bi, dilation, first, q_s, k_s, v_s, bias_ref, o_s, lse_s):
    logd = int(math.log2(dilation))

    def rows(st):
        return pl.ds(st, ATTN_BLOCK) if dilation == 1 else pl.ds(st, ATTN_BLOCK, stride=dilation)

    def ld(ref, st):
        return jnp.concatenate([ref[0, rows(st), :], ref[1, rows(st), :]], axis=1)

    lane_head = _lane_head()
    bias = bias_ref[0, bi, :, ATTN_BLOCK:] if first else bias_ref[0, bi]
    starts, operands = [], []
    for it in its:
        r = jnp.bitwise_and(it, dilation - 1)
        n = lax.shift_right_logical(it, logd)
        start = r + (dilation * ATTN_BLOCK) * n
        qf = ld(q_s, start)
        if first:
            kk = ld(k_s, start).astype(BF16)
            vv = ld(v_s, start).astype(BF16)
        else:
            prev = start - dilation * ATTN_BLOCK
            kk = jnp.concatenate([ld(k_s, prev), ld(k_s, start)], axis=0).astype(BF16)
            vv = jnp.concatenate([ld(v_s, prev), ld(v_s, start)], axis=0).astype(BF16)
        starts.append(start)
        operands.append((qf, kk, vv))

    results = []
    for qf, kk, vv in operands:
        q4 = jnp.concatenate(
            [jnp.where(lane_head == h, qf, 0.0) for h in range(HEADS_PER_GROUP)], axis=0).astype(BF16)
        s = lax.dot_general(q4, kk, (((1,), (1,)), ((), ())), preferred_element_type=F32) + bias
        m = jnp.max(s, axis=-1, keepdims=True)
        p = jnp.exp(s - m)
        l = jnp.sum(p, axis=-1, keepdims=True)
        pv = jnp.dot(p.astype(BF16), vv, preferred_element_type=F32)
        pvn = pv * (1.0 / l)
        lse = m + jnp.log(l)
        o = jnp.zeros((ATTN_BLOCK, GROUP_WIDTH), F32)
        lb = jnp.zeros((ATTN_BLOCK, GROUP_WIDTH), F32)
        for h in range(HEADS_PER_GROUP):
            sel = lane_head == h
            o = jnp.where(sel, pvn[h * ATTN_BLOCK:(h + 1) * ATTN_BLOCK], o)
            lb = jnp.where(sel, lse[h * ATTN_BLOCK:(h + 1) * ATTN_BLOCK], lb)
        results.append((o, lb))

    for start, (o, lb) in zip(starts, results):
        for sl in range(2):
            o_s[bi, sl, rows(start), :] = o[:, sl * LANES:(sl + 1) * LANES]
            lse_s[bi, sl, rows(start), :] = lb[:, sl * LANES:(sl + 1) * LANES]


def _attn_range(lo, hi, **kw):
    n_groups = (hi - lo) // ATTN_UNROLL

    def body(j, carry):
        base = lo + j * ATTN_UNROLL
        _attn_steps([base + u for u in range(ATTN_UNROLL)], **kw)
        return carry
    if n_groups > 0:
        lax.fori_loop(0, n_groups, body, 0)
    tail = list(range(lo + n_groups * ATTN_UNROLL, hi))
    if tail:
        _attn_steps([jnp.int32(t) for t in tail], **kw)


def _mix_kernel(x_ref, w_ref, bias_ref, cw_ref, out_ref, xb_s, q_s, k_s, v_s, o_s, lse_s, u_s):
    g = pl.program_id(1)
    seq = x_ref.shape[1]
    n_chunks = seq // ROW_CHUNK

    @pl.when(g == 0)
    def _cast_x():
        def body(c, carry):
            rows = pl.ds(pl.multiple_of(c * ROW_CHUNK, ROW_CHUNK), ROW_CHUNK)
            xb_s[rows, :] = x_ref[0, rows, :].astype(BF16)
            return carry
        lax.fori_loop(0, n_chunks, body, 0)

    def proj(c, carry):
        rows = pl.ds(pl.multiple_of(c * ROW_CHUNK, ROW_CHUNK), ROW_CHUNK)
        res = jnp.dot(xb_s[rows, :], w_ref[0], preferred_element_type=F32)
        for j, dst in enumerate((q_s, k_s, v_s)):
            for sl in range(2):
                lo = j * GROUP_WIDTH + sl * LANES
                dst[sl, rows, :] = res[:, lo:lo + LANES]
        return carry
    lax.fori_loop(0, n_chunks, proj, 0)

    @pl.when(g < N_ATTN_GROUPS)
    def _attention():
        for bi, (window, dilation) in enumerate(DILATED_BRANCHES):
            n_blocks = seq // dilation // ATTN_BLOCK
            n_steps = n_blocks * dilation
            kw = dict(bi=bi, dilation=dilation, q_s=q_s, k_s=k_s, v_s=v_s,
                      bias_ref=bias_ref, o_s=o_s, lse_s=lse_s)
            _attn_range(0, dilation, first=True, **kw)
            _attn_range(dilation, n_steps, first=False, **kw)

        def combine(c, carry):
            rows = pl.ds(pl.multiple_of(c * ROW_CHUNK, ROW_CHUNK), ROW_CHUNK)
            for sl in range(2):
                ls = [lse_s[bi, sl, rows, :] for bi in range(len(DILATED_BRANCHES))]
                mx = jnp.maximum(jnp.maximum(ls[0], ls[1]), ls[2])
                ws = [jnp.exp(v - mx) for v in ls]
                den = ws[0] + ws[1] + ws[2]
                num = ws[0] * o_s[0, sl, rows, :] + ws[1] * o_s[1, sl, rows, :] + ws[2] * o_s[2, sl, rows, :]
                out_ref[0, rows, sl * LANES:(sl + 1) * LANES] = (num / den).astype(BF16)
            return carry
        lax.fori_loop(0, n_chunks, combine, 0)

    @pl.when(g == N_ATTN_GROUPS)
    def _short_conv():
        pad = 8
        for sl in range(2):
            u_s[sl, 0:pad, :] = jnp.zeros((pad, LANES), F32)
            for c in range(n_chunks):
                lo = c * ROW_CHUNK
                u_s[sl, pad + lo:pad + lo + ROW_CHUNK, :] = v_s[sl, lo:lo + ROW_CHUNK, :] * q_s[sl, lo:lo + ROW_CHUNK, :]
            w = [cw_ref[kk:kk + 1, sl * LANES:(sl + 1) * LANES] for kk in range(CONV_K)]
            for c in range(n_chunks):
                lo = c * ROW_CHUNK
                y = w[2] * u_s[sl, pad + lo:pad + lo + ROW_CHUNK, :]
                y = y + w[1] * u_s[sl, pad + lo - 1:pad + lo - 1 + ROW_CHUNK, :]
                y = y + w[0] * u_s[sl, pad + lo - 2:pad + lo - 2 + ROW_CHUNK, :]
                out_ref[0, lo:lo + ROW_CHUNK, sl * LANES:(sl + 1) * LANES] = (
                    k_s[sl, lo:lo + ROW_CHUNK, :] * y).astype(BF16)


def _mix_call(x, w_groups, bias_tbl, conv_w):
    batch, seq, d = x.shape
    return pl.pallas_call(
        _mix_kernel,
        out_shape=jax.ShapeDtypeStruct((batch, seq, d), BF16),
        grid=(batch, N_GROUPS),
        in_specs=[
            pl.BlockSpec((1, seq, d), lambda b, g: (b, 0, 0)),
            pl.BlockSpec((1, d, 3 * GROUP_WIDTH), lambda b, g: (g, 0, 0)),
            pl.BlockSpec((1,) + bias_tbl.shape[1:], lambda b, g: (jnp.minimum(g, N_ATTN_GROUPS - 1), 0, 0, 0)),
            pl.BlockSpec(conv_w.shape, lambda b, g: (0, 0)),
        ],
        out_specs=pl.BlockSpec((1, seq, GROUP_WIDTH), lambda b, g: (b, 0, g)),
        scratch_shapes=[
            pltpu.VMEM((seq, d), BF16),
            pltpu.VMEM((2, seq, LANES), F32),
            pltpu.VMEM((2, seq, LANES), F32),
            pltpu.VMEM((2, seq, LANES), F32),
            pltpu.VMEM((len(DILATED_BRANCHES), 2, seq, LANES), F32),
            pltpu.VMEM((len(DILATED_BRANCHES), 2, seq, LANES), F32),
            pltpu.VMEM((2, seq + 8, LANES), F32),
        ],
        compiler_params=pltpu.CompilerParams(
            dimension_semantics=("arbitrary", "arbitrary"), vmem_limit_bytes=VMEM_LIMIT_BYTES),
        name="mix",
    )(x, w_groups, bias_tbl, conv_w)


def _layer_norm(h, g, b):
    mu = jnp.mean(h, axis=-1, keepdims=True)
    c = h - mu
    var = jnp.mean(c * c, axis=-1, keepdims=True)
    return c * lax.rsqrt(var + LN_EPS) * g + b


def _bf16_bits(v):
    return lax.bitcast_convert_type(v.astype(BF16).astype(F32), U32)


def _store_row_tiles(ref, val):
    n = val.shape[0]
    for j in range(ROW_TILE):
        lo = _bf16_bits(val[:, (2 * j) * LANES:(2 * j + 1) * LANES])
        hi = _bf16_bits(val[:, (2 * j + 1) * LANES:(2 * j + 2) * LANES])
        ref[pl.ds(j, n, stride=ROW_TILE), :] = jnp.bitwise_or(lax.shift_right_logical(lo, jnp.uint32(16)), hi)


def _load_row_tiles(ref, n):
    pieces = []
    for j in range(ROW_TILE):
        w = ref[pl.ds(j, n, stride=ROW_TILE), :]
        pieces.append(lax.bitcast_convert_type(lax.shift_left(w, jnp.uint32(16)), F32))
        pieces.append(lax.bitcast_convert_type(jnp.bitwise_and(w, jnp.uint32(0xFFFF0000)), F32))
    return jnp.concatenate(pieces, axis=1)


def _first_argmax_rows(v, row_ids, n_rows):
    m = jnp.max(v, axis=0, keepdims=True)
    idx = jnp.min(jnp.where(v == m, row_ids, n_rows), axis=0, keepdims=True)
    return m, idx


def _post_kernel(mix_ref, x_ref, wo_ref, g_ref, b_ref, wrh_ref, wrl_ref, rb_ref,
                 x1_ref, x1p_ref, e_ref, rank_ref, gate_ref, cnt_ref, carry_s):
    i = pl.program_id(0)
    tm = x_ref.shape[0]

    @pl.when(i == 0)
    def _init():
        carry_s[...] = jnp.zeros_like(carry_s)

    h = DEEPNORM_ALPHA * x_ref[...] + jnp.dot(mix_ref[...], wo_ref[...], preferred_element_type=F32)
    x1 = _layer_norm(h, g_ref[...], b_ref[...])
    x1_ref[...] = x1
    _store_row_tiles(x1p_ref, x1)

    x_hi = x1.astype(BF16)
    x_lo = (x1 - x_hi.astype(F32)).astype(BF16)
    dn = (((1,), (1,)), ((), ()))
    logits = lax.dot_general(wrh_ref[...], x_hi, dn, preferred_element_type=F32)
    logits = logits + lax.dot_general(wrh_ref[...], x_lo, dn, preferred_element_type=F32)
    logits = logits + lax.dot_general(wrl_ref[...], x_hi, dn, preferred_element_type=F32)
    scores = jax.nn.sigmoid(logits)
    biased = scores + rb_ref[...]

    neg_inf = -jnp.inf
    sub_ids = lax.broadcasted_iota(I32, (GROUP_SIZE, tm), 0)
    gs_rows = []
    for gi in range(N_EXPERT_GROUPS):
        bg = biased[gi * GROUP_SIZE:(gi + 1) * GROUP_SIZE]
        m1, i1 = _first_argmax_rows(bg, sub_ids, GROUP_SIZE)
        m2 = jnp.max(jnp.where(sub_ids == i1, neg_inf, bg), axis=0, keepdims=True)
        gs_rows.append(m1 + m2)
    gscore = jnp.concatenate(gs_rows, axis=0)
    grp_ids = lax.broadcasted_iota(I32, (N_EXPERT_GROUPS, tm), 0)
    keep_g = jnp.zeros((N_EXPERT_GROUPS, tm), jnp.bool_)
    for _ in range(TOPK_GROUPS):
        _, gi1 = _first_argmax_rows(gscore, grp_ids, N_EXPERT_GROUPS)
        hit = grp_ids == gi1
        keep_g = jnp.logical_or(keep_g, hit)
        gscore = jnp.where(hit, neg_inf, gscore)
    keep_f = keep_g.astype(F32)
    keep_e = jnp.concatenate(
        [jnp.broadcast_to(keep_f[gi:gi + 1], (GROUP_SIZE, tm)) for gi in range(N_EXPERT_GROUPS)], axis=0)
    masked = jnp.where(keep_e > 0.5, biased, neg_inf)

    exp_ids = lax.broadcasted_iota(I32, (N_EXPERTS, tm), 0)
    sel = jnp.zeros((N_EXPERTS, tm), F32)
    e_rows, g_rows = [], []
    for _ in range(TOP_K):
        _, e1 = _first_argmax_rows(masked, exp_ids, N_EXPERTS)
        hit = exp_ids == e1
        sel = jnp.where(hit, 1.0, sel)
        g_rows.append(jnp.sum(jnp.where(hit, scores, 0.0), axis=0, keepdims=True))
        e_rows.append(e1)
        masked = jnp.where(hit, neg_inf, masked)
    gates = jnp.concatenate(g_rows, axis=0)
    gates = gates / jnp.sum(gates, axis=0, keepdims=True) * ROUTED_SCALE
    gate_ref[...] = gates
    e_ref[...] = jnp.concatenate(e_rows, axis=0)

    t_row = lax.broadcasted_iota(I32, (tm, tm), 0)
    t_col = lax.broadcasted_iota(I32, (tm, tm), 1)
    upper = (t_row < t_col).astype(BF16)
    excl = jnp.dot(sel.astype(BF16), upper, preferred_element_type=F32)
    base = carry_s[...] + excl
    r_rows = [jnp.sum(jnp.where(exp_ids == e1, base, 0.0), axis=0, keepdims=True) for e1 in e_rows]
    rank_ref[...] = jnp.concatenate(r_rows, axis=0).astype(I32)
    carry = carry_s[...] + jnp.sum(sel, axis=1, keepdims=True)
    carry_s[...] = carry
    cnt_ref[...] = carry.astype(I32)


def _post_call(mix2d, x2d, wo, ln_g, ln_b, wr_hi, wr_lo, rb):
    n_tok, d = x2d.shape
    tm = POST_TILE
    tok_spec = pl.BlockSpec((tm, d), lambda i: (i, 0))
    k_spec = pl.BlockSpec((TOP_K, tm), lambda i: (0, i))
    full = lambda a: pl.BlockSpec(a.shape, lambda i: (0,) * a.ndim)
    return pl.pallas_call(
        _post_kernel,
        out_shape=(
            jax.ShapeDtypeStruct((n_tok, d), F32),
            jax.ShapeDtypeStruct((n_tok * ROW_TILE, LANES), U32),
            jax.ShapeDtypeStruct((TOP_K, n_tok), I32),
            jax.ShapeDtypeStruct((TOP_K, n_tok), I32),
            jax.ShapeDtypeStruct((TOP_K, n_tok), F32),
            jax.ShapeDtypeStruct((N_EXPERTS, 1), I32),
        ),
        grid=(n_tok // tm,),
        in_specs=[tok_spec, tok_spec, full(wo), full(ln_g), full(ln_b), full(wr_hi), full(wr_lo), full(rb)],
        out_specs=(tok_spec, pl.BlockSpec((tm * ROW_TILE, LANES), lambda i: (i, 0)), k_spec, k_spec, k_spec,
                   pl.BlockSpec((N_EXPERTS, 1), lambda i: (0, 0))),
        scratch_shapes=[pltpu.VMEM((N_EXPERTS, 1), F32)],
        compiler_params=pltpu.CompilerParams(
            dimension_semantics=("arbitrary",), vmem_limit_bytes=VMEM_LIMIT_BYTES),
        name="post",
    )(mix2d, x2d, wo, ln_g, ln_b, wr_hi, wr_lo, rb)


def _pos_kernel(e_ref, rank_ref, ps_ref, pos_ref):
    tp = e_ref.shape[1]
    exp_ids = lax.broadcasted_iota(I32, (N_EXPERTS, tp), 0)
    ps = ps_ref[...]
    rows = []
    for k in range(TOP_K):
        start = jnp.sum(jnp.where(exp_ids == e_ref[k:k + 1, :], ps, 0), axis=0, keepdims=True)
        rows.append((start + rank_ref[k:k + 1, :]) * ROW_TILE)
    pos_ref[...] = jnp.concatenate(rows, axis=0)


def _pos_call(e_idx, rank, pstarts):
    n_tok = e_idx.shape[1]
    tp = POS_TILE
    k_spec = pl.BlockSpec((TOP_K, tp), lambda i: (0, i))
    return pl.pallas_call(
        _pos_kernel,
        out_shape=jax.ShapeDtypeStruct((TOP_K, n_tok), I32),
        grid=(n_tok // tp,),
        in_specs=[k_spec, k_spec, pl.BlockSpec((N_EXPERTS, 1), lambda i: (0, 0))],
        out_specs=k_spec,
        compiler_params=pltpu.CompilerParams(dimension_semantics=("arbitrary",)),
        name="positions",
    )(e_idx, rank, pstarts)


def _dispatch_kernel(cnt_ref, ps_ref, pos_ref, x_ref, xs_ref, zbuf, sem, zsem):
    i = pl.program_id(0)
    tt = x_ref.shape[0] // ROW_TILE

    def pad_copies(e, fn):
        cnt = cnt_ref[e]
        pad = jnp.bitwise_and(-cnt, MOE_BLOCK - 1)
        off = ps_ref[e] + cnt
        for b in range(int(math.log2(MOE_BLOCK))):
            size = 1 << b
            hit = jnp.bitwise_and(pad, size)

            @pl.when(hit != 0)
            def _(off=off, size=size):
                fn(pltpu.make_async_copy(
                    zbuf.at[pl.ds(0, size * ROW_TILE)],
                    xs_ref.at[pl.ds(pl.multiple_of(off * ROW_TILE, ROW_TILE), size * ROW_TILE)], zsem))
            off = off + hit

    @pl.when(i == 0)
    def _zero_padding_rows():
        zbuf[...] = jnp.zeros_like(zbuf)

        def start(e, carry):
            pad_copies(e, lambda cp: cp.start())
            return carry

        def wait(e, carry):
            pad_copies(e, lambda cp: cp.wait())
            return carry
        lax.fori_loop(0, N_EXPERTS, start, 0)
        lax.fori_loop(0, N_EXPERTS, wait, 0)

    def row_copy(t, k):
        src = x_ref.at[pl.ds(t * ROW_TILE, ROW_TILE)]
        dst = xs_ref.at[pl.ds(pl.multiple_of(pos_ref[k, t], ROW_TILE), ROW_TILE)]
        return pltpu.make_async_copy(src, dst, sem)

    for t in range(tt):
        for k in range(TOP_K):
            row_copy(t, k).start(priority=k % 2)
    for t in range(tt):
        for k in range(TOP_K):
            row_copy(t, k).wait()


def _dispatch_call(counts, pstarts, pos, x1r, n_rows):
    n_tok = pos.shape[1]
    tt = DISPATCH_TILE
    grid_spec = pltpu.PrefetchScalarGridSpec(
        num_scalar_prefetch=2,
        grid=(n_tok // tt,),
        in_specs=[
            pl.BlockSpec((TOP_K, tt), lambda i, c, p: (0, i), memory_space=pltpu.SMEM),
            pl.BlockSpec((tt * ROW_TILE, LANES), lambda i, c, p: (i, 0)),
        ],
        out_specs=pl.BlockSpec(memory_space=pl.ANY),
        scratch_shapes=[
            pltpu.VMEM((MOE_BLOCK // 2 * ROW_TILE, LANES), U32),
            pltpu.SemaphoreType.DMA(()),
            pltpu.SemaphoreType.DMA(()),
        ],
    )
    return pl.pallas_call(
        _dispatch_kernel,
        out_shape=jax.ShapeDtypeStruct((n_rows * ROW_TILE, LANES), U32),
        grid_spec=grid_spec,
        compiler_params=pltpu.CompilerParams(
            dimension_semantics=("arbitrary",), vmem_limit_bytes=VMEM_LIMIT_BYTES),
        name="dispatch",
    )(counts, pstarts, pos, x1r)


def _experts_kernel(ps_ref, nb_ref, xs_ref, wg_ref, wu_ref, wd_ref, y_ref,
                    xbuf, ybuf, wg_b, wu_b, wd_b, sem_in, sem_out):
    e = pl.program_id(0)
    nb = nb_ref[e]
    blk_rows = MOE_BLOCK * ROW_TILE
    base = ps_ref[e] * ROW_TILE

    def rows_of(j):
        return pl.ds(pl.multiple_of(base + j * blk_rows, blk_rows), blk_rows)

    def in_copy(j, slot):
        return pltpu.make_async_copy(xs_ref.at[rows_of(j)], xbuf.at[slot], sem_in.at[slot])

    def out_copy(j, slot):
        return pltpu.make_async_copy(ybuf.at[slot], y_ref.at[rows_of(j)], sem_out.at[slot])

    def do_block(j, slot):
        in_copy(j, slot).wait()

        @pl.when(j + 1 < nb)
        def _prefetch():
            in_copy(j + 1, 1 - slot).start()

        @pl.when(j >= 2)
        def _free_slot():
            out_copy(j - 2, slot).wait()

        xb = _load_row_tiles(xbuf.at[slot], MOE_BLOCK).astype(BF16)
        hg = jnp.dot(xb, wg_b[...], preferred_element_type=F32)
        hu = jnp.dot(xb, wu_b[...], preferred_element_type=F32)
        hh = (hg * jax.nn.sigmoid(hg) * hu).astype(BF16)
        _store_row_tiles(ybuf.at[slot], jnp.dot(hh, wd_b[...], preferred_element_type=F32))
        out_copy(j, slot).start()

    @pl.when(nb > 0)
    def _run():
        in_copy(0, 0).start()
        wg_b[...] = wg_ref[0].astype(BF16)
        wu_b[...] = wu_ref[0].astype(BF16)
        wd_b[...] = wd_ref[0].astype(BF16)

        def pair(jj, carry):
            j0 = 2 * jj
            do_block(j0, 0)

            @pl.when(j0 + 1 < nb)
            def _odd():
                do_block(j0 + 1, 1)
            return carry
        lax.fori_loop(0, lax.shift_right_logical(nb + 1, 1), pair, 0)

        last = nb - 1
        out_copy(last, 0).wait()

        @pl.when(nb >= 2)
        def _second():
            out_copy(last, 1).wait()


def _experts_call(pstarts, n_blk, xs, w_gate, w_up, w_down):
    d = D_MODEL

    def wsel(e, ps, nb):
        return (e, 0, 0)
    grid_spec = pltpu.PrefetchScalarGridSpec(
        num_scalar_prefetch=2,
        grid=(N_EXPERTS,),
        in_specs=[
            pl.BlockSpec(memory_space=pl.ANY),
            pl.BlockSpec((1, d, EXPERT_DIM), wsel),
            pl.BlockSpec((1, d, EXPERT_DIM), wsel),
            pl.BlockSpec((1, EXPERT_DIM, d), wsel),
        ],
        out_specs=pl.BlockSpec(memory_space=pl.ANY),
        scratch_shapes=[
            pltpu.VMEM((2, MOE_BLOCK * ROW_TILE, LANES), U32),
            pltpu.VMEM((2, MOE_BLOCK * ROW_TILE, LANES), U32),
            pltpu.VMEM((d, EXPERT_DIM), BF16),
            pltpu.VMEM((d, EXPERT_DIM), BF16),
            pltpu.VMEM((EXPERT_DIM, d), BF16),
            pltpu.SemaphoreType.DMA((2,)),
            pltpu.SemaphoreType.DMA((2,)),
        ],
    )
    return pl.pallas_call(
        _experts_kernel,
        out_shape=jax.ShapeDtypeStruct(xs.shape, U32),
        grid_spec=grid_spec,
        compiler_params=pltpu.CompilerParams(
            dimension_semantics=("arbitrary",), vmem_limit_bytes=VMEM_LIMIT_BYTES),
        name="experts",
    )(pstarts, n_blk, xs, w_gate, w_up, w_down)


def _combine_kernel(pos_ref, gate_ref, x1_ref, y_ref, sg_ref, su_ref, sd_ref, g_ref, b_ref, out_ref, buf, sem):
    tt = x1_ref.shape[0]

    def row_copy(t, k):
        src = y_ref.at[pl.ds(pl.multiple_of(pos_ref[k, t], ROW_TILE), ROW_TILE)]
        dst = buf.at[k, pl.ds(t * ROW_TILE, ROW_TILE)]
        return pltpu.make_async_copy(src, dst, sem)

    for t in range(tt):
        for k in range(TOP_K):
            row_copy(t, k).start(priority=k % 2)

    x1 = x1_ref[...]
    xb = x1.astype(BF16)
    hg = jnp.dot(xb, sg_ref[...], preferred_element_type=F32)
    hu = jnp.dot(xb, su_ref[...], preferred_element_type=F32)
    hh = (hg * jax.nn.sigmoid(hg) * hu).astype(BF16)
    acc = DEEPNORM_ALPHA * x1 + jnp.dot(hh, sd_ref[...], preferred_element_type=F32)

    for t in range(tt):
        for k in range(TOP_K):
            row_copy(t, k).wait()

    gates = gate_ref[...]
    for k in range(TOP_K):
        acc = acc + gates[:, k:k + 1] * _load_row_tiles(buf.at[k], tt)
    out_ref[...] = _layer_norm(acc, g_ref[...], b_ref[...])


def _combine_call(pos, gates_t, x1, y, sg, su, sd, ln_g, ln_b):
    n_tok, d = x1.shape
    tt = COMBINE_TILE
    full = lambda a: pl.BlockSpec(a.shape, lambda i: (0,) * a.ndim)
    return pl.pallas_call(
        _combine_kernel,
        out_shape=jax.ShapeDtypeStruct((n_tok, d), F32),
        grid=(n_tok // tt,),
        in_specs=[
            pl.BlockSpec((TOP_K, tt), lambda i: (0, i), memory_space=pltpu.SMEM),
            pl.BlockSpec((tt, TOP_K), lambda i: (i, 0)),
            pl.BlockSpec((tt, d), lambda i: (i, 0)),
            pl.BlockSpec(memory_space=pl.ANY),
            full(sg), full(su), full(sd), full(ln_g), full(ln_b),
        ],
        out_specs=pl.BlockSpec((tt, d), lambda i: (i, 0)),
        scratch_shapes=[pltpu.VMEM((TOP_K, tt * ROW_TILE, LANES), U32), pltpu.SemaphoreType.DMA(())],
        compiler_params=pltpu.CompilerParams(
            dimension_semantics=("arbitrary",), vmem_limit_bytes=VMEM_LIMIT_BYTES),
        name="combine",
    )(pos, gates_t, x1, y, sg, su, sd, ln_g, ln_b)


def _group_weights(w_in):
    blocks = w_in.reshape(D_MODEL, 12, GROUP_WIDTH)
    scale = HEAD_DIM ** -0.5
    groups = []
    for g in range(N_ATTN_GROUPS):
        groups.append(jnp.concatenate(
            [blocks[:, g] * scale, blocks[:, N_ATTN_GROUPS + g], blocks[:, 2 * N_ATTN_GROUPS + g]], axis=1))
    groups.append(jnp.concatenate([blocks[:, 9], blocks[:, 10], blocks[:, 11]], axis=1))
    return jnp.stack(groups).astype(BF16)


def kernel(x, w_in, conv_w, w_o, ln1_g, ln1_b, rel_bias, w_router, router_bias, exp_w_gate, exp_w_up,
           exp_w_down, sh_w_gate, sh_w_up, sh_w_down, ln2_g, ln2_b):
    batch, seq, d = x.shape
    assert d == D_MODEL and seq == DILATED_BRANCHES[-1][0] and w_in.shape[0] == DEPTH
    n_tok = batch * seq
    assert n_tok % POST_TILE == 0 and n_tok % DISPATCH_TILE == 0 and n_tok % COMBINE_TILE == 0

    mix = _mix_call(x, _group_weights(w_in[0]), _bias_table(rel_bias), conv_w[0])

    wr = w_router[0]
    wr_hi = wr.astype(BF16)
    wr_lo = (wr - wr_hi.astype(F32)).astype(BF16)
    x2d = x.reshape(n_tok, d)
    x1, x1p, e_idx, rank, gates, counts = _post_call(
        mix.reshape(n_tok, d), x2d, w_o[0].astype(BF16), ln1_g, ln1_b,
        wr_hi.T, wr_lo.T, router_bias[0].reshape(N_EXPERTS, 1))

    counts = counts[:, 0]
    padded = (counts + MOE_BLOCK - 1) // MOE_BLOCK * MOE_BLOCK
    pends = jnp.cumsum(padded)
    pstarts = (pends - padded).astype(I32)
    n_blocks = n_tok * TOP_K // MOE_BLOCK + N_EXPERTS
    n_blk = (padded // MOE_BLOCK).astype(I32)
    pos = _pos_call(e_idx, rank, pstarts.reshape(N_EXPERTS, 1))

    xs = _dispatch_call(counts, pstarts, pos, x1p, n_blocks * MOE_BLOCK)
    y = _experts_call(pstarts, n_blk, xs, exp_w_gate[0], exp_w_up[0], exp_w_down[0])

    out = _combine_call(pos, gates.T, x1, y, sh_w_gate[0].astype(BF16), sh_w_up[0].astype(BF16),
                        sh_w_down[0].astype(BF16), ln2_g, ln2_b)
    return out.reshape(batch, seq, d)
```

```python
import functools
import math

import numpy as np
import jax
import jax.numpy as jnp
from jax import lax
from jax.experimental import pallas as pl
from jax.experimental.pallas import tpu as pltpu

F32 = jnp.float32
BF16 = jnp.bfloat16
I32 = jnp.int32
U32 = jnp.uint32

D_MODEL = 1024
HEAD_DIM = 64
N_ATTN_HEADS = 12
ATTN_WIDTH = N_ATTN_HEADS * HEAD_DIM
CONV_WIDTH = D_MODEL - ATTN_WIDTH
CONV_K = 3
DILATED_BRANCHES = ((128, 1), (512, 4), (2048, 16))
ATTN_BLOCK = 128
N_REL_BUCKETS = 32
REL_MAX_DISTANCE = 2048
N_EXPERTS = 256
TOP_K = 8
N_EXPERT_GROUPS = 8
TOPK_GROUPS = 4
GROUP_SIZE = N_EXPERTS // N_EXPERT_GROUPS
EXPERT_DIM = 256
SHARED_DIM = 256
ROUTED_SCALE = 2.5
DEPTH = 1
DEEPNORM_ALPHA = (2.0 * DEPTH) ** 0.25
LN_EPS = 1e-5

LANES = 128
GROUP_WIDTH = 2 * LANES
HEADS_PER_GROUP = GROUP_WIDTH // HEAD_DIM
N_GROUPS = D_MODEL // GROUP_WIDTH
N_ATTN_GROUPS = ATTN_WIDTH // GROUP_WIDTH
VMEM_LIMIT_BYTES = 56 * 1024 * 1024

MASK_VALUE = -1e30

ROW_CHUNK = 256
ATTN_UNROLL = 4
ROW_TILE = 4
POS_TILE = 2048
POST_TILE = 512
MOE_BLOCK = 256
EXPERT_SLOTS = 4
DISPATCH_TILE = 256
COMBINE_TILE = 256
COMBINE_PARTS = 4


def _t5_bucket(dist):
    max_exact = N_REL_BUCKETS // 2
    dist = np.asarray(dist)
    log_part = np.log(np.maximum(dist, 1) / max_exact) / math.log(REL_MAX_DISTANCE / max_exact)
    large = max_exact + (log_part * (N_REL_BUCKETS - max_exact)).astype(np.int32)
    large = np.minimum(large, N_REL_BUCKETS - 1)
    return np.where(dist < max_exact, dist, large).astype(np.int32)


def _branch_bucket_and_band(window, dilation):
    w_sub = window // dilation
    qi = np.arange(ATTN_BLOCK)[:, None]
    ki = np.arange(2 * ATTN_BLOCK)[None, :]
    steps = qi + ATTN_BLOCK - ki
    band = (steps >= 0) & (steps <= w_sub)
    bucket = _t5_bucket(np.clip(steps, 0, w_sub) * dilation)
    return bucket, band


def _bias_table(rel_bias):
    tabs = []
    for window, dilation in DILATED_BRANCHES:
        bucket, band = _branch_bucket_and_band(window, dilation)
        onehot = (bucket[:, :, None] == np.arange(N_REL_BUCKETS)).astype(np.float32)
        b = jnp.einsum("qkb,bh->qkh", onehot, rel_bias.astype(F32), precision=lax.Precision.HIGHEST)
        b = jnp.where(band[:, :, None], b, MASK_VALUE)
        tabs.append(b.transpose(2, 0, 1))
    t = jnp.stack(tabs, axis=1)
    t = t.reshape(N_ATTN_GROUPS, HEADS_PER_GROUP, len(DILATED_BRANCHES), ATTN_BLOCK, 2 * ATTN_BLOCK)
    t = t.transpose(0, 2, 1, 3, 4)
    return t.reshape(N_ATTN_GROUPS, len(DILATED_BRANCHES), HEADS_PER_GROUP * ATTN_BLOCK, 2 * ATTN_BLOCK)


def _lane_head():
    return lax.shift_right_logical(lax.broadcasted_iota(I32, (ATTN_BLOCK, GROUP_WIDTH), 1), 6)


def _attn_steps(its, bi, dilation, first, q_s, k_s, v_s, bias_ref, o_s, lse_s):
    logd = int(math.log2(dilation))

    def rows(st):
        return pl.ds(st, ATTN_BLOCK) if dilation == 1 else pl.ds(st, ATTN_BLOCK, stride=dilation)

    def ld(ref, st):
        return jnp.concatenate([ref[0, rows(st), :], ref[1, rows(st), :]], axis=1)

    lane_head = _lane_head()
    bias = bias_ref[0, bi, :, ATTN_BLOCK:] if first else bias_ref[0, bi]
    starts, operands = [], []
    for it in its:
        r = jnp.bitwise_and(it, dilation - 1)
        n = lax.shift_right_logical(it, logd)
        start = r + (dilation * ATTN_BLOCK) * n
        qf = ld(q_s, start)
        if first:
            kk = ld(k_s, start).astype(BF16)
            vv = ld(v_s, start).astype(BF16)
        else:
            prev = start - dilation * ATTN_BLOCK
            kk = jnp.concatenate([ld(k_s, prev), ld(k_s, start)], axis=0).astype(BF16)
            vv = jnp.concatenate([ld(v_s, prev), ld(v_s, start)], axis=0).astype(BF16)
        starts.append(start)
        operands.append((qf, kk, vv))

    results = []
    for qf, kk, vv in operands:
        q4 = jnp.concatenate(
            [jnp.where(lane_head == h, qf, 0.0) for h in range(HEADS_PER_GROUP)], axis=0).astype(BF16)
        s = lax.dot_general(q4, kk, (((1,), (1,)), ((), ())), preferred_element_type=F32) + bias
        m = jnp.max(s, axis=-1, keepdims=True)
        p = jnp.exp(s - m)
        l = jnp.sum(p, axis=-1, keepdims=True)
        pv = jnp.dot(p.astype(BF16), vv, preferred_element_type=F32)
        pvn = pv * (1.0 / l)
        lse = m + jnp.log(l)
        o = jnp.zeros((ATTN_BLOCK, GROUP_WIDTH), F32)
        lb = jnp.zeros((ATTN_BLOCK, GROUP_WIDTH), F32)
        for h in range(HEADS_PER_GROUP):
            sel = lane_head == h
            o = jnp.where(sel, pvn[h * ATTN_BLOCK:(h + 1) * ATTN_BLOCK], o)
            lb = jnp.where(sel, lse[h * ATTN_BLOCK:(h + 1) * ATTN_BLOCK], lb)
        results.append((o, lb))

    for start, (o, lb) in zip(starts, results):
        for sl in range(2):
            o_s[bi, sl, rows(start), :] = o[:, sl * LANES:(sl + 1) * LANES]
            lse_s[bi, sl, rows(start), :] = lb[:, sl * LANES:(sl + 1) * LANES]


def _attn_range(lo, hi, **kw):
    n_groups = (hi - lo) // ATTN_UNROLL

    def body(j, carry):
        base = lo + j * ATTN_UNROLL
        _attn_steps([base + u for u in range(ATTN_UNROLL)], **kw)
        return carry
    if n_groups > 0:
        lax.fori_loop(0, n_groups, body, 0)
    tail = list(range(lo + n_groups * ATTN_UNROLL, hi))
    if tail:
        _attn_steps([jnp.int32(t) for t in tail], **kw)


def _mix_kernel(x_ref, w_ref, bias_ref, cw_ref, out_ref, xb_s, q_s, k_s, v_s, o_s, lse_s, u_s):
    g = pl.program_id(1)
    seq = x_ref.shape[1]
    n_chunks = seq // ROW_CHUNK

    @pl.when(g == 0)
    def _cast_x():
        def body(c, carry):
            rows = pl.ds(pl.multiple_of(c * ROW_CHUNK, ROW_CHUNK), ROW_CHUNK)
            xb_s[rows, :] = x_ref[0, rows, :].astype(BF16)
            return carry
        lax.fori_loop(0, n_chunks, body, 0)

    def proj(c, carry):
        rows = pl.ds(pl.multiple_of(c * ROW_CHUNK, ROW_CHUNK), ROW_CHUNK)
        res = jnp.dot(xb_s[rows, :], w_ref[0], preferred_element_type=F32)
        for j, dst in enumerate((q_s, k_s, v_s)):
            for sl in range(2):
                lo = j * GROUP_WIDTH + sl * LANES
                dst[sl, rows, :] = res[:, lo:lo + LANES]
        return carry
    lax.fori_loop(0, n_chunks, proj, 0)

    @pl.when(g < N_ATTN_GROUPS)
    def _attention():
        for bi, (window, dilation) in enumerate(DILATED_BRANCHES):
            n_blocks = seq // dilation // ATTN_BLOCK
            n_steps = n_blocks * dilation
            kw = dict(bi=bi, dilation=dilation, q_s=q_s, k_s=k_s, v_s=v_s,
                      bias_ref=bias_ref, o_s=o_s, lse_s=lse_s)
            _attn_range(0, dilation, first=True, **kw)
            _attn_range(dilation, n_steps, first=False, **kw)

        def combine(c, carry):
            rows = pl.ds(pl.multiple_of(c * ROW_CHUNK, ROW_CHUNK), ROW_CHUNK)
            for sl in range(2):
                ls = [lse_s[bi, sl, rows, :] for bi in range(len(DILATED_BRANCHES))]
                mx = jnp.maximum(jnp.maximum(ls[0], ls[1]), ls[2])
                ws = [jnp.exp(v - mx) for v in ls]
                den = ws[0] + ws[1] + ws[2]
                num = ws[0] * o_s[0, sl, rows, :] + ws[1] * o_s[1, sl, rows, :] + ws[2] * o_s[2, sl, rows, :]
                out_ref[0, rows, sl * LANES:(sl + 1) * LANES] = (num / den).astype(BF16)
            return carry
        lax.fori_loop(0, n_chunks, combine, 0)

    @pl.when(g == N_ATTN_GROUPS)
    def _short_conv():
        pad = 8
        for sl in range(2):
            u_s[sl, 0:pad, :] = jnp.zeros((pad, LANES), F32)
            for c in range(n_chunks):
                lo = c * ROW_CHUNK
                u_s[sl, pad + lo:pad + lo + ROW_CHUNK, :] = v_s[sl, lo:lo + ROW_CHUNK, :] * q_s[sl, lo:lo + ROW_CHUNK, :]
            w = [cw_ref[kk:kk + 1, sl * LANES:(sl + 1) * LANES] for kk in range(CONV_K)]
            for c in range(n_chunks):
                lo = c * ROW_CHUNK
                y = w[2] * u_s[sl, pad + lo:pad + lo + ROW_CHUNK, :]
                y = y + w[1] * u_s[sl, pad + lo - 1:pad + lo - 1 + ROW_CHUNK, :]
                y = y + w[0] * u_s[sl, pad + lo - 2:pad + lo - 2 + ROW_CHUNK, :]
                out_ref[0, lo:lo + ROW_CHUNK, sl * LANES:(sl + 1) * LANES] = (
                    k_s[sl, lo:lo + ROW_CHUNK, :] * y).astype(BF16)


def _mix_call(x, w_groups, bias_tbl, conv_w):
    batch, seq, d = x.shape
    return pl.pallas_call(
        _mix_kernel,
        out_shape=jax.ShapeDtypeStruct((batch, seq, d), BF16),
        grid=(batch, N_GROUPS),
        in_specs=[
            pl.BlockSpec((1, seq, d), lambda b, g: (b, 0, 0)),
            pl.BlockSpec((1, d, 3 * GROUP_WIDTH), lambda b, g: (g, 0, 0)),
            pl.BlockSpec((1,) + bias_tbl.shape[1:], lambda b, g: (jnp.minimum(g, N_ATTN_GROUPS - 1), 0, 0, 0)),
            pl.BlockSpec(conv_w.shape, lambda b, g: (0, 0)),
        ],
        out_specs=pl.BlockSpec((1, seq, GROUP_WIDTH), lambda b, g: (b, 0, g)),
        scratch_shapes=[
            pltpu.VMEM((seq, d), BF16),
            pltpu.VMEM((2, seq, LANES), F32),
            pltpu.VMEM((2, seq, LANES), F32),
            pltpu.VMEM((2, seq, LANES), F32),
            pltpu.VMEM((len(DILATED_BRANCHES), 2, seq, LANES), F32),
            pltpu.VMEM((len(DILATED_BRANCHES), 2, seq, LANES), F32),
            pltpu.VMEM((2, seq + 8, LANES), F32),
        ],
        compiler_params=pltpu.CompilerParams(
            dimension_semantics=("arbitrary", "arbitrary"), vmem_limit_bytes=VMEM_LIMIT_BYTES),
        name="mix",
    )(x, w_groups, bias_tbl, conv_w)


def _layer_norm(h, g, b):
    mu = jnp.mean(h, axis=-1, keepdims=True)
    c = h - mu
    var = jnp.mean(c * c, axis=-1, keepdims=True)
    return c * lax.rsqrt(var + LN_EPS) * g + b


def _bf16_bits(v):
    return lax.bitcast_convert_type(v.astype(BF16).astype(F32), U32)


def _store_row_tiles(ref, val):
    n = val.shape[0]
    for j in range(ROW_TILE):
        lo = _bf16_bits(val[:, (2 * j) * LANES:(2 * j + 1) * LANES])
        hi = _bf16_bits(val[:, (2 * j + 1) * LANES:(2 * j + 2) * LANES])
        ref[pl.ds(j, n, stride=ROW_TILE), :] = jnp.bitwise_or(lax.shift_right_logical(lo, jnp.uint32(16)), hi)


def _load_row_tiles(ref, n):
    pieces = []
    for j in range(ROW_TILE):
        w = ref[pl.ds(j, n, stride=ROW_TILE), :]
        pieces.append(lax.bitcast_convert_type(lax.shift_left(w, jnp.uint32(16)), F32))
        pieces.append(lax.bitcast_convert_type(jnp.bitwise_and(w, jnp.uint32(0xFFFF0000)), F32))
    return jnp.concatenate(pieces, axis=1)


def _first_argmax_rows(v, row_ids, n_rows):
    m = jnp.max(v, axis=0, keepdims=True)
    idx = jnp.min(jnp.where(v == m, row_ids, n_rows), axis=0, keepdims=True)
    return m, idx


def _post_kernel(mix_ref, x_ref, wo_ref, g_ref, b_ref, wrh_ref, wrl_ref, rb_ref,
                 x1_ref, x1p_ref, e_ref, rank_ref, gate_ref, cnt_ref, carry_s):
    i = pl.program_id(0)
    tm = x_ref.shape[0]

    @pl.when(i == 0)
    def _init():
        carry_s[...] = jnp.zeros_like(carry_s)

    h = DEEPNORM_ALPHA * x_ref[...] + jnp.dot(mix_ref[...], wo_ref[...], preferred_element_type=F32)
    x1 = _layer_norm(h, g_ref[...], b_ref[...])
    x1_ref[...] = x1
    _store_row_tiles(x1p_ref, x1)

    x_hi = x1.astype(BF16)
    x_lo = (x1 - x_hi.astype(F32)).astype(BF16)
    dn = (((1,), (1,)), ((), ()))
    logits = lax.dot_general(wrh_ref[...], x_hi, dn, preferred_element_type=F32)
    logits = logits + lax.dot_general(wrh_ref[...], x_lo, dn, preferred_element_type=F32)
    logits = logits + lax.dot_general(wrl_ref[...], x_hi, dn, preferred_element_type=F32)
    scores = jax.nn.sigmoid(logits)
    biased = scores + rb_ref[...]

    neg_inf = -jnp.inf
    sub_ids = lax.broadcasted_iota(I32, (GROUP_SIZE, tm), 0)
    gs_rows = []
    for gi in range(N_EXPERT_GROUPS):
        bg = biased[gi * GROUP_SIZE:(gi + 1) * GROUP_SIZE]
        m1, i1 = _first_argmax_rows(bg, sub_ids, GROUP_SIZE)
        m2 = jnp.max(jnp.where(sub_ids == i1, neg_inf, bg), axis=0, keepdims=True)
        gs_rows.append(m1 + m2)
    gscore = jnp.concatenate(gs_rows, axis=0)
    grp_ids = lax.broadcasted_iota(I32, (N_EXPERT_GROUPS, tm), 0)
    keep_g = jnp.zeros((N_EXPERT_GROUPS, tm), jnp.bool_)
    for _ in range(TOPK_GROUPS):
        _, gi1 = _first_argmax_rows(gscore, grp_ids, N_EXPERT_GROUPS)
        hit = grp_ids == gi1
        keep_g = jnp.logical_or(keep_g, hit)
        gscore = jnp.where(hit, neg_inf, gscore)
    keep_f = keep_g.astype(F32)
    keep_e = jnp.concatenate(
        [jnp.broadcast_to(keep_f[gi:gi + 1], (GROUP_SIZE, tm)) for gi in range(N_EXPERT_GROUPS)], axis=0)
    masked = jnp.where(keep_e > 0.5, biased, neg_inf)

    exp_ids = lax.broadcasted_iota(I32, (N_EXPERTS, tm), 0)
    sel = jnp.zeros((N_EXPERTS, tm), F32)
    e_rows, g_rows = [], []
    for _ in range(TOP_K):
        _, e1 = _first_argmax_rows(masked, exp_ids, N_EXPERTS)
        hit = exp_ids == e1
        sel = jnp.where(hit, 1.0, sel)
        g_rows.append(jnp.sum(jnp.where(hit, scores, 0.0), axis=0, keepdims=True))
        e_rows.append(e1)
        masked = jnp.where(hit, neg_inf, masked)
    gates = jnp.concatenate(g_rows, axis=0)
    gates = gates / jnp.sum(gates, axis=0, keepdims=True) * ROUTED_SCALE
    gate_ref[...] = gates
    e_ref[...] = jnp.concatenate(e_rows, axis=0)

    t_row = lax.broadcasted_iota(I32, (tm, tm), 0)
    t_col = lax.broadcasted_iota(I32, (tm, tm), 1)
    upper = (t_row < t_col).astype(BF16)
    excl = jnp.dot(sel.astype(BF16), upper, preferred_element_type=F32)
    base = carry_s[...] + excl
    r_rows = [jnp.sum(jnp.where(exp_ids == e1, base, 0.0), axis=0, keepdims=True) for e1 in e_rows]
    rank_ref[...] = jnp.concatenate(r_rows, axis=0).astype(I32)
    carry = carry_s[...] + jnp.sum(sel, axis=1, keepdims=True)
    carry_s[...] = carry
    cnt_ref[...] = carry.astype(I32)


def _post_call(mix2d, x2d, wo, ln_g, ln_b, wr_hi, wr_lo, rb):
    n_tok, d = x2d.shape
    tm = POST_TILE
    tok_spec = pl.BlockSpec((tm, d), lambda i: (i, 0))
    k_spec = pl.BlockSpec((TOP_K, tm), lambda i: (0, i))
    full = lambda a: pl.BlockSpec(a.shape, lambda i: (0,) * a.ndim)
    return pl.pallas_call(
        _post_kernel,
        out_shape=(
            jax.ShapeDtypeStruct((n_tok, d), F32),
            jax.ShapeDtypeStruct((n_tok * ROW_TILE, LANES), U32),
            jax.ShapeDtypeStruct((TOP_K, n_tok), I32),
            jax.ShapeDtypeStruct((TOP_K, n_tok), I32),
            jax.ShapeDtypeStruct((TOP_K, n_tok), F32),
            jax.ShapeDtypeStruct((N_EXPERTS, 1), I32),
        ),
        grid=(n_tok // tm,),
        in_specs=[tok_spec, tok_spec, full(wo), full(ln_g), full(ln_b), full(wr_hi), full(wr_lo), full(rb)],
        out_specs=(tok_spec, pl.BlockSpec((tm * ROW_TILE, LANES), lambda i: (i, 0)), k_spec, k_spec, k_spec,
                   pl.BlockSpec((N_EXPERTS, 1), lambda i: (0, 0))),
        scratch_shapes=[pltpu.VMEM((N_EXPERTS, 1), F32)],
        compiler_params=pltpu.CompilerParams(
            dimension_semantics=("arbitrary",), vmem_limit_bytes=VMEM_LIMIT_BYTES),
        name="post",
    )(mix2d, x2d, wo, ln_g, ln_b, wr_hi, wr_lo, rb)


def _pos_kernel(e_ref, rank_ref, ps_ref, pos_ref):
    tp = e_ref.shape[1]
    exp_ids = lax.broadcasted_iota(I32, (N_EXPERTS, tp), 0)
    ps = ps_ref[...]
    rows = []
    for k in range(TOP_K):
        start = jnp.sum(jnp.where(exp_ids == e_ref[k:k + 1, :], ps, 0), axis=0, keepdims=True)
        rows.append((start + rank_ref[k:k + 1, :]) * ROW_TILE)
    pos_ref[...] = jnp.concatenate(rows, axis=0)


def _pos_call(e_idx, rank, pstarts):
    n_tok = e_idx.shape[1]
    tp = POS_TILE
    k_spec = pl.BlockSpec((TOP_K, tp), lambda i: (0, i))
    return pl.pallas_call(
        _pos_kernel,
        out_shape=jax.ShapeDtypeStruct((TOP_K, n_tok), I32),
        grid=(n_tok // tp,),
        in_specs=[k_spec, k_spec, pl.BlockSpec((N_EXPERTS, 1), lambda i: (0, 0))],
        out_specs=k_spec,
        compiler_params=pltpu.CompilerParams(dimension_semantics=("arbitrary",)),
        name="positions",
    )(e_idx, rank, pstarts)


def _dispatch_kernel(cnt_ref, ps_ref, pos_ref, x_ref, xs_ref, zbuf, sem, zsem):
    i = pl.program_id(0)
    tt = x_ref.shape[0] // ROW_TILE

    def pad_copies(e, fn):
        cnt = cnt_ref[e]
        pad = jnp.bitwise_and(-cnt, MOE_BLOCK - 1)
        off = ps_ref[e] + cnt
        for b in range(int(math.log2(MOE_BLOCK))):
            size = 1 << b
            hit = jnp.bitwise_and(pad, size)

            @pl.when(hit != 0)
            def _(off=off, size=size):
                fn(pltpu.make_async_copy(
                    zbuf.at[pl.ds(0, size * ROW_TILE)],
                    xs_ref.at[pl.ds(pl.multiple_of(off * ROW_TILE, ROW_TILE), size * ROW_TILE)], zsem))
            off = off + hit

    @pl.when(i == 0)
    def _zero_padding_rows():
        zbuf[...] = jnp.zeros_like(zbuf)

        def start(e, carry):
            pad_copies(e, lambda cp: cp.start())
            return carry

        def wait(e, carry):
            pad_copies(e, lambda cp: cp.wait())
            return carry
        lax.fori_loop(0, N_EXPERTS, start, 0)
        lax.fori_loop(0, N_EXPERTS, wait, 0)

    def row_copy(t, k):
        src = x_ref.at[pl.ds(t * ROW_TILE, ROW_TILE)]
        dst = xs_ref.at[pl.ds(pl.multiple_of(pos_ref[k, t], ROW_TILE), ROW_TILE)]
        return pltpu.make_async_copy(src, dst, sem)

    for t in range(tt):
        for k in range(TOP_K):
            row_copy(t, k).start(priority=k % 2)
    for t in range(tt):
        for k in range(TOP_K):
            row_copy(t, k).wait()


def _dispatch_call(counts, pstarts, pos, x1r, n_rows):
    n_tok = pos.shape[1]
    tt = DISPATCH_TILE
    grid_spec = pltpu.PrefetchScalarGridSpec(
        num_scalar_prefetch=2,
        grid=(n_tok // tt,),
        in_specs=[
            pl.BlockSpec((TOP_K, tt), lambda i, c, p: (0, i), memory_space=pltpu.SMEM),
            pl.BlockSpec((tt * ROW_TILE, LANES), lambda i, c, p: (i, 0)),
        ],
        out_specs=pl.BlockSpec(memory_space=pl.ANY),
        scratch_shapes=[
            pltpu.VMEM((MOE_BLOCK // 2 * ROW_TILE, LANES), U32),
            pltpu.SemaphoreType.DMA(()),
            pltpu.SemaphoreType.DMA(()),
        ],
    )
    return pl.pallas_call(
        _dispatch_kernel,
        out_shape=jax.ShapeDtypeStruct((n_rows * ROW_TILE, LANES), U32),
        grid_spec=grid_spec,
        compiler_params=pltpu.CompilerParams(
            dimension_semantics=("arbitrary",), vmem_limit_bytes=VMEM_LIMIT_BYTES),
        name="dispatch",
    )(counts, pstarts, pos, x1r)


def _experts_kernel(ps_ref, nb_ref, xs_ref, wg_ref, wu_ref, wd_ref, y_ref,
                    xbuf, ybuf, wg_b, wu_b, wd_b, sem_in, sem_out):
    e = pl.program_id(0)
    nb = nb_ref[e]
    blk_rows = MOE_BLOCK * ROW_TILE
    g0 = ps_ref[e] // MOE_BLOCK
    n_total = (ps_ref[N_EXPERTS - 1] // MOE_BLOCK) + nb_ref[N_EXPERTS - 1]
    ahead = EXPERT_SLOTS - 1

    def rows_of(g):
        return pl.ds(pl.multiple_of(g * blk_rows, blk_rows), blk_rows)

    def slot_of(g):
        return jnp.bitwise_and(g, EXPERT_SLOTS - 1)

    def in_copy(g):
        s = slot_of(g)
        return pltpu.make_async_copy(xs_ref.at[rows_of(g)], xbuf.at[s], sem_in.at[s])

    def out_copy(g):
        s = slot_of(g)
        return pltpu.make_async_copy(ybuf.at[s], y_ref.at[rows_of(g)], sem_out.at[s])

    @pl.when(e == 0)
    def _prime():
        for u in range(ahead):
            @pl.when(u < n_total)
            def _(u=u):
                in_copy(u).start()

    @pl.when(nb > 0)
    def _run():
        wg_b[...] = wg_ref[0].astype(BF16)
        wu_b[...] = wu_ref[0].astype(BF16)
        wd_b[...] = wd_ref[0].astype(BF16)

        def block(j, carry):
            g = g0 + j
            s = slot_of(g)
            in_copy(g).wait()

            @pl.when(g + ahead < n_total)
            def _prefetch():
                in_copy(g + ahead).start()

            @pl.when(g >= EXPERT_SLOTS)
            def _reclaim():
                out_copy(g - EXPERT_SLOTS).wait()

            xb = _load_row_tiles(xbuf.at[s], MOE_BLOCK).astype(BF16)
            hg = jnp.dot(xb, wg_b[...], preferred_element_type=F32)
            hu = jnp.dot(xb, wu_b[...], preferred_element_type=F32)
            hh = (hg * jax.nn.sigmoid(hg) * hu).astype(BF16)
            _store_row_tiles(ybuf.at[s], jnp.dot(hh, wd_b[...], preferred_element_type=F32))
            out_copy(g).start()
            return carry
        lax.fori_loop(0, nb, block, 0)

    @pl.when(e == N_EXPERTS - 1)
    def _drain():
        for u in range(EXPERT_SLOTS):
            @pl.when(u < n_total)
            def _(u=u):
                out_copy(u).wait()


def _experts_call(pstarts, n_blk, xs, w_gate, w_up, w_down):
    d = D_MODEL

    def wsel(e, ps, nb):
        return (e, 0, 0)
    grid_spec = pltpu.PrefetchScalarGridSpec(
        num_scalar_prefetch=2,
        grid=(N_EXPERTS,),
        in_specs=[
            pl.BlockSpec(memory_space=pl.ANY),
            pl.BlockSpec((1, d, EXPERT_DIM), wsel),
            pl.BlockSpec((1, d, EXPERT_DIM), wsel),
            pl.BlockSpec((1, EXPERT_DIM, d), wsel),
        ],
        out_specs=pl.BlockSpec(memory_space=pl.ANY),
        scratch_shapes=[
            pltpu.VMEM((EXPERT_SLOTS, MOE_BLOCK * ROW_TILE, LANES), U32),
            pltpu.VMEM((EXPERT_SLOTS, MOE_BLOCK * ROW_TILE, LANES), U32),
            pltpu.VMEM((d, EXPERT_DIM), BF16),
            pltpu.VMEM((d, EXPERT_DIM), BF16),
            pltpu.VMEM((EXPERT_DIM, d), BF16),
            pltpu.SemaphoreType.DMA((EXPERT_SLOTS,)),
            pltpu.SemaphoreType.DMA((EXPERT_SLOTS,)),
        ],
    )
    return pl.pallas_call(
        _experts_kernel,
        out_shape=jax.ShapeDtypeStruct(xs.shape, U32),
        grid_spec=grid_spec,
        compiler_params=pltpu.CompilerParams(
            dimension_semantics=("arbitrary",), vmem_limit_bytes=VMEM_LIMIT_BYTES),
        name="experts",
    )(pstarts, n_blk, xs, w_gate, w_up, w_down)


def _combine_kernel(pos_ref, gate_ref, x1_ref, y_ref, sg_ref, su_ref, sd_ref, g_ref, b_ref, out_ref, buf, sem):
    tt = x1_ref.shape[0]
    part = tt // COMBINE_PARTS

    def row_copy(t, k):
        src = y_ref.at[pl.ds(pl.multiple_of(pos_ref[k, t], ROW_TILE), ROW_TILE)]
        dst = buf.at[k, pl.ds(t * ROW_TILE, ROW_TILE)]
        return pltpu.make_async_copy(src, dst, sem.at[t // part])

    def issue(q):
        for t in range(q * part, (q + 1) * part):
            for k in range(TOP_K):
                row_copy(t, k).start(priority=k % 2)

    def wait(q):
        for t in range(q * part, (q + 1) * part):
            for k in range(TOP_K):
                row_copy(t, k).wait()

    def reduce(q, acc):
        lo = q * part
        a = acc[lo:lo + part]
        gates = gate_ref[lo:lo + part, :]
        for k in range(TOP_K):
            rows = buf.at[k, pl.ds(lo * ROW_TILE, part * ROW_TILE)]
            a = a + gates[:, k:k + 1] * _load_row_tiles(rows, part)
        out_ref[lo:lo + part, :] = _layer_norm(a, g_ref[...], b_ref[...])

    issue(0)
    issue(1)
    x1 = x1_ref[...]
    xb = x1.astype(BF16)
    hg = jnp.dot(xb, sg_ref[...], preferred_element_type=F32)
    hu = jnp.dot(xb, su_ref[...], preferred_element_type=F32)
    hh = (hg * jax.nn.sigmoid(hg) * hu).astype(BF16)
    acc = DEEPNORM_ALPHA * x1 + jnp.dot(hh, sd_ref[...], preferred_element_type=F32)
    for q in range(COMBINE_PARTS):
        wait(q)
        if q + 2 < COMBINE_PARTS:
            issue(q + 2)
        reduce(q, acc)


def _combine_call(pos, gates_t, x1, y, sg, su, sd, ln_g, ln_b):
    n_tok, d = x1.shape
    tt = COMBINE_TILE
    full = lambda a: pl.BlockSpec(a.shape, lambda i: (0,) * a.ndim)
    return pl.pallas_call(
        _combine_kernel,
        out_shape=jax.ShapeDtypeStruct((n_tok, d), F32),
        grid=(n_tok // tt,),
        in_specs=[
            pl.BlockSpec((TOP_K, tt), lambda i: (0, i), memory_space=pltpu.SMEM),
            pl.BlockSpec((tt, TOP_K), lambda i: (i, 0)),
            pl.BlockSpec((tt, d), lambda i: (i, 0)),
            pl.BlockSpec(memory_space=pl.ANY),
            full(sg), full(su), full(sd), full(ln_g), full(ln_b),
        ],
        out_specs=pl.BlockSpec((tt, d), lambda i: (i, 0)),
        scratch_shapes=[pltpu.VMEM((TOP_K, tt * ROW_TILE, LANES), U32),
                        pltpu.SemaphoreType.DMA((COMBINE_PARTS,))],
        compiler_params=pltpu.CompilerParams(
            dimension_semantics=("arbitrary",), vmem_limit_bytes=VMEM_LIMIT_BYTES),
        name="combine",
    )(pos, gates_t, x1, y, sg, su, sd, ln_g, ln_b)


def _group_weights(w_in):
    blocks = w_in.reshape(D_MODEL, 12, GROUP_WIDTH)
    scale = HEAD_DIM ** -0.5
    groups = []
    for g in range(N_ATTN_GROUPS):
        groups.append(jnp.concatenate(
            [blocks[:, g] * scale, blocks[:, N_ATTN_GROUPS + g], blocks[:, 2 * N_ATTN_GROUPS + g]], axis=1))
    groups.append(jnp.concatenate([blocks[:, 9], blocks[:, 10], blocks[:, 11]], axis=1))
    return jnp.stack(groups).astype(BF16)


def kernel(x, w_in, conv_w, w_o, ln1_g, ln1_b, rel_bias, w_router, router_bias, exp_w_gate, exp_w_up,
           exp_w_down, sh_w_gate, sh_w_up, sh_w_down, ln2_g, ln2_b):
    batch, seq, d = x.shape
    assert d == D_MODEL and seq == DILATED_BRANCHES[-1][0] and w_in.shape[0] == DEPTH
    n_tok = batch * seq
    assert n_tok % POST_TILE == 0 and n_tok % DISPATCH_TILE == 0 and n_tok % COMBINE_TILE == 0

    mix = _mix_call(x, _group_weights(w_in[0]), _bias_table(rel_bias), conv_w[0])

    wr = w_router[0]
    wr_hi = wr.astype(BF16)
    wr_lo = (wr - wr_hi.astype(F32)).astype(BF16)
    x2d = x.reshape(n_tok, d)
    x1, x1p, e_idx, rank, gates, counts = _post_call(
        mix.reshape(n_tok, d), x2d, w_o[0].astype(BF16), ln1_g, ln1_b,
        wr_hi.T, wr_lo.T, router_bias[0].reshape(N_EXPERTS, 1))

    counts = counts[:, 0]
    padded = (counts + MOE_BLOCK - 1) // MOE_BLOCK * MOE_BLOCK
    pends = jnp.cumsum(padded)
    pstarts = (pends - padded).astype(I32)
    n_blocks = n_tok * TOP_K // MOE_BLOCK + N_EXPERTS
    n_blk = (padded // MOE_BLOCK).astype(I32)
    pos = _pos_call(e_idx, rank, pstarts.reshape(N_EXPERTS, 1))

    xs = _dispatch_call(counts, pstarts, pos, x1p, n_blocks * MOE_BLOCK)
    y = _experts_call(pstarts, n_blk, xs, exp_w_gate[0], exp_w_up[0], exp_w_down[0])

    out = _combine_call(pos, gates.T, x1, y, sh_w_gate[0].astype(BF16), sh_w_up[0].astype(BF16),
                        sh_w_down[0].astype(BF16), ln2_g, ln2_b)
    return out.reshape(batch, seq, d)
```

```python
import functools
import math

import numpy as np
import jax
import jax.numpy as jnp
from jax import lax
from jax.experimental import pallas as pl
from jax.experimental.pallas import tpu as pltpu

F32 = jnp.float32
BF16 = jnp.bfloat16
I32 = jnp.int32
U32 = jnp.uint32

D_MODEL = 1024
HEAD_DIM = 64
N_ATTN_HEADS = 12
ATTN_WIDTH = N_ATTN_HEADS * HEAD_DIM
CONV_WIDTH = D_MODEL - ATTN_WIDTH
CONV_K = 3
DILATED_BRANCHES = ((128, 1), (512, 4), (2048, 16))
ATTN_BLOCK = 128
N_REL_BUCKETS = 32
REL_MAX_DISTANCE = 2048
N_EXPERTS = 256
TOP_K = 8
N_EXPERT_GROUPS = 8
TOPK_GROUPS = 4
GROUP_SIZE = N_EXPERTS // N_EXPERT_GROUPS
EXPERT_DIM = 256
SHARED_DIM = 256
ROUTED_SCALE = 2.5
DEPTH = 1
DEEPNORM_ALPHA = (2.0 * DEPTH) ** 0.25
LN_EPS = 1e-5

LANES = 128
GROUP_WIDTH = 2 * LANES
HEADS_PER_GROUP = GROUP_WIDTH // HEAD_DIM
N_GROUPS = D_MODEL // GROUP_WIDTH
N_ATTN_GROUPS = ATTN_WIDTH // GROUP_WIDTH
VMEM_LIMIT_BYTES = 56 * 1024 * 1024

MASK_VALUE = -1e30

ROW_CHUNK = 256
ATTN_UNROLL = 8
CONV_PAD = 8
ROW_TILE = 4
POS_TILE = 2048
POST_TILE = 512
MOE_BLOCK = 512
EXPERT_SLOTS = 4
DISPATCH_TILE = 256
COMBINE_TILE = 256
COMBINE_PARTS = 4


def _t5_bucket(dist):
    max_exact = N_REL_BUCKETS // 2
    dist = np.asarray(dist)
    log_part = np.log(np.maximum(dist, 1) / max_exact) / math.log(REL_MAX_DISTANCE / max_exact)
    large = max_exact + (log_part * (N_REL_BUCKETS - max_exact)).astype(np.int32)
    large = np.minimum(large, N_REL_BUCKETS - 1)
    return np.where(dist < max_exact, dist, large).astype(np.int32)


def _branch_bucket_and_band(window, dilation):
    w_sub = window // dilation
    qi = np.arange(ATTN_BLOCK)[:, None]
    ki = np.arange(2 * ATTN_BLOCK)[None, :]
    steps = qi + ATTN_BLOCK - ki
    band = (steps >= 0) & (steps <= w_sub)
    bucket = _t5_bucket(np.clip(steps, 0, w_sub) * dilation)
    return bucket, band


def _bias_table(rel_bias):
    tabs = []
    for window, dilation in DILATED_BRANCHES:
        bucket, band = _branch_bucket_and_band(window, dilation)
        onehot = (bucket[:, :, None] == np.arange(N_REL_BUCKETS)).astype(np.float32)
        b = jnp.einsum("qkb,bh->qkh", onehot, rel_bias.astype(F32), precision=lax.Precision.HIGHEST)
        b = jnp.where(band[:, :, None], b, MASK_VALUE)
        tabs.append(b.transpose(2, 0, 1))
    t = jnp.stack(tabs, axis=1)
    t = t.reshape(N_ATTN_GROUPS, HEADS_PER_GROUP, len(DILATED_BRANCHES), ATTN_BLOCK, 2 * ATTN_BLOCK)
    t = t.transpose(0, 2, 1, 3, 4)
    return t.reshape(N_ATTN_GROUPS, len(DILATED_BRANCHES), HEADS_PER_GROUP * ATTN_BLOCK, 2 * ATTN_BLOCK)


def _lane_head():
    return lax.shift_right_logical(lax.broadcasted_iota(I32, (ATTN_BLOCK, GROUP_WIDTH), 1), 6)


def _attn_steps(its, bi, dilation, first, q_s, k_s, v_s, bias_ref, o_s, m_s, l_s):
    logd = int(math.log2(dilation))

    def rows(st):
        return pl.ds(st, ATTN_BLOCK) if dilation == 1 else pl.ds(st, ATTN_BLOCK, stride=dilation)

    def ld(ref, st):
        return jnp.concatenate([ref[0, rows(st), :], ref[1, rows(st), :]], axis=1)

    lane_head = _lane_head()
    bias = bias_ref[0, bi, :, ATTN_BLOCK:] if first else bias_ref[0, bi]
    starts, operands = [], []
    for it in its:
        r = jnp.bitwise_and(it, dilation - 1)
        n = lax.shift_right_logical(it, logd)
        start = r + (dilation * ATTN_BLOCK) * n
        qf = ld(q_s, start)
        if first:
            kk = ld(k_s, start).astype(BF16)
            vv = ld(v_s, start).astype(BF16)
        else:
            prev = start - dilation * ATTN_BLOCK
            kk = jnp.concatenate([ld(k_s, prev), ld(k_s, start)], axis=0).astype(BF16)
            vv = jnp.concatenate([ld(v_s, prev), ld(v_s, start)], axis=0).astype(BF16)
        starts.append(start)
        operands.append((qf, kk, vv))

    results = []
    for qf, kk, vv in operands:
        q4 = jnp.concatenate(
            [jnp.where(lane_head == h, qf, 0.0) for h in range(HEADS_PER_GROUP)], axis=0).astype(BF16)
        s = lax.dot_general(q4, kk, (((1,), (1,)), ((), ())), preferred_element_type=F32) + bias
        m = jnp.max(s, axis=-1, keepdims=True)
        p = jnp.exp(s - m)
        l = jnp.sum(p, axis=-1, keepdims=True)
        pv = jnp.dot(p.astype(BF16), vv, preferred_element_type=F32)
        o = jnp.zeros((ATTN_BLOCK, GROUP_WIDTH), F32)
        mb = jnp.zeros((ATTN_BLOCK, GROUP_WIDTH), F32)
        lb = jnp.zeros((ATTN_BLOCK, GROUP_WIDTH), F32)
        for h in range(HEADS_PER_GROUP):
            sel = lane_head == h
            hr = slice(h * ATTN_BLOCK, (h + 1) * ATTN_BLOCK)
            o = jnp.where(sel, pv[hr], o)
            mb = jnp.where(sel, m[hr], mb)
            lb = jnp.where(sel, l[hr], lb)
        results.append((o, mb, lb))

    for start, (o, mb, lb) in zip(starts, results):
        for sl in range(2):
            lanes = slice(sl * LANES, (sl + 1) * LANES)
            o_s[bi, sl, rows(start), :] = o[:, lanes]
            m_s[bi, sl, rows(start), :] = mb[:, lanes]
            l_s[bi, sl, rows(start), :] = lb[:, lanes]


def _attn_range(lo, hi, **kw):
    n_groups = (hi - lo) // ATTN_UNROLL

    def body(j, carry):
        base = lo + j * ATTN_UNROLL
        _attn_steps([base + u for u in range(ATTN_UNROLL)], **kw)
        return carry
    if n_groups > 0:
        lax.fori_loop(0, n_groups, body, 0)
    tail = list(range(lo + n_groups * ATTN_UNROLL, hi))
    if tail:
        _attn_steps([jnp.int32(t) for t in tail], **kw)


def _mix_kernel(x_ref, w_ref, bias_ref, cw_ref, out_ref, q_s, k_s, v_s, o_s, m_s, l_s):
    g = pl.program_id(1)
    seq = x_ref.shape[1]
    n_chunks = seq // ROW_CHUNK

    def proj(c, carry):
        rows = pl.ds(pl.multiple_of(c * ROW_CHUNK, ROW_CHUNK), ROW_CHUNK)
        res = jnp.dot(x_ref[0, rows, :].astype(BF16), w_ref[0], preferred_element_type=F32)
        for j, dst in enumerate((q_s, k_s, v_s)):
            for sl in range(2):
                lo = j * GROUP_WIDTH + sl * LANES
                dst[sl, rows, :] = res[:, lo:lo + LANES]
        return carry
    lax.fori_loop(0, n_chunks, proj, 0)

    @pl.when(g < N_ATTN_GROUPS)
    def _attention():
        for bi, (window, dilation) in enumerate(DILATED_BRANCHES):
            n_blocks = seq // dilation // ATTN_BLOCK
            n_steps = n_blocks * dilation
            kw = dict(bi=bi, dilation=dilation, q_s=q_s, k_s=k_s, v_s=v_s,
                      bias_ref=bias_ref, o_s=o_s, m_s=m_s, l_s=l_s)
            _attn_range(0, dilation, first=True, **kw)
            _attn_range(dilation, n_steps, first=False, **kw)

        def combine(c, carry):
            rows = pl.ds(pl.multiple_of(c * ROW_CHUNK, ROW_CHUNK), ROW_CHUNK)
            for sl in range(2):
                ms = [m_s[bi, sl, rows, :] for bi in range(len(DILATED_BRANCHES))]
                mx = jnp.maximum(jnp.maximum(ms[0], ms[1]), ms[2])
                es = [jnp.exp(v - mx) for v in ms]
                den = es[0] * l_s[0, sl, rows, :] + es[1] * l_s[1, sl, rows, :] + es[2] * l_s[2, sl, rows, :]
                num = es[0] * o_s[0, sl, rows, :] + es[1] * o_s[1, sl, rows, :] + es[2] * o_s[2, sl, rows, :]
                out_ref[0, rows, sl * LANES:(sl + 1) * LANES] = (num / den).astype(BF16)
            return carry
        lax.fori_loop(0, n_chunks, combine, 0)

    @pl.when(g == N_ATTN_GROUPS)
    def _short_conv():
        pad = CONV_PAD
        u_s = o_s.at[0]
        for sl in range(2):
            u_s[sl, 0:pad, :] = jnp.zeros((pad, LANES), F32)
            for c in range(n_chunks):
                lo = c * ROW_CHUNK
                u_s[sl, pad + lo:pad + lo + ROW_CHUNK, :] = v_s[sl, lo:lo + ROW_CHUNK, :] * q_s[sl, lo:lo + ROW_CHUNK, :]
            w = [cw_ref[kk:kk + 1, sl * LANES:(sl + 1) * LANES] for kk in range(CONV_K)]
            for c in range(n_chunks):
                lo = c * ROW_CHUNK
                y = w[2] * u_s[sl, pad + lo:pad + lo + ROW_CHUNK, :]
                y = y + w[1] * u_s[sl, pad + lo - 1:pad + lo - 1 + ROW_CHUNK, :]
                y = y + w[0] * u_s[sl, pad + lo - 2:pad + lo - 2 + ROW_CHUNK, :]
                out_ref[0, lo:lo + ROW_CHUNK, sl * LANES:(sl + 1) * LANES] = (
                    k_s[sl, lo:lo + ROW_CHUNK, :] * y).astype(BF16)


def _mix_call(x, w_groups, bias_tbl, conv_w):
    batch, seq, d = x.shape
    return pl.pallas_call(
        _mix_kernel,
        out_shape=jax.ShapeDtypeStruct((batch, seq, d), BF16),
        grid=(batch, N_GROUPS),
        in_specs=[
            pl.BlockSpec((1, seq, d), lambda b, g: (b, 0, 0)),
            pl.BlockSpec((1, d, 3 * GROUP_WIDTH), lambda b, g: (g, 0, 0)),
            pl.BlockSpec((1,) + bias_tbl.shape[1:], lambda b, g: (jnp.minimum(g, N_ATTN_GROUPS - 1), 0, 0, 0)),
            pl.BlockSpec(conv_w.shape, lambda b, g: (0, 0)),
        ],
        out_specs=pl.BlockSpec((1, seq, GROUP_WIDTH), lambda b, g: (b, 0, g)),
        scratch_shapes=[
            pltpu.VMEM((2, seq, LANES), F32),
            pltpu.VMEM((2, seq, LANES), F32),
            pltpu.VMEM((2, seq, LANES), F32),
            pltpu.VMEM((len(DILATED_BRANCHES), 2, seq + CONV_PAD, LANES), F32),
            pltpu.VMEM((len(DILATED_BRANCHES), 2, seq, LANES), F32),
            pltpu.VMEM((len(DILATED_BRANCHES), 2, seq, LANES), F32),
        ],
        compiler_params=pltpu.CompilerParams(
            dimension_semantics=("arbitrary", "arbitrary"), vmem_limit_bytes=VMEM_LIMIT_BYTES),
        name="mix",
    )(x, w_groups, bias_tbl, conv_w)


def _layer_norm(h, g, b):
    mu = jnp.mean(h, axis=-1, keepdims=True)
    c = h - mu
    var = jnp.mean(c * c, axis=-1, keepdims=True)
    return c * lax.rsqrt(var + LN_EPS) * g + b


def _bf16_bits(v):
    return lax.bitcast_convert_type(v.astype(BF16).astype(F32), U32)


def _store_row_tiles(ref, val):
    n = val.shape[0]
    for j in range(ROW_TILE):
        lo = _bf16_bits(val[:, (2 * j) * LANES:(2 * j + 1) * LANES])
        hi = _bf16_bits(val[:, (2 * j + 1) * LANES:(2 * j + 2) * LANES])
        ref[pl.ds(j, n, stride=ROW_TILE), :] = jnp.bitwise_or(lax.shift_right_logical(lo, jnp.uint32(16)), hi)


def _load_row_tiles(ref, n):
    pieces = []
    for j in range(ROW_TILE):
        w = ref[pl.ds(j, n, stride=ROW_TILE), :]
        pieces.append(lax.bitcast_convert_type(lax.shift_left(w, jnp.uint32(16)), F32))
        pieces.append(lax.bitcast_convert_type(jnp.bitwise_and(w, jnp.uint32(0xFFFF0000)), F32))
    return jnp.concatenate(pieces, axis=1)


def _first_argmax_rows(v, row_ids, n_rows):
    m = jnp.max(v, axis=0, keepdims=True)
    idx = jnp.min(jnp.where(v == m, row_ids, n_rows), axis=0, keepdims=True)
    return m, idx


def _post_kernel(mix_ref, x_ref, wo_ref, g_ref, b_ref, wrh_ref, wrl_ref, rb_ref,
                 x1_ref, x1p_ref, e_ref, rank_ref, gate_ref, cnt_ref, carry_s):
    i = pl.program_id(0)
    tm = x_ref.shape[0]

    @pl.when(i == 0)
    def _init():
        carry_s[...] = jnp.zeros_like(carry_s)

    h = DEEPNORM_ALPHA * x_ref[...] + jnp.dot(mix_ref[...], wo_ref[...], preferred_element_type=F32)
    x1 = _layer_norm(h, g_ref[...], b_ref[...])
    x1_ref[...] = x1
    _store_row_tiles(x1p_ref, x1)

    x_hi = x1.astype(BF16)
    x_lo = (x1 - x_hi.astype(F32)).astype(BF16)
    dn = (((1,), (1,)), ((), ()))
    logits = lax.dot_general(wrh_ref[...], x_hi, dn, preferred_element_type=F32)
    logits = logits + lax.dot_general(wrh_ref[...], x_lo, dn, preferred_element_type=F32)
    logits = logits + lax.dot_general(wrl_ref[...], x_hi, dn, preferred_element_type=F32)
    scores = jax.nn.sigmoid(logits)
    biased = scores + rb_ref[...]

    neg_inf = -jnp.inf
    sub_ids = lax.broadcasted_iota(I32, (GROUP_SIZE, tm), 0)
    gs_rows = []
    for gi in range(N_EXPERT_GROUPS):
        bg = biased[gi * GROUP_SIZE:(gi + 1) * GROUP_SIZE]
        m1, i1 = _first_argmax_rows(bg, sub_ids, GROUP_SIZE)
        m2 = jnp.max(jnp.where(sub_ids == i1, neg_inf, bg), axis=0, keepdims=True)
        gs_rows.append(m1 + m2)
    gscore = jnp.concatenate(gs_rows, axis=0)
    grp_ids = lax.broadcasted_iota(I32, (N_EXPERT_GROUPS, tm), 0)
    keep_g = jnp.zeros((N_EXPERT_GROUPS, tm), jnp.bool_)
    for _ in range(TOPK_GROUPS):
        _, gi1 = _first_argmax_rows(gscore, grp_ids, N_EXPERT_GROUPS)
        hit = grp_ids == gi1
        keep_g = jnp.logical_or(keep_g, hit)
        gscore = jnp.where(hit, neg_inf, gscore)
    keep_f = keep_g.astype(F32)
    keep_e = jnp.concatenate(
        [jnp.broadcast_to(keep_f[gi:gi + 1], (GROUP_SIZE, tm)) for gi in range(N_EXPERT_GROUPS)], axis=0)
    masked = jnp.where(keep_e > 0.5, biased, neg_inf)

    exp_ids = lax.broadcasted_iota(I32, (N_EXPERTS, tm), 0)
    sel = jnp.zeros((N_EXPERTS, tm), F32)
    e_rows, g_rows = [], []
    for _ in range(TOP_K):
        _, e1 = _first_argmax_rows(masked, exp_ids, N_EXPERTS)
        hit = exp_ids == e1
        sel = jnp.where(hit, 1.0, sel)
        g_rows.append(jnp.sum(jnp.where(hit, scores, 0.0), axis=0, keepdims=True))
        e_rows.append(e1)
        masked = jnp.where(hit, neg_inf, masked)
    gates = jnp.concatenate(g_rows, axis=0)
    gates = gates / jnp.sum(gates, axis=0, keepdims=True) * ROUTED_SCALE
    gate_ref[...] = gates
    e_ref[...] = jnp.concatenate(e_rows, axis=0)

    t_row = lax.broadcasted_iota(I32, (tm, tm), 0)
    t_col = lax.broadcasted_iota(I32, (tm, tm), 1)
    upper = (t_row < t_col).astype(BF16)
    excl = jnp.dot(sel.astype(BF16), upper, preferred_element_type=F32)
    base = carry_s[...] + excl
    r_rows = [jnp.sum(jnp.where(exp_ids == e1, base, 0.0), axis=0, keepdims=True) for e1 in e_rows]
    rank_ref[...] = jnp.concatenate(r_rows, axis=0).astype(I32)
    carry = carry_s[...] + jnp.sum(sel, axis=1, keepdims=True)
    carry_s[...] = carry
    cnt_ref[...] = carry.astype(I32)


def _post_call(mix2d, x2d, wo, ln_g, ln_b, wr_hi, wr_lo, rb):
    n_tok, d = x2d.shape
    tm = POST_TILE
    tok_spec = pl.BlockSpec((tm, d), lambda i: (i, 0))
    k_spec = pl.BlockSpec((TOP_K, tm), lambda i: (0, i))
    full = lambda a: pl.BlockSpec(a.shape, lambda i: (0,) * a.ndim)
    return pl.pallas_call(
        _post_kernel,
        out_shape=(
            jax.ShapeDtypeStruct((n_tok, d), F32),
            jax.ShapeDtypeStruct((n_tok * ROW_TILE, LANES), U32),
            jax.ShapeDtypeStruct((TOP_K, n_tok), I32),
            jax.ShapeDtypeStruct((TOP_K, n_tok), I32),
            jax.ShapeDtypeStruct((TOP_K, n_tok), F32),
            jax.ShapeDtypeStruct((N_EXPERTS, 1), I32),
        ),
        grid=(n_tok // tm,),
        in_specs=[tok_spec, tok_spec, full(wo), full(ln_g), full(ln_b), full(wr_hi), full(wr_lo), full(rb)],
        out_specs=(tok_spec, pl.BlockSpec((tm * ROW_TILE, LANES), lambda i: (i, 0)), k_spec, k_spec, k_spec,
                   pl.BlockSpec((N_EXPERTS, 1), lambda i: (0, 0))),
        scratch_shapes=[pltpu.VMEM((N_EXPERTS, 1), F32)],
        compiler_params=pltpu.CompilerParams(
            dimension_semantics=("arbitrary",), vmem_limit_bytes=VMEM_LIMIT_BYTES),
        name="post",
    )(mix2d, x2d, wo, ln_g, ln_b, wr_hi, wr_lo, rb)


def _pos_kernel(e_ref, rank_ref, ps_ref, pos_ref):
    tp = e_ref.shape[1]
    exp_ids = lax.broadcasted_iota(I32, (N_EXPERTS, tp), 0)
    ps = ps_ref[...]
    rows = []
    for k in range(TOP_K):
        start = jnp.sum(jnp.where(exp_ids == e_ref[k:k + 1, :], ps, 0), axis=0, keepdims=True)
        rows.append((start + rank_ref[k:k + 1, :]) * ROW_TILE)
    pos_ref[...] = jnp.concatenate(rows, axis=0)


def _pos_call(e_idx, rank, pstarts):
    n_tok = e_idx.shape[1]
    tp = POS_TILE
    k_spec = pl.BlockSpec((TOP_K, tp), lambda i: (0, i))
    return pl.pallas_call(
        _pos_kernel,
        out_shape=jax.ShapeDtypeStruct((TOP_K, n_tok), I32),
        grid=(n_tok // tp,),
        in_specs=[k_spec, k_spec, pl.BlockSpec((N_EXPERTS, 1), lambda i: (0, 0))],
        out_specs=k_spec,
        compiler_params=pltpu.CompilerParams(dimension_semantics=("arbitrary",)),
        name="positions",
    )(e_idx, rank, pstarts)


def _dispatch_kernel(cnt_ref, ps_ref, pos_ref, x_ref, xs_ref, zbuf, sem, zsem):
    i = pl.program_id(0)
    tt = x_ref.shape[0] // ROW_TILE

    def pad_copies(e, fn):
        cnt = cnt_ref[e]
        pad = jnp.bitwise_and(-cnt, MOE_BLOCK - 1)
        off = ps_ref[e] + cnt
        for b in range(int(math.log2(MOE_BLOCK))):
            size = 1 << b
            hit = jnp.bitwise_and(pad, size)

            @pl.when(hit != 0)
            def _(off=off, size=size):
                fn(pltpu.make_async_copy(
                    zbuf.at[pl.ds(0, size * ROW_TILE)],
                    xs_ref.at[pl.ds(pl.multiple_of(off * ROW_TILE, ROW_TILE), size * ROW_TILE)], zsem))
            off = off + hit

    @pl.when(i == 0)
    def _zero_padding_rows():
        zbuf[...] = jnp.zeros_like(zbuf)

        def start(e, carry):
            pad_copies(e, lambda cp: cp.start())
            return carry

        def wait(e, carry):
            pad_copies(e, lambda cp: cp.wait())
            return carry
        lax.fori_loop(0, N_EXPERTS, start, 0)
        lax.fori_loop(0, N_EXPERTS, wait, 0)

    def row_copy(t, k):
        src = x_ref.at[pl.ds(t * ROW_TILE, ROW_TILE)]
        dst = xs_ref.at[pl.ds(pl.multiple_of(pos_ref[k, t], ROW_TILE), ROW_TILE)]
        return pltpu.make_async_copy(src, dst, sem)

    for t in range(tt):
        for k in range(TOP_K):
            row_copy(t, k).start(priority=k % 2)
    for t in range(tt):
        for k in range(TOP_K):
            row_copy(t, k).wait()


def _dispatch_call(counts, pstarts, pos, x1r, n_rows):
    n_tok = pos.shape[1]
    tt = DISPATCH_TILE
    grid_spec = pltpu.PrefetchScalarGridSpec(
        num_scalar_prefetch=2,
        grid=(n_tok // tt,),
        in_specs=[
            pl.BlockSpec((TOP_K, tt), lambda i, c, p: (0, i), memory_space=pltpu.SMEM),
            pl.BlockSpec((tt * ROW_TILE, LANES), lambda i, c, p: (i, 0)),
        ],
        out_specs=pl.BlockSpec(memory_space=pl.ANY),
        scratch_shapes=[
            pltpu.VMEM((MOE_BLOCK // 2 * ROW_TILE, LANES), U32),
            pltpu.SemaphoreType.DMA(()),
            pltpu.SemaphoreType.DMA(()),
        ],
    )
    return pl.pallas_call(
        _dispatch_kernel,
        out_shape=jax.ShapeDtypeStruct((n_rows * ROW_TILE, LANES), U32),
        grid_spec=grid_spec,
        compiler_params=pltpu.CompilerParams(
            dimension_semantics=("arbitrary",), vmem_limit_bytes=VMEM_LIMIT_BYTES),
        name="dispatch",
    )(counts, pstarts, pos, x1r)


def _experts_kernel(ps_ref, nb_ref, xs_ref, wg_ref, wu_ref, wd_ref, y_ref,
                    xbuf, ybuf, wg_b, wu_b, wd_b, sem_in, sem_out):
    e = pl.program_id(0)
    nb = nb_ref[e]
    blk_rows = MOE_BLOCK * ROW_TILE
    g0 = ps_ref[e] // MOE_BLOCK
    n_total = (ps_ref[N_EXPERTS - 1] // MOE_BLOCK) + nb_ref[N_EXPERTS - 1]
    ahead = EXPERT_SLOTS - 1

    def rows_of(g):
        return pl.ds(pl.multiple_of(g * blk_rows, blk_rows), blk_rows)

    def slot_of(g):
        return jnp.bitwise_and(g, EXPERT_SLOTS - 1)

    def in_copy(g):
        s = slot_of(g)
        return pltpu.make_async_copy(xs_ref.at[rows_of(g)], xbuf.at[s], sem_in.at[s])

    def out_copy(g):
        s = slot_of(g)
        return pltpu.make_async_copy(ybuf.at[s], y_ref.at[rows_of(g)], sem_out.at[s])

    @pl.when(e == 0)
    def _prime():
        for u in range(ahead):
            @pl.when(u < n_total)
            def _(u=u):
                in_copy(u).start()

    @pl.when(nb > 0)
    def _run():
        wg_b[...] = wg_ref[0].astype(BF16)
        wu_b[...] = wu_ref[0].astype(BF16)
        wd_b[...] = wd_ref[0].astype(BF16)

        def block(j, carry):
            g = g0 + j
            s = slot_of(g)
            in_copy(g).wait()

            @pl.when(g + ahead < n_total)
            def _prefetch():
                in_copy(g + ahead).start()

            @pl.when(g >= EXPERT_SLOTS)
            def _reclaim():
                out_copy(g - EXPERT_SLOTS).wait()

            xb = _load_row_tiles(xbuf.at[s], MOE_BLOCK).astype(BF16)
            hg = jnp.dot(xb, wg_b[...], preferred_element_type=F32)
            hu = jnp.dot(xb, wu_b[...], preferred_element_type=F32)
            hh = (hg * jax.nn.sigmoid(hg) * hu).astype(BF16)
            _store_row_tiles(ybuf.at[s], jnp.dot(hh, wd_b[...], preferred_element_type=F32))
            out_copy(g).start()
            return carry
        lax.fori_loop(0, nb, block, 0)

    @pl.when(e == N_EXPERTS - 1)
    def _drain():
        for u in range(EXPERT_SLOTS):
            @pl.when(u < n_total)
            def _(u=u):
                out_copy(u).wait()


def _experts_call(pstarts, n_blk, xs, w_gate, w_up, w_down):
    d = D_MODEL

    def wsel(e, ps, nb):
        return (e, 0, 0)
    grid_spec = pltpu.PrefetchScalarGridSpec(
        num_scalar_prefetch=2,
        grid=(N_EXPERTS,),
        in_specs=[
            pl.BlockSpec(memory_space=pl.ANY),
            pl.BlockSpec((1, d, EXPERT_DIM), wsel),
            pl.BlockSpec((1, d, EXPERT_DIM), wsel),
            pl.BlockSpec((1, EXPERT_DIM, d), wsel),
        ],
        out_specs=pl.BlockSpec(memory_space=pl.ANY),
        scratch_shapes=[
            pltpu.VMEM((EXPERT_SLOTS, MOE_BLOCK * ROW_TILE, LANES), U32),
            pltpu.VMEM((EXPERT_SLOTS, MOE_BLOCK * ROW_TILE, LANES), U32),
            pltpu.VMEM((d, EXPERT_DIM), BF16),
            pltpu.VMEM((d, EXPERT_DIM), BF16),
            pltpu.VMEM((EXPERT_DIM, d), BF16),
            pltpu.SemaphoreType.DMA((EXPERT_SLOTS,)),
            pltpu.SemaphoreType.DMA((EXPERT_SLOTS,)),
        ],
    )
    return pl.pallas_call(
        _experts_kernel,
        out_shape=jax.ShapeDtypeStruct(xs.shape, U32),
        grid_spec=grid_spec,
        compiler_params=pltpu.CompilerParams(
            dimension_semantics=("arbitrary",), vmem_limit_bytes=VMEM_LIMIT_BYTES),
        name="experts",
    )(pstarts, n_blk, xs, w_gate, w_up, w_down)


def _combine_kernel(pos_ref, gate_ref, x1_ref, y_ref, sg_ref, su_ref, sd_ref, g_ref, b_ref, out_ref, buf, sem):
    tt = x1_ref.shape[0]
    part = tt // COMBINE_PARTS

    def row_copy(t, k):
        src = y_ref.at[pl.ds(pl.multiple_of(pos_ref[k, t], ROW_TILE), ROW_TILE)]
        dst = buf.at[k, pl.ds(t * ROW_TILE, ROW_TILE)]
        return pltpu.make_async_copy(src, dst, sem.at[t // part])

    def issue(q):
        for t in range(q * part, (q + 1) * part):
            for k in range(TOP_K):
                row_copy(t, k).start(priority=k % 2)

    def wait(q):
        for t in range(q * part, (q + 1) * part):
            for k in range(TOP_K):
                row_copy(t, k).wait()

    def reduce(q, acc):
        lo = q * part
        a = acc[lo:lo + part]
        gates = gate_ref[lo:lo + part, :]
        for k in range(TOP_K):
            rows = buf.at[k, pl.ds(lo * ROW_TILE, part * ROW_TILE)]
            a = a + gates[:, k:k + 1] * _load_row_tiles(rows, part)
        out_ref[lo:lo + part, :] = _layer_norm(a, g_ref[...], b_ref[...])

    issue(0)
    issue(1)
    x1 = x1_ref[...]
    xb = x1.astype(BF16)
    hg = jnp.dot(xb, sg_ref[...], preferred_element_type=F32)
    hu = jnp.dot(xb, su_ref[...], preferred_element_type=F32)
    hh = (hg * jax.nn.sigmoid(hg) * hu).astype(BF16)
    acc = DEEPNORM_ALPHA * x1 + jnp.dot(hh, sd_ref[...], preferred_element_type=F32)
    for q in range(COMBINE_PARTS):
        wait(q)
        if q + 2 < COMBINE_PARTS:
            issue(q + 2)
        reduce(q, acc)


def _combine_call(pos, gates_t, x1, y, sg, su, sd, ln_g, ln_b):
    n_tok, d = x1.shape
    tt = COMBINE_TILE
    full = lambda a: pl.BlockSpec(a.shape, lambda i: (0,) * a.ndim)
    return pl.pallas_call(
        _combine_kernel,
        out_shape=jax.ShapeDtypeStruct((n_tok, d), F32),
        grid=(n_tok // tt,),
        in_specs=[
            pl.BlockSpec((TOP_K, tt), lambda i: (0, i), memory_space=pltpu.SMEM),
            pl.BlockSpec((tt, TOP_K), lambda i: (i, 0)),
            pl.BlockSpec((tt, d), lambda i: (i, 0)),
            pl.BlockSpec(memory_space=pl.ANY),
            full(sg), full(su), full(sd), full(ln_g), full(ln_b),
        ],
        out_specs=pl.BlockSpec((tt, d), lambda i: (i, 0)),
        scratch_shapes=[pltpu.VMEM((TOP_K, tt * ROW_TILE, LANES), U32),
                        pltpu.SemaphoreType.DMA((COMBINE_PARTS,))],
        compiler_params=pltpu.CompilerParams(
            dimension_semantics=("arbitrary",), vmem_limit_bytes=VMEM_LIMIT_BYTES),
        name="combine",
    )(pos, gates_t, x1, y, sg, su, sd, ln_g, ln_b)


def _group_weights(w_in):
    blocks = w_in.reshape(D_MODEL, 12, GROUP_WIDTH)
    scale = HEAD_DIM ** -0.5
    groups = []
    for g in range(N_ATTN_GROUPS):
        groups.append(jnp.concatenate(
            [blocks[:, g] * scale, blocks[:, N_ATTN_GROUPS + g], blocks[:, 2 * N_ATTN_GROUPS + g]], axis=1))
    groups.append(jnp.concatenate([blocks[:, 9], blocks[:, 10], blocks[:, 11]], axis=1))
    return jnp.stack(groups).astype(BF16)


def kernel(x, w_in, conv_w, w_o, ln1_g, ln1_b, rel_bias, w_router, router_bias, exp_w_gate, exp_w_up,
           exp_w_down, sh_w_gate, sh_w_up, sh_w_down, ln2_g, ln2_b):
    batch, seq, d = x.shape
    assert d == D_MODEL and seq == DILATED_BRANCHES[-1][0] and w_in.shape[0] == DEPTH
    n_tok = batch * seq
    assert n_tok % POST_TILE == 0 and n_tok % DISPATCH_TILE == 0 and n_tok % COMBINE_TILE == 0

    mix = _mix_call(x, _group_weights(w_in[0]), _bias_table(rel_bias), conv_w[0])

    wr = w_router[0]
    wr_hi = wr.astype(BF16)
    wr_lo = (wr - wr_hi.astype(F32)).astype(BF16)
    x2d = x.reshape(n_tok, d)
    x1, x1p, e_idx, rank, gates, counts = _post_call(
        mix.reshape(n_tok, d), x2d, w_o[0].astype(BF16), ln1_g, ln1_b,
        wr_hi.T, wr_lo.T, router_bias[0].reshape(N_EXPERTS, 1))

    counts = counts[:, 0]
    padded = (counts + MOE_BLOCK - 1) // MOE_BLOCK * MOE_BLOCK
    pends = jnp.cumsum(padded)
    pstarts = (pends - padded).astype(I32)
    n_blocks = n_tok * TOP_K // MOE_BLOCK + N_EXPERTS
    n_blk = (padded // MOE_BLOCK).astype(I32)
    pos = _pos_call(e_idx, rank, pstarts.reshape(N_EXPERTS, 1))

    xs = _dispatch_call(counts, pstarts, pos, x1p, n_blocks * MOE_BLOCK)
    y = _experts_call(pstarts, n_blk, xs, exp_w_gate[0], exp_w_up[0], exp_w_down[0])

    out = _combine_call(pos, gates.T, x1, y, sh_w_gate[0].astype(BF16), sh_w_up[0].astype(BF16),
                        sh_w_down[0].astype(BF16), ln2_g, ln2_b)
    return out.reshape(batch, seq, d)
```

```python
import functools
import math

import numpy as np
import jax
import jax.numpy as jnp
from jax import lax
from jax.experimental import pallas as pl
from jax.experimental.pallas import tpu as pltpu
from jax.experimental.pallas import tpu_sc as plsc

F32 = jnp.float32
BF16 = jnp.bfloat16
I32 = jnp.int32
U32 = jnp.uint32

D_MODEL = 1024
HEAD_DIM = 64
N_ATTN_HEADS = 12
ATTN_WIDTH = N_ATTN_HEADS * HEAD_DIM
CONV_WIDTH = D_MODEL - ATTN_WIDTH
CONV_K = 3
DILATED_BRANCHES = ((128, 1), (512, 4), (2048, 16))
ATTN_BLOCK = 128
N_REL_BUCKETS = 32
REL_MAX_DISTANCE = 2048
N_EXPERTS = 256
TOP_K = 8
N_EXPERT_GROUPS = 8
TOPK_GROUPS = 4
GROUP_SIZE = N_EXPERTS // N_EXPERT_GROUPS
EXPERT_DIM = 256
SHARED_DIM = 256
ROUTED_SCALE = 2.5
DEPTH = 1
DEEPNORM_ALPHA = (2.0 * DEPTH) ** 0.25
LN_EPS = 1e-5

LANES = 128
GROUP_WIDTH = 2 * LANES
HEADS_PER_GROUP = GROUP_WIDTH // HEAD_DIM
N_GROUPS = D_MODEL // GROUP_WIDTH
N_ATTN_GROUPS = ATTN_WIDTH // GROUP_WIDTH
VMEM_LIMIT_BYTES = 56 * 1024 * 1024

MASK_VALUE = -1e30

ROW_CHUNK = 256
ATTN_UNROLL = 8
CONV_PAD = 8
ROW_TILE = 4
POS_TILE = 512
POST_TILE = 512
MOE_BLOCK = 512
EXPERT_SLOTS = 4
DISPATCH_TILE = 256
COMBINE_TILE = 256
SC_GATHER_WINDOW = 128
COMBINE_CHUNKS = 4


def _t5_bucket(dist):
    max_exact = N_REL_BUCKETS // 2
    dist = np.asarray(dist)
    log_part = np.log(np.maximum(dist, 1) / max_exact) / math.log(REL_MAX_DISTANCE / max_exact)
    large = max_exact + (log_part * (N_REL_BUCKETS - max_exact)).astype(np.int32)
    large = np.minimum(large, N_REL_BUCKETS - 1)
    return np.where(dist < max_exact, dist, large).astype(np.int32)


def _branch_bucket_and_band(window, dilation):
    w_sub = window // dilation
    qi = np.arange(ATTN_BLOCK)[:, None]
    ki = np.arange(2 * ATTN_BLOCK)[None, :]
    steps = qi + ATTN_BLOCK - ki
    band = (steps >= 0) & (steps <= w_sub)
    bucket = _t5_bucket(np.clip(steps, 0, w_sub) * dilation)
    return bucket, band


def _bias_table(rel_bias):
    tabs = []
    for window, dilation in DILATED_BRANCHES:
        bucket, band = _branch_bucket_and_band(window, dilation)
        onehot = (bucket[:, :, None] == np.arange(N_REL_BUCKETS)).astype(np.float32)
        b = jnp.einsum("qkb,bh->qkh", onehot, rel_bias.astype(F32), precision=lax.Precision.HIGHEST)
        b = jnp.where(band[:, :, None], b, MASK_VALUE)
        tabs.append(b.transpose(2, 0, 1))
    t = jnp.stack(tabs, axis=1)
    t = t.reshape(N_ATTN_GROUPS, HEADS_PER_GROUP, len(DILATED_BRANCHES), ATTN_BLOCK, 2 * ATTN_BLOCK)
    t = t.transpose(0, 2, 1, 3, 4)
    return t.reshape(N_ATTN_GROUPS, len(DILATED_BRANCHES), HEADS_PER_GROUP * ATTN_BLOCK, 2 * ATTN_BLOCK)


def _lane_head():
    return lax.shift_right_logical(lax.broadcasted_iota(I32, (ATTN_BLOCK, GROUP_WIDTH), 1), 6)


def _attn_steps(its, bi, dilation, first, q_s, k_s, v_s, bias_ref, o_s, m_s, l_s):
    logd = int(math.log2(dilation))

    def rows(st):
        return pl.ds(st, ATTN_BLOCK) if dilation == 1 else pl.ds(st, ATTN_BLOCK, stride=dilation)

    def ld(ref, st):
        return jnp.concatenate([ref[0, rows(st), :], ref[1, rows(st), :]], axis=1)

    lane_head = _lane_head()
    bias = bias_ref[0, bi, :, ATTN_BLOCK:] if first else bias_ref[0, bi]
    starts, operands = [], []
    for it in its:
        r = jnp.bitwise_and(it, dilation - 1)
        n = lax.shift_right_logical(it, logd)
        start = r + (dilation * ATTN_BLOCK) * n
        qf = ld(q_s, start)
        if first:
            kk = ld(k_s, start).astype(BF16)
            vv = ld(v_s, start).astype(BF16)
        else:
            prev = start - dilation * ATTN_BLOCK
            kk = jnp.concatenate([ld(k_s, prev), ld(k_s, start)], axis=0).astype(BF16)
            vv = jnp.concatenate([ld(v_s, prev), ld(v_s, start)], axis=0).astype(BF16)
        starts.append(start)
        operands.append((qf, kk, vv))

    results = []
    for qf, kk, vv in operands:
        q4 = jnp.concatenate(
            [jnp.where(lane_head == h, qf, 0.0) for h in range(HEADS_PER_GROUP)], axis=0).astype(BF16)
        s = lax.dot_general(q4, kk, (((1,), (1,)), ((), ())), preferred_element_type=F32) + bias
        m = jnp.max(s, axis=-1, keepdims=True)
        p = jnp.exp(s - m)
        l = jnp.sum(p, axis=-1, keepdims=True)
        pv = jnp.dot(p.astype(BF16), vv, preferred_element_type=F32)
        o = jnp.zeros((ATTN_BLOCK, GROUP_WIDTH), F32)
        mb = jnp.zeros((ATTN_BLOCK, GROUP_WIDTH), F32)
        lb = jnp.zeros((ATTN_BLOCK, GROUP_WIDTH), F32)
        for h in range(HEADS_PER_GROUP):
            sel = lane_head == h
            hr = slice(h * ATTN_BLOCK, (h + 1) * ATTN_BLOCK)
            o = jnp.where(sel, pv[hr], o)
            mb = jnp.where(sel, m[hr], mb)
            lb = jnp.where(sel, l[hr], lb)
        results.append((o, mb, lb))

    for start, (o, mb, lb) in zip(starts, results):
        for sl in range(2):
            lanes = slice(sl * LANES, (sl + 1) * LANES)
            o_s[bi, sl, rows(start), :] = o[:, lanes]
            m_s[bi, sl, rows(start), :] = mb[:, lanes]
            l_s[bi, sl, rows(start), :] = lb[:, lanes]


def _attn_range(lo, hi, **kw):
    n_groups = (hi - lo) // ATTN_UNROLL

    def body(j, carry):
        base = lo + j * ATTN_UNROLL
        _attn_steps([base + u for u in range(ATTN_UNROLL)], **kw)
        return carry
    if n_groups > 0:
        lax.fori_loop(0, n_groups, body, 0)
    tail = list(range(lo + n_groups * ATTN_UNROLL, hi))
    if tail:
        _attn_steps([jnp.int32(t) for t in tail], **kw)


def _mix_kernel(x_ref, w_ref, bias_ref, cw_ref, out_ref, q_s, k_s, v_s, o_s, m_s, l_s):
    g = pl.program_id(1)
    seq = x_ref.shape[1]
    n_chunks = seq // ROW_CHUNK

    def proj(c, carry):
        rows = pl.ds(pl.multiple_of(c * ROW_CHUNK, ROW_CHUNK), ROW_CHUNK)
        res = jnp.dot(x_ref[0, rows, :].astype(BF16), w_ref[0], preferred_element_type=F32)
        for j, dst in enumerate((q_s, k_s, v_s)):
            for sl in range(2):
                lo = j * GROUP_WIDTH + sl * LANES
                dst[sl, rows, :] = res[:, lo:lo + LANES]
        return carry
    lax.fori_loop(0, n_chunks, proj, 0)

    @pl.when(g < N_ATTN_GROUPS)
    def _attention():
        for bi, (window, dilation) in enumerate(DILATED_BRANCHES):
            n_blocks = seq // dilation // ATTN_BLOCK
            n_steps = n_blocks * dilation
            kw = dict(bi=bi, dilation=dilation, q_s=q_s, k_s=k_s, v_s=v_s,
                      bias_ref=bias_ref, o_s=o_s, m_s=m_s, l_s=l_s)
            _attn_range(0, dilation, first=True, **kw)
            _attn_range(dilation, n_steps, first=False, **kw)

        def combine(c, carry):
            rows = pl.ds(pl.multiple_of(c * ROW_CHUNK, ROW_CHUNK), ROW_CHUNK)
            for sl in range(2):
                ms = [m_s[bi, sl, rows, :] for bi in range(len(DILATED_BRANCHES))]
                mx = jnp.maximum(jnp.maximum(ms[0], ms[1]), ms[2])
                es = [jnp.exp(v - mx) for v in ms]
                den = es[0] * l_s[0, sl, rows, :] + es[1] * l_s[1, sl, rows, :] + es[2] * l_s[2, sl, rows, :]
                num = es[0] * o_s[0, sl, rows, :] + es[1] * o_s[1, sl, rows, :] + es[2] * o_s[2, sl, rows, :]
                out_ref[0, rows, sl * LANES:(sl + 1) * LANES] = (num / den).astype(BF16)
            return carry
        lax.fori_loop(0, n_chunks, combine, 0)

    @pl.when(g == N_ATTN_GROUPS)
    def _short_conv():
        pad = CONV_PAD
        u_s = o_s.at[0]
        for sl in range(2):
            u_s[sl, 0:pad, :] = jnp.zeros((pad, LANES), F32)
            for c in range(n_chunks):
                lo = c * ROW_CHUNK
                u_s[sl, pad + lo:pad + lo + ROW_CHUNK, :] = v_s[sl, lo:lo + ROW_CHUNK, :] * q_s[sl, lo:lo + ROW_CHUNK, :]
            w = [cw_ref[kk:kk + 1, sl * LANES:(sl + 1) * LANES] for kk in range(CONV_K)]
            for c in range(n_chunks):
                lo = c * ROW_CHUNK
                y = w[2] * u_s[sl, pad + lo:pad + lo + ROW_CHUNK, :]
                y = y + w[1] * u_s[sl, pad + lo - 1:pad + lo - 1 + ROW_CHUNK, :]
                y = y + w[0] * u_s[sl, pad + lo - 2:pad + lo - 2 + ROW_CHUNK, :]
                out_ref[0, lo:lo + ROW_CHUNK, sl * LANES:(sl + 1) * LANES] = (
                    k_s[sl, lo:lo + ROW_CHUNK, :] * y).astype(BF16)


def _mix_call(x, w_groups, bias_tbl, conv_w):
    batch, seq, d = x.shape
    return pl.pallas_call(
        _mix_kernel,
        out_shape=jax.ShapeDtypeStruct((batch, seq, d), BF16),
        grid=(batch, N_GROUPS),
        in_specs=[
            pl.BlockSpec((1, seq, d), lambda b, g: (b, 0, 0)),
            pl.BlockSpec((1, d, 3 * GROUP_WIDTH), lambda b, g: (g, 0, 0)),
            pl.BlockSpec((1,) + bias_tbl.shape[1:], lambda b, g: (jnp.minimum(g, N_ATTN_GROUPS - 1), 0, 0, 0)),
            pl.BlockSpec(conv_w.shape, lambda b, g: (0, 0)),
        ],
        out_specs=pl.BlockSpec((1, seq, GROUP_WIDTH), lambda b, g: (b, 0, g)),
        scratch_shapes=[
            pltpu.VMEM((2, seq, LANES), F32),
            pltpu.VMEM((2, seq, LANES), F32),
            pltpu.VMEM((2, seq, LANES), F32),
            pltpu.VMEM((len(DILATED_BRANCHES), 2, seq + CONV_PAD, LANES), F32),
            pltpu.VMEM((len(DILATED_BRANCHES), 2, seq, LANES), F32),
            pltpu.VMEM((len(DILATED_BRANCHES), 2, seq, LANES), F32),
        ],
        compiler_params=pltpu.CompilerParams(
            dimension_semantics=("arbitrary", "arbitrary"), vmem_limit_bytes=VMEM_LIMIT_BYTES),
        name="mix",
    )(x, w_groups, bias_tbl, conv_w)


def _layer_norm(h, g, b):
    mu = jnp.mean(h, axis=-1, keepdims=True)
    c = h - mu
    var = jnp.mean(c * c, axis=-1, keepdims=True)
    return c * lax.rsqrt(var + LN_EPS) * g + b


def _bf16_bits(v):
    return lax.bitcast_convert_type(v.astype(BF16).astype(F32), U32)


def _store_row_tiles(ref, val):
    n = val.shape[0]
    for j in range(ROW_TILE):
        lo = _bf16_bits(val[:, (2 * j) * LANES:(2 * j + 1) * LANES])
        hi = _bf16_bits(val[:, (2 * j + 1) * LANES:(2 * j + 2) * LANES])
        ref[pl.ds(j, n, stride=ROW_TILE), :] = jnp.bitwise_or(lax.shift_right_logical(lo, jnp.uint32(16)), hi)


def _load_row_tiles(ref, n):
    pieces = []
    for j in range(ROW_TILE):
        w = ref[pl.ds(j, n, stride=ROW_TILE), :]
        pieces.append(lax.bitcast_convert_type(lax.shift_left(w, jnp.uint32(16)), F32))
        pieces.append(lax.bitcast_convert_type(jnp.bitwise_and(w, jnp.uint32(0xFFFF0000)), F32))
    return jnp.concatenate(pieces, axis=1)


def _first_argmax_rows(v, row_ids, n_rows):
    m = jnp.max(v, axis=0, keepdims=True)
    idx = jnp.min(jnp.where(v == m, row_ids, n_rows), axis=0, keepdims=True)
    return m, idx


def _post_kernel(mix_ref, x_ref, wo_ref, g_ref, b_ref, wrh_ref, wrl_ref, rb_ref,
                 x1_ref, x1p_ref, e_ref, rank_ref, gate_ref, cnt_ref, carry_s):
    i = pl.program_id(0)
    tm = x_ref.shape[0]

    @pl.when(i == 0)
    def _init():
        carry_s[...] = jnp.zeros_like(carry_s)

    h = DEEPNORM_ALPHA * x_ref[...] + jnp.dot(mix_ref[...], wo_ref[...], preferred_element_type=F32)
    x1 = _layer_norm(h, g_ref[...], b_ref[...])
    x1_ref[...] = x1
    _store_row_tiles(x1p_ref, x1)

    x_hi = x1.astype(BF16)
    x_lo = (x1 - x_hi.astype(F32)).astype(BF16)
    dn = (((1,), (1,)), ((), ()))
    logits = lax.dot_general(wrh_ref[...], x_hi, dn, preferred_element_type=F32)
    logits = logits + lax.dot_general(wrh_ref[...], x_lo, dn, preferred_element_type=F32)
    logits = logits + lax.dot_general(wrl_ref[...], x_hi, dn, preferred_element_type=F32)
    scores = jax.nn.sigmoid(logits)
    biased = scores + rb_ref[...]

    neg_inf = -jnp.inf
    sub_ids = lax.broadcasted_iota(I32, (GROUP_SIZE, tm), 0)
    gs_rows = []
    for gi in range(N_EXPERT_GROUPS):
        bg = biased[gi * GROUP_SIZE:(gi + 1) * GROUP_SIZE]
        m1, i1 = _first_argmax_rows(bg, sub_ids, GROUP_SIZE)
        m2 = jnp.max(jnp.where(sub_ids == i1, neg_inf, bg), axis=0, keepdims=True)
        gs_rows.append(m1 + m2)
    gscore = jnp.concatenate(gs_rows, axis=0)
    grp_ids = lax.broadcasted_iota(I32, (N_EXPERT_GROUPS, tm), 0)
    keep_g = jnp.zeros((N_EXPERT_GROUPS, tm), jnp.bool_)
    for _ in range(TOPK_GROUPS):
        _, gi1 = _first_argmax_rows(gscore, grp_ids, N_EXPERT_GROUPS)
        hit = grp_ids == gi1
        keep_g = jnp.logical_or(keep_g, hit)
        gscore = jnp.where(hit, neg_inf, gscore)
    keep_f = keep_g.astype(F32)
    keep_e = jnp.concatenate(
        [jnp.broadcast_to(keep_f[gi:gi + 1], (GROUP_SIZE, tm)) for gi in range(N_EXPERT_GROUPS)], axis=0)
    masked = jnp.where(keep_e > 0.5, biased, neg_inf)

    exp_ids = lax.broadcasted_iota(I32, (N_EXPERTS, tm), 0)
    sel = jnp.zeros((N_EXPERTS, tm), F32)
    e_rows, g_rows = [], []
    for _ in range(TOP_K):
        _, e1 = _first_argmax_rows(masked, exp_ids, N_EXPERTS)
        hit = exp_ids == e1
        sel = jnp.where(hit, 1.0, sel)
        g_rows.append(jnp.sum(jnp.where(hit, scores, 0.0), axis=0, keepdims=True))
        e_rows.append(e1)
        masked = jnp.where(hit, neg_inf, masked)
    gates = jnp.concatenate(g_rows, axis=0)
    gates = gates / jnp.sum(gates, axis=0, keepdims=True) * ROUTED_SCALE
    gate_ref[...] = gates
    e_ref[...] = jnp.concatenate(e_rows, axis=0)

    t_row = lax.broadcasted_iota(I32, (tm, tm), 0)
    t_col = lax.broadcasted_iota(I32, (tm, tm), 1)
    upper = (t_row < t_col).astype(BF16)
    excl = jnp.dot(sel.astype(BF16), upper, preferred_element_type=F32)
    base = carry_s[...] + excl
    r_rows = [jnp.sum(jnp.where(exp_ids == e1, base, 0.0), axis=0, keepdims=True) for e1 in e_rows]
    rank_ref[...] = jnp.concatenate(r_rows, axis=0).astype(I32)
    carry = carry_s[...] + jnp.sum(sel, axis=1, keepdims=True)
    carry_s[...] = carry
    cnt_ref[...] = carry.astype(I32)


def _post_call(mix2d, x2d, wo, ln_g, ln_b, wr_hi, wr_lo, rb):
    n_tok, d = x2d.shape
    tm = POST_TILE
    tok_spec = pl.BlockSpec((tm, d), lambda i: (i, 0))
    k_spec = pl.BlockSpec((TOP_K, tm), lambda i: (0, i))
    full = lambda a: pl.BlockSpec(a.shape, lambda i: (0,) * a.ndim)
    return pl.pallas_call(
        _post_kernel,
        out_shape=(
            jax.ShapeDtypeStruct((n_tok, d), F32),
            jax.ShapeDtypeStruct((n_tok * ROW_TILE, LANES), U32),
            jax.ShapeDtypeStruct((TOP_K, n_tok), I32),
            jax.ShapeDtypeStruct((TOP_K, n_tok), I32),
            jax.ShapeDtypeStruct((TOP_K, n_tok), F32),
            jax.ShapeDtypeStruct((N_EXPERTS, 1), I32),
        ),
        grid=(n_tok // tm,),
        in_specs=[tok_spec, tok_spec, full(wo), full(ln_g), full(ln_b), full(wr_hi), full(wr_lo), full(rb)],
        out_specs=(tok_spec, pl.BlockSpec((tm * ROW_TILE, LANES), lambda i: (i, 0)), k_spec, k_spec, k_spec,
                   pl.BlockSpec((N_EXPERTS, 1), lambda i: (0, 0))),
        scratch_shapes=[pltpu.VMEM((N_EXPERTS, 1), F32)],
        compiler_params=pltpu.CompilerParams(
            dimension_semantics=("arbitrary",), vmem_limit_bytes=VMEM_LIMIT_BYTES),
        name="post",
    )(mix2d, x2d, wo, ln_g, ln_b, wr_hi, wr_lo, rb)


def _pos_kernel(e_ref, rank_ref, ps_ref, pos_ref, rows_ref):
    tp = e_ref.shape[1]
    exp_ids = lax.broadcasted_iota(I32, (N_EXPERTS, tp), 0)
    ps = ps_ref[...]
    rows = []
    for k in range(TOP_K):
        start = jnp.sum(jnp.where(exp_ids == e_ref[k:k + 1, :], ps, 0), axis=0, keepdims=True)
        rows.append((start + rank_ref[k:k + 1, :]) * ROW_TILE)
    pos = jnp.concatenate(rows, axis=0)
    pos_ref[...] = pos
    src = lax.broadcasted_iota(I32, (tp, tp * ROW_TILE), 0)
    dst = lax.broadcasted_iota(I32, (tp, tp * ROW_TILE), 1)
    spread = (lax.shift_right_logical(dst, int(math.log2(ROW_TILE))) == src).astype(BF16)
    out = lax.broadcasted_iota(I32, (TOP_K, tp * ROW_TILE), 1) & (ROW_TILE - 1)
    for byte in range(3):
        part = jnp.bitwise_and(lax.shift_right_logical(pos, 8 * byte), 255).astype(F32).astype(BF16)
        wide = jnp.dot(part, spread, preferred_element_type=F32).astype(I32)
        out = out + lax.shift_left(wide, 8 * byte)
    rows_ref[...] = out


def _pos_call(e_idx, rank, pstarts):
    n_tok = e_idx.shape[1]
    tp = POS_TILE
    k_spec = pl.BlockSpec((TOP_K, tp), lambda i: (0, i))
    return pl.pallas_call(
        _pos_kernel,
        out_shape=(jax.ShapeDtypeStruct((TOP_K, n_tok), I32),
                   jax.ShapeDtypeStruct((TOP_K, n_tok * ROW_TILE), I32)),
        grid=(n_tok // tp,),
        in_specs=[k_spec, k_spec, pl.BlockSpec((N_EXPERTS, 1), lambda i: (0, 0))],
        out_specs=(k_spec, pl.BlockSpec((TOP_K, tp * ROW_TILE), lambda i: (0, i))),
        compiler_params=pltpu.CompilerParams(dimension_semantics=("arbitrary",)),
        name="positions",
    )(e_idx, rank, pstarts)


def _dispatch_kernel(cnt_ref, ps_ref, pos_ref, x_ref, xs_ref, zbuf, sem, zsem):
    i = pl.program_id(0)
    tt = x_ref.shape[0] // ROW_TILE

    def pad_copies(e, fn):
        cnt = cnt_ref[e]
        pad = jnp.bitwise_and(-cnt, MOE_BLOCK - 1)
        off = ps_ref[e] + cnt
        for b in range(int(math.log2(MOE_BLOCK))):
            size = 1 << b
            hit = jnp.bitwise_and(pad, size)

            @pl.when(hit != 0)
            def _(off=off, size=size):
                fn(pltpu.make_async_copy(
                    zbuf.at[pl.ds(0, size * ROW_TILE)],
                    xs_ref.at[pl.ds(pl.multiple_of(off * ROW_TILE, ROW_TILE), size * ROW_TILE)], zsem))
            off = off + hit

    @pl.when(i == 0)
    def _zero_padding_rows():
        zbuf[...] = jnp.zeros_like(zbuf)

        def start(e, carry):
            pad_copies(e, lambda cp: cp.start())
            return carry

        def wait(e, carry):
            pad_copies(e, lambda cp: cp.wait())
            return carry
        lax.fori_loop(0, N_EXPERTS, start, 0)
        lax.fori_loop(0, N_EXPERTS, wait, 0)

    def row_copy(t, k):
        src = x_ref.at[pl.ds(t * ROW_TILE, ROW_TILE)]
        dst = xs_ref.at[pl.ds(pl.multiple_of(pos_ref[k, t], ROW_TILE), ROW_TILE)]
        return pltpu.make_async_copy(src, dst, sem)

    for t in range(tt):
        for k in range(TOP_K):
            row_copy(t, k).start(priority=k % 2)
    for t in range(tt):
        for k in range(TOP_K):
            row_copy(t, k).wait()


def _dispatch_call(counts, pstarts, pos, x1r, n_rows):
    n_tok = pos.shape[1]
    tt = DISPATCH_TILE
    grid_spec = pltpu.PrefetchScalarGridSpec(
        num_scalar_prefetch=2,
        grid=(n_tok // tt,),
        in_specs=[
            pl.BlockSpec((TOP_K, tt), lambda i, c, p: (0, i), memory_space=pltpu.SMEM),
            pl.BlockSpec((tt * ROW_TILE, LANES), lambda i, c, p: (i, 0)),
        ],
        out_specs=pl.BlockSpec(memory_space=pl.ANY),
        scratch_shapes=[
            pltpu.VMEM((MOE_BLOCK // 2 * ROW_TILE, LANES), U32),
            pltpu.SemaphoreType.DMA(()),
            pltpu.SemaphoreType.DMA(()),
        ],
    )
    return pl.pallas_call(
        _dispatch_kernel,
        out_shape=jax.ShapeDtypeStruct((n_rows * ROW_TILE, LANES), U32),
        grid_spec=grid_spec,
        compiler_params=pltpu.CompilerParams(
            dimension_semantics=("arbitrary",), vmem_limit_bytes=VMEM_LIMIT_BYTES),
        name="dispatch",
    )(counts, pstarts, pos, x1r)


def _experts_kernel(ps_ref, nb_ref, xs_ref, wg_ref, wu_ref, wd_ref, y_ref,
                    xbuf, ybuf, wg_b, wu_b, wd_b, sem_in, sem_out):
    e = pl.program_id(0)
    nb = nb_ref[e]
    blk_rows = MOE_BLOCK * ROW_TILE
    g0 = ps_ref[e] // MOE_BLOCK
    n_total = (ps_ref[N_EXPERTS - 1] // MOE_BLOCK) + nb_ref[N_EXPERTS - 1]
    ahead = EXPERT_SLOTS - 1

    def rows_of(g):
        return pl.ds(pl.multiple_of(g * blk_rows, blk_rows), blk_rows)

    def slot_of(g):
        return jnp.bitwise_and(g, EXPERT_SLOTS - 1)

    def in_copy(g):
        s = slot_of(g)
        return pltpu.make_async_copy(xs_ref.at[rows_of(g)], xbuf.at[s], sem_in.at[s])

    def out_copy(g):
        s = slot_of(g)
        return pltpu.make_async_copy(ybuf.at[s], y_ref.at[rows_of(g)], sem_out.at[s])

    @pl.when(e == 0)
    def _prime():
        for u in range(ahead):
            @pl.when(u < n_total)
            def _(u=u):
                in_copy(u).start()

    @pl.when(nb > 0)
    def _run():
        wg_b[...] = wg_ref[0].astype(BF16)
        wu_b[...] = wu_ref[0].astype(BF16)
        wd_b[...] = wd_ref[0].astype(BF16)

        def block(j, carry):
            g = g0 + j
            s = slot_of(g)
            in_copy(g).wait()

            @pl.when(g + ahead < n_total)
            def _prefetch():
                in_copy(g + ahead).start()

            @pl.when(g >= EXPERT_SLOTS)
            def _reclaim():
                out_copy(g - EXPERT_SLOTS).wait()

            xb = _load_row_tiles(xbuf.at[s], MOE_BLOCK).astype(BF16)
            hg = jnp.dot(xb, wg_b[...], preferred_element_type=F32)
            hu = jnp.dot(xb, wu_b[...], preferred_element_type=F32)
            hh = (hg * jax.nn.sigmoid(hg) * hu).astype(BF16)
            _store_row_tiles(ybuf.at[s], jnp.dot(hh, wd_b[...], preferred_element_type=F32))
            out_copy(g).start()
            return carry
        lax.fori_loop(0, nb, block, 0)

    @pl.when(e == N_EXPERTS - 1)
    def _drain():
        for u in range(EXPERT_SLOTS):
            @pl.when(u < n_total)
            def _(u=u):
                out_copy(u).wait()


def _experts_call(pstarts, n_blk, xs, w_gate, w_up, w_down):
    d = D_MODEL

    def wsel(e, ps, nb):
        return (e, 0, 0)
    grid_spec = pltpu.PrefetchScalarGridSpec(
        num_scalar_prefetch=2,
        grid=(N_EXPERTS,),
        in_specs=[
            pl.BlockSpec(memory_space=pl.ANY),
            pl.BlockSpec((1, d, EXPERT_DIM), wsel),
            pl.BlockSpec((1, d, EXPERT_DIM), wsel),
            pl.BlockSpec((1, EXPERT_DIM, d), wsel),
        ],
        out_specs=pl.BlockSpec(memory_space=pl.ANY),
        scratch_shapes=[
            pltpu.VMEM((EXPERT_SLOTS, MOE_BLOCK * ROW_TILE, LANES), U32),
            pltpu.VMEM((EXPERT_SLOTS, MOE_BLOCK * ROW_TILE, LANES), U32),
            pltpu.VMEM((d, EXPERT_DIM), BF16),
            pltpu.VMEM((d, EXPERT_DIM), BF16),
            pltpu.VMEM((EXPERT_DIM, d), BF16),
            pltpu.SemaphoreType.DMA((EXPERT_SLOTS,)),
            pltpu.SemaphoreType.DMA((EXPERT_SLOTS,)),
        ],
    )
    return pl.pallas_call(
        _experts_kernel,
        out_shape=jax.ShapeDtypeStruct(xs.shape, U32),
        grid_spec=grid_spec,
        compiler_params=pltpu.CompilerParams(
            dimension_semantics=("arbitrary",), vmem_limit_bytes=VMEM_LIMIT_BYTES),
        name="experts",
    )(pstarts, n_blk, xs, w_gate, w_up, w_down)


def _sc_row_gather(table, row_idx, chunk, n_chunks):
    n_k = row_idx.shape[0]
    windows = row_idx.shape[1] // n_chunks // SC_GATHER_WINDOW
    mesh = plsc.VectorSubcoreMesh(core_axis_name="c", subcore_axis_name="s")

    @functools.partial(
        pl.kernel, out_type=jax.ShapeDtypeStruct((n_k * windows * SC_GATHER_WINDOW, LANES), table.dtype),
        mesh=mesh, compiler_params=pltpu.CompilerParams(use_tc_tiling_on_sc=True), name="sc_row_gather")
    def gather(table_hbm, idx_hbm, out_hbm):
        def body(idx_vmem, out_vmem):
            pltpu.sync_copy(table_hbm.at[idx_vmem.at[0]], out_vmem)
        pltpu.emit_pipeline(
            body, grid=(n_k, windows),
            in_specs=[pl.BlockSpec((1, SC_GATHER_WINDOW), lambda k, i: (k, chunk * windows + i))],
            out_specs=[pl.BlockSpec((SC_GATHER_WINDOW, LANES), lambda k, i: (k * windows + i, 0))],
            core_axis_name=("c", "s"), dimension_semantics=(pltpu.PARALLEL, pltpu.PARALLEL),
        )(idx_hbm, out_hbm)
    return gather(table, row_idx)


def _combine_kernel(gate_ref, x1_ref, *refs):
    yk_refs = refs[:TOP_K]
    sg_ref, su_ref, sd_ref, g_ref, b_ref = refs[TOP_K:TOP_K + 5]
    out_ref = refs[-1]
    tt = x1_ref.shape[0]
    x1 = x1_ref[...]
    xb = x1.astype(BF16)
    hg = jnp.dot(xb, sg_ref[...], preferred_element_type=F32)
    hu = jnp.dot(xb, su_ref[...], preferred_element_type=F32)
    hh = (hg * jax.nn.sigmoid(hg) * hu).astype(BF16)
    acc = DEEPNORM_ALPHA * x1 + jnp.dot(hh, sd_ref[...], preferred_element_type=F32)
    gates = gate_ref[...]
    for k in range(TOP_K):
        acc = acc + gates[:, k:k + 1] * _load_row_tiles(yk_refs[k], tt)
    out_ref[...] = _layer_norm(acc, g_ref[...], b_ref[...])


def _combine_call(chunk, n_chunks, gates_t, x1, y_chunk, sg, su, sd, ln_g, ln_b, carry):
    n_tok, d = x1.shape
    tt = COMBINE_TILE
    tiles = n_tok // n_chunks // tt
    first = chunk * tiles
    full = lambda a: pl.BlockSpec(a.shape, lambda i: (0,) * a.ndim)
    y_specs = [pl.BlockSpec((tt * ROW_TILE, LANES), lambda i, k=k: (k * tiles + i, 0)) for k in range(TOP_K)]
    in_specs = [
        pl.BlockSpec((tt, TOP_K), lambda i: (first + i, 0)),
        pl.BlockSpec((tt, d), lambda i: (first + i, 0)),
        *y_specs,
        full(sg), full(su), full(sd), full(ln_g), full(ln_b),
    ]
    operands = [gates_t, x1] + [y_chunk] * TOP_K + [sg, su, sd, ln_g, ln_b]
    aliases = {}
    if carry is not None:
        in_specs.append(pl.BlockSpec(memory_space=pl.ANY))
        operands.append(carry)
        aliases = {len(operands) - 1: 0}
    return pl.pallas_call(
        _combine_kernel,
        out_shape=jax.ShapeDtypeStruct((n_tok, d), F32),
        grid=(tiles,),
        in_specs=in_specs,
        out_specs=pl.BlockSpec((tt, d), lambda i: (first + i, 0)),
        input_output_aliases=aliases,
        compiler_params=pltpu.CompilerParams(
            dimension_semantics=("arbitrary",), vmem_limit_bytes=VMEM_LIMIT_BYTES),
        name="combine",
    )(*operands)


def _group_weights(w_in):
    blocks = w_in.reshape(D_MODEL, 12, GROUP_WIDTH)
    scale = HEAD_DIM ** -0.5
    groups = []
    for g in range(N_ATTN_GROUPS):
        groups.append(jnp.concatenate(
            [blocks[:, g] * scale, blocks[:, N_ATTN_GROUPS + g], blocks[:, 2 * N_ATTN_GROUPS + g]], axis=1))
    groups.append(jnp.concatenate([blocks[:, 9], blocks[:, 10], blocks[:, 11]], axis=1))
    return jnp.stack(groups).astype(BF16)


def kernel(x, w_in, conv_w, w_o, ln1_g, ln1_b, rel_bias, w_router, router_bias, exp_w_gate, exp_w_up,
           exp_w_down, sh_w_gate, sh_w_up, sh_w_down, ln2_g, ln2_b):
    batch, seq, d = x.shape
    assert d == D_MODEL and seq == DILATED_BRANCHES[-1][0] and w_in.shape[0] == DEPTH
    n_tok = batch * seq
    assert n_tok % POST_TILE == 0 and n_tok % DISPATCH_TILE == 0 and n_tok % COMBINE_TILE == 0

    mix = _mix_call(x, _group_weights(w_in[0]), _bias_table(rel_bias), conv_w[0])

    wr = w_router[0]
    wr_hi = wr.astype(BF16)
    wr_lo = (wr - wr_hi.astype(F32)).astype(BF16)
    x2d = x.reshape(n_tok, d)
    x1, x1p, e_idx, rank, gates, counts = _post_call(
        mix.reshape(n_tok, d), x2d, w_o[0].astype(BF16), ln1_g, ln1_b,
        wr_hi.T, wr_lo.T, router_bias[0].reshape(N_EXPERTS, 1))

    counts = counts[:, 0]
    padded = (counts + MOE_BLOCK - 1) // MOE_BLOCK * MOE_BLOCK
    pends = jnp.cumsum(padded)
    pstarts = (pends - padded).astype(I32)
    n_blocks = n_tok * TOP_K // MOE_BLOCK + N_EXPERTS
    n_blk = (padded // MOE_BLOCK).astype(I32)
    pos, row_idx = _pos_call(e_idx, rank, pstarts.reshape(N_EXPERTS, 1))

    xs = _dispatch_call(counts, pstarts, pos, x1p, n_blocks * MOE_BLOCK)
    y = _experts_call(pstarts, n_blk, xs, exp_w_gate[0], exp_w_up[0], exp_w_down[0])

    tc = n_tok // COMBINE_CHUNKS
    gates_t = gates.T
    sh = (sh_w_gate[0].astype(BF16), sh_w_up[0].astype(BF16), sh_w_down[0].astype(BF16))
    out = None
    for c in range(COMBINE_CHUNKS):
        y_chunk = _sc_row_gather(y, row_idx, c, COMBINE_CHUNKS)
        out = _combine_call(c, COMBINE_CHUNKS, gates_t, x1, y_chunk, *sh, ln2_g, ln2_b, out)
    return out.reshape(batch, seq, d)
```

```python
import functools
import math

import numpy as np
import jax
import jax.numpy as jnp
from jax import lax
from jax.experimental import pallas as pl
from jax.experimental.pallas import tpu as pltpu
from jax.experimental.pallas import tpu_sc as plsc

F32 = jnp.float32
BF16 = jnp.bfloat16
I32 = jnp.int32
U32 = jnp.uint32

D_MODEL = 1024
HEAD_DIM = 64
N_ATTN_HEADS = 12
ATTN_WIDTH = N_ATTN_HEADS * HEAD_DIM
CONV_WIDTH = D_MODEL - ATTN_WIDTH
CONV_K = 3
DILATED_BRANCHES = ((128, 1), (512, 4), (2048, 16))
ATTN_BLOCK = 128
N_REL_BUCKETS = 32
REL_MAX_DISTANCE = 2048
N_EXPERTS = 256
TOP_K = 8
N_EXPERT_GROUPS = 8
TOPK_GROUPS = 4
GROUP_SIZE = N_EXPERTS // N_EXPERT_GROUPS
EXPERT_DIM = 256
SHARED_DIM = 256
ROUTED_SCALE = 2.5
DEPTH = 1
DEEPNORM_ALPHA = (2.0 * DEPTH) ** 0.25
LN_EPS = 1e-5

LANES = 128
GROUP_WIDTH = 2 * LANES
HEADS_PER_GROUP = GROUP_WIDTH // HEAD_DIM
N_GROUPS = D_MODEL // GROUP_WIDTH
N_ATTN_GROUPS = ATTN_WIDTH // GROUP_WIDTH
VMEM_LIMIT_BYTES = 56 * 1024 * 1024

MASK_VALUE = -1e30

ROW_CHUNK = 256
ATTN_UNROLL = 8
CONV_PAD = 8
ROW_TILE = 4
POS_TILE = 512
POST_TILE = 512
MOE_BLOCK = 512
EXPERT_SLOTS = 4
DISPATCH_TILE = 256
COMBINE_TILE = 256
SC_GATHER_WINDOW = 128
COMBINE_CHUNKS = 4


def _t5_bucket(dist):
    max_exact = N_REL_BUCKETS // 2
    dist = np.asarray(dist)
    log_part = np.log(np.maximum(dist, 1) / max_exact) / math.log(REL_MAX_DISTANCE / max_exact)
    large = max_exact + (log_part * (N_REL_BUCKETS - max_exact)).astype(np.int32)
    large = np.minimum(large, N_REL_BUCKETS - 1)
    return np.where(dist < max_exact, dist, large).astype(np.int32)


def _branch_bucket_and_band(window, dilation):
    w_sub = window // dilation
    qi = np.arange(ATTN_BLOCK)[:, None]
    ki = np.arange(2 * ATTN_BLOCK)[None, :]
    steps = qi + ATTN_BLOCK - ki
    band = (steps >= 0) & (steps <= w_sub)
    bucket = _t5_bucket(np.clip(steps, 0, w_sub) * dilation)
    return bucket, band


def _bias_table(rel_bias):
    tabs = []
    for window, dilation in DILATED_BRANCHES:
        bucket, band = _branch_bucket_and_band(window, dilation)
        onehot = (bucket[:, :, None] == np.arange(N_REL_BUCKETS)).astype(np.float32)
        b = jnp.einsum("qkb,bh->qkh", onehot, rel_bias.astype(F32), precision=lax.Precision.HIGHEST)
        b = jnp.where(band[:, :, None], b, MASK_VALUE)
        tabs.append(b.transpose(2, 0, 1))
    t = jnp.stack(tabs, axis=1)
    t = t.reshape(N_ATTN_GROUPS, HEADS_PER_GROUP, len(DILATED_BRANCHES), ATTN_BLOCK, 2 * ATTN_BLOCK)
    t = t.transpose(0, 2, 1, 3, 4)
    return t.reshape(N_ATTN_GROUPS, len(DILATED_BRANCHES), HEADS_PER_GROUP * ATTN_BLOCK, 2 * ATTN_BLOCK)


def _lane_head():
    return lax.shift_right_logical(lax.broadcasted_iota(I32, (ATTN_BLOCK, GROUP_WIDTH), 1), 6)


def _attn_steps(its, bi, dilation, first, q_s, k_s, v_s, bias_ref, o_s, m_s, l_s):
    logd = int(math.log2(dilation))

    def rows(st):
        return pl.ds(st, ATTN_BLOCK) if dilation == 1 else pl.ds(st, ATTN_BLOCK, stride=dilation)

    def ld(ref, st):
        return jnp.concatenate([ref[0, rows(st), :], ref[1, rows(st), :]], axis=1)

    lane_head = _lane_head()
    bias = bias_ref[0, bi, :, ATTN_BLOCK:] if first else bias_ref[0, bi]
    starts, operands = [], []
    for it in its:
        r = jnp.bitwise_and(it, dilation - 1)
        n = lax.shift_right_logical(it, logd)
        start = r + (dilation * ATTN_BLOCK) * n
        qf = ld(q_s, start)
        if first:
            kk = ld(k_s, start).astype(BF16)
            vv = ld(v_s, start).astype(BF16)
        else:
            prev = start - dilation * ATTN_BLOCK
            kk = jnp.concatenate([ld(k_s, prev), ld(k_s, start)], axis=0).astype(BF16)
            vv = jnp.concatenate([ld(v_s, prev), ld(v_s, start)], axis=0).astype(BF16)
        starts.append(start)
        operands.append((qf, kk, vv))

    results = []
    for qf, kk, vv in operands:
        q4 = jnp.concatenate(
            [jnp.where(lane_head == h, qf, 0.0) for h in range(HEADS_PER_GROUP)], axis=0).astype(BF16)
        s = lax.dot_general(q4, kk, (((1,), (1,)), ((), ())), preferred_element_type=F32) + bias
        m = jnp.max(s, axis=-1, keepdims=True)
        p = jnp.exp(s - m)
        l = jnp.sum(p, axis=-1, keepdims=True)
        pv = jnp.dot(p.astype(BF16), vv, preferred_element_type=F32)
        o = jnp.zeros((ATTN_BLOCK, GROUP_WIDTH), F32)
        mb = jnp.zeros((ATTN_BLOCK, GROUP_WIDTH), F32)
        lb = jnp.zeros((ATTN_BLOCK, GROUP_WIDTH), F32)
        for h in range(HEADS_PER_GROUP):
            sel = lane_head == h
            hr = slice(h * ATTN_BLOCK, (h + 1) * ATTN_BLOCK)
            o = jnp.where(sel, pv[hr], o)
            mb = jnp.where(sel, m[hr], mb)
            lb = jnp.where(sel, l[hr], lb)
        results.append((o, mb, lb))

    for start, (o, mb, lb) in zip(starts, results):
        for sl in range(2):
            lanes = slice(sl * LANES, (sl + 1) * LANES)
            o_s[bi, sl, rows(start), :] = o[:, lanes]
            m_s[bi, sl, rows(start), :] = mb[:, lanes]
            l_s[bi, sl, rows(start), :] = lb[:, lanes]


def _attn_range(lo, hi, **kw):
    n_groups = (hi - lo) // ATTN_UNROLL

    def body(j, carry):
        base = lo + j * ATTN_UNROLL
        _attn_steps([base + u for u in range(ATTN_UNROLL)], **kw)
        return carry
    if n_groups > 0:
        lax.fori_loop(0, n_groups, body, 0)
    tail = list(range(lo + n_groups * ATTN_UNROLL, hi))
    if tail:
        _attn_steps([jnp.int32(t) for t in tail], **kw)


def _mix_kernel(x_ref, w_ref, bias_ref, cw_ref, out_ref, q_s, k_s, v_s, o_s, m_s, l_s):
    g = pl.program_id(1)
    seq = x_ref.shape[1]
    n_chunks = seq // ROW_CHUNK

    def proj(c, carry):
        rows = pl.ds(pl.multiple_of(c * ROW_CHUNK, ROW_CHUNK), ROW_CHUNK)
        res = jnp.dot(x_ref[0, rows, :].astype(BF16), w_ref[0], preferred_element_type=F32)
        for j, dst in enumerate((q_s, k_s, v_s)):
            for sl in range(2):
                lo = j * GROUP_WIDTH + sl * LANES
                dst[sl, rows, :] = res[:, lo:lo + LANES]
        return carry
    lax.fori_loop(0, n_chunks, proj, 0)

    @pl.when(g < N_ATTN_GROUPS)
    def _attention():
        for bi, (window, dilation) in enumerate(DILATED_BRANCHES):
            n_blocks = seq // dilation // ATTN_BLOCK
            n_steps = n_blocks * dilation
            kw = dict(bi=bi, dilation=dilation, q_s=q_s, k_s=k_s, v_s=v_s,
                      bias_ref=bias_ref, o_s=o_s, m_s=m_s, l_s=l_s)
            _attn_range(0, dilation, first=True, **kw)
            _attn_range(dilation, n_steps, first=False, **kw)

        def combine(c, carry):
            rows = pl.ds(pl.multiple_of(c * ROW_CHUNK, ROW_CHUNK), ROW_CHUNK)
            for sl in range(2):
                ms = [m_s[bi, sl, rows, :] for bi in range(len(DILATED_BRANCHES))]
                mx = jnp.maximum(jnp.maximum(ms[0], ms[1]), ms[2])
                es = [jnp.exp(v - mx) for v in ms]
                den = es[0] * l_s[0, sl, rows, :] + es[1] * l_s[1, sl, rows, :] + es[2] * l_s[2, sl, rows, :]
                num = es[0] * o_s[0, sl, rows, :] + es[1] * o_s[1, sl, rows, :] + es[2] * o_s[2, sl, rows, :]
                out_ref[0, rows, sl * LANES:(sl + 1) * LANES] = (num / den).astype(BF16)
            return carry
        lax.fori_loop(0, n_chunks, combine, 0)

    @pl.when(g == N_ATTN_GROUPS)
    def _short_conv():
        pad = CONV_PAD
        u_s = o_s.at[0]
        for sl in range(2):
            u_s[sl, 0:pad, :] = jnp.zeros((pad, LANES), F32)
            for c in range(n_chunks):
                lo = c * ROW_CHUNK
                u_s[sl, pad + lo:pad + lo + ROW_CHUNK, :] = v_s[sl, lo:lo + ROW_CHUNK, :] * q_s[sl, lo:lo + ROW_CHUNK, :]
            w = [cw_ref[kk:kk + 1, sl * LANES:(sl + 1) * LANES] for kk in range(CONV_K)]
            for c in range(n_chunks):
                lo = c * ROW_CHUNK
                y = w[2] * u_s[sl, pad + lo:pad + lo + ROW_CHUNK, :]
                y = y + w[1] * u_s[sl, pad + lo - 1:pad + lo - 1 + ROW_CHUNK, :]
                y = y + w[0] * u_s[sl, pad + lo - 2:pad + lo - 2 + ROW_CHUNK, :]
                out_ref[0, lo:lo + ROW_CHUNK, sl * LANES:(sl + 1) * LANES] = (
                    k_s[sl, lo:lo + ROW_CHUNK, :] * y).astype(BF16)


def _mix_call(x, w_groups, bias_tbl, conv_w):
    batch, seq, d = x.shape
    return pl.pallas_call(
        _mix_kernel,
        out_shape=jax.ShapeDtypeStruct((batch, seq, d), BF16),
        grid=(batch, N_GROUPS),
        in_specs=[
            pl.BlockSpec((1, seq, d), lambda b, g: (b, 0, 0)),
            pl.BlockSpec((1, d, 3 * GROUP_WIDTH), lambda b, g: (g, 0, 0)),
            pl.BlockSpec((1,) + bias_tbl.shape[1:], lambda b, g: (jnp.minimum(g, N_ATTN_GROUPS - 1), 0, 0, 0)),
            pl.BlockSpec(conv_w.shape, lambda b, g: (0, 0)),
        ],
        out_specs=pl.BlockSpec((1, seq, GROUP_WIDTH), lambda b, g: (b, 0, g)),
        scratch_shapes=[
            pltpu.VMEM((2, seq, LANES), F32),
            pltpu.VMEM((2, seq, LANES), F32),
            pltpu.VMEM((2, seq, LANES), F32),
            pltpu.VMEM((len(DILATED_BRANCHES), 2, seq + CONV_PAD, LANES), F32),
            pltpu.VMEM((len(DILATED_BRANCHES), 2, seq, LANES), F32),
            pltpu.VMEM((len(DILATED_BRANCHES), 2, seq, LANES), F32),
        ],
        compiler_params=pltpu.CompilerParams(
            dimension_semantics=("arbitrary", "arbitrary"), vmem_limit_bytes=VMEM_LIMIT_BYTES),
        name="mix",
    )(x, w_groups, bias_tbl, conv_w)


def _layer_norm(h, g, b):
    mu = jnp.mean(h, axis=-1, keepdims=True)
    c = h - mu
    var = jnp.mean(c * c, axis=-1, keepdims=True)
    return c * lax.rsqrt(var + LN_EPS) * g + b


def _bf16_bits(v):
    return lax.bitcast_convert_type(v.astype(BF16).astype(F32), U32)


def _store_row_tiles(ref, val):
    n = val.shape[0]
    for j in range(ROW_TILE):
        lo = _bf16_bits(val[:, (2 * j) * LANES:(2 * j + 1) * LANES])
        hi = _bf16_bits(val[:, (2 * j + 1) * LANES:(2 * j + 2) * LANES])
        ref[pl.ds(j, n, stride=ROW_TILE), :] = jnp.bitwise_or(lax.shift_right_logical(lo, jnp.uint32(16)), hi)


def _load_row_tiles(ref, n):
    pieces = []
    for j in range(ROW_TILE):
        w = ref[pl.ds(j, n, stride=ROW_TILE), :]
        pieces.append(lax.bitcast_convert_type(lax.shift_left(w, jnp.uint32(16)), F32))
        pieces.append(lax.bitcast_convert_type(jnp.bitwise_and(w, jnp.uint32(0xFFFF0000)), F32))
    return jnp.concatenate(pieces, axis=1)


def _first_argmax_rows(v, row_ids, n_rows):
    m = jnp.max(v, axis=0, keepdims=True)
    idx = jnp.min(jnp.where(v == m, row_ids, n_rows), axis=0, keepdims=True)
    return m, idx


def _post_kernel(mix_ref, x_ref, wo_ref, g_ref, b_ref, wrh_ref, wrl_ref, rb_ref,
                 x1_ref, x1p_ref, e_ref, rank_ref, gate_ref, cnt_ref, carry_s):
    i = pl.program_id(0)
    tm = x_ref.shape[0]

    @pl.when(i == 0)
    def _init():
        carry_s[...] = jnp.zeros_like(carry_s)

    h = DEEPNORM_ALPHA * x_ref[...] + jnp.dot(mix_ref[...], wo_ref[...], preferred_element_type=F32)
    x1 = _layer_norm(h, g_ref[...], b_ref[...])
    x1_ref[...] = x1
    _store_row_tiles(x1p_ref, x1)

    x_hi = x1.astype(BF16)
    x_lo = (x1 - x_hi.astype(F32)).astype(BF16)
    dn = (((1,), (1,)), ((), ()))
    logits = lax.dot_general(wrh_ref[...], x_hi, dn, preferred_element_type=F32)
    logits = logits + lax.dot_general(wrh_ref[...], x_lo, dn, preferred_element_type=F32)
    logits = logits + lax.dot_general(wrl_ref[...], x_hi, dn, preferred_element_type=F32)
    scores = jax.nn.sigmoid(logits)
    biased = scores + rb_ref[...]

    neg_inf = -jnp.inf
    sub_ids = lax.broadcasted_iota(I32, (GROUP_SIZE, tm), 0)
    gs_rows = []
    for gi in range(N_EXPERT_GROUPS):
        bg = biased[gi * GROUP_SIZE:(gi + 1) * GROUP_SIZE]
        m1, i1 = _first_argmax_rows(bg, sub_ids, GROUP_SIZE)
        m2 = jnp.max(jnp.where(sub_ids == i1, neg_inf, bg), axis=0, keepdims=True)
        gs_rows.append(m1 + m2)
    gscore = jnp.concatenate(gs_rows, axis=0)
    grp_ids = lax.broadcasted_iota(I32, (N_EXPERT_GROUPS, tm), 0)
    keep_g = jnp.zeros((N_EXPERT_GROUPS, tm), jnp.bool_)
    for _ in range(TOPK_GROUPS):
        _, gi1 = _first_argmax_rows(gscore, grp_ids, N_EXPERT_GROUPS)
        hit = grp_ids == gi1
        keep_g = jnp.logical_or(keep_g, hit)
        gscore = jnp.where(hit, neg_inf, gscore)
    keep_f = keep_g.astype(F32)
    keep_e = jnp.concatenate(
        [jnp.broadcast_to(keep_f[gi:gi + 1], (GROUP_SIZE, tm)) for gi in range(N_EXPERT_GROUPS)], axis=0)
    masked = jnp.where(keep_e > 0.5, biased, neg_inf)

    exp_ids = lax.broadcasted_iota(I32, (N_EXPERTS, tm), 0)
    sel = jnp.zeros((N_EXPERTS, tm), F32)
    e_rows, g_rows = [], []
    for _ in range(TOP_K):
        _, e1 = _first_argmax_rows(masked, exp_ids, N_EXPERTS)
        hit = exp_ids == e1
        sel = jnp.where(hit, 1.0, sel)
        g_rows.append(jnp.sum(jnp.where(hit, scores, 0.0), axis=0, keepdims=True))
        e_rows.append(e1)
        masked = jnp.where(hit, neg_inf, masked)
    gates = jnp.concatenate(g_rows, axis=0)
    gates = gates / jnp.sum(gates, axis=0, keepdims=True) * ROUTED_SCALE
    gate_ref[...] = gates
    e_ref[...] = jnp.concatenate(e_rows, axis=0)

    t_row = lax.broadcasted_iota(I32, (tm, tm), 0)
    t_col = lax.broadcasted_iota(I32, (tm, tm), 1)
    upper = (t_row < t_col).astype(BF16)
    excl = jnp.dot(sel.astype(BF16), upper, preferred_element_type=F32)
    base = carry_s[...] + excl
    r_rows = [jnp.sum(jnp.where(exp_ids == e1, base, 0.0), axis=0, keepdims=True) for e1 in e_rows]
    rank_ref[...] = jnp.concatenate(r_rows, axis=0).astype(I32)
    carry = carry_s[...] + jnp.sum(sel, axis=1, keepdims=True)
    carry_s[...] = carry
    cnt_ref[...] = carry.astype(I32)


def _post_call(mix2d, x2d, wo, ln_g, ln_b, wr_hi, wr_lo, rb):
    n_tok, d = x2d.shape
    tm = POST_TILE
    tok_spec = pl.BlockSpec((tm, d), lambda i: (i, 0))
    k_spec = pl.BlockSpec((TOP_K, tm), lambda i: (0, i))
    full = lambda a: pl.BlockSpec(a.shape, lambda i: (0,) * a.ndim)
    return pl.pallas_call(
        _post_kernel,
        out_shape=(
            jax.ShapeDtypeStruct((n_tok, d), F32),
            jax.ShapeDtypeStruct((n_tok * ROW_TILE, LANES), U32),
            jax.ShapeDtypeStruct((TOP_K, n_tok), I32),
            jax.ShapeDtypeStruct((TOP_K, n_tok), I32),
            jax.ShapeDtypeStruct((TOP_K, n_tok), F32),
            jax.ShapeDtypeStruct((N_EXPERTS, 1), I32),
        ),
        grid=(n_tok // tm,),
        in_specs=[tok_spec, tok_spec, full(wo), full(ln_g), full(ln_b), full(wr_hi), full(wr_lo), full(rb)],
        out_specs=(tok_spec, pl.BlockSpec((tm * ROW_TILE, LANES), lambda i: (i, 0)), k_spec, k_spec, k_spec,
                   pl.BlockSpec((N_EXPERTS, 1), lambda i: (0, 0))),
        scratch_shapes=[pltpu.VMEM((N_EXPERTS, 1), F32)],
        compiler_params=pltpu.CompilerParams(
            dimension_semantics=("arbitrary",), vmem_limit_bytes=VMEM_LIMIT_BYTES),
        name="post",
    )(mix2d, x2d, wo, ln_g, ln_b, wr_hi, wr_lo, rb)


def _pos_kernel(e_ref, rank_ref, ps_ref, pos_ref, rows_ref):
    tp = e_ref.shape[1]
    exp_ids = lax.broadcasted_iota(I32, (N_EXPERTS, tp), 0)
    ps = ps_ref[...]
    rows = []
    for k in range(TOP_K):
        start = jnp.sum(jnp.where(exp_ids == e_ref[k:k + 1, :], ps, 0), axis=0, keepdims=True)
        rows.append((start + rank_ref[k:k + 1, :]) * ROW_TILE)
    pos = jnp.concatenate(rows, axis=0)
    pos_ref[...] = pos
    src = lax.broadcasted_iota(I32, (tp, tp * ROW_TILE), 0)
    dst = lax.broadcasted_iota(I32, (tp, tp * ROW_TILE), 1)
    spread = (lax.shift_right_logical(dst, int(math.log2(ROW_TILE))) == src).astype(BF16)
    out = lax.broadcasted_iota(I32, (TOP_K, tp * ROW_TILE), 1) & (ROW_TILE - 1)
    for byte in range(3):
        part = jnp.bitwise_and(lax.shift_right_logical(pos, 8 * byte), 255).astype(F32).astype(BF16)
        wide = jnp.dot(part, spread, preferred_element_type=F32).astype(I32)
        out = out + lax.shift_left(wide, 8 * byte)
    rows_ref[...] = out


def _pos_call(e_idx, rank, pstarts):
    n_tok = e_idx.shape[1]
    tp = POS_TILE
    k_spec = pl.BlockSpec((TOP_K, tp), lambda i: (0, i))
    return pl.pallas_call(
        _pos_kernel,
        out_shape=(jax.ShapeDtypeStruct((TOP_K, n_tok), I32),
                   jax.ShapeDtypeStruct((TOP_K, n_tok * ROW_TILE), I32)),
        grid=(n_tok // tp,),
        in_specs=[k_spec, k_spec, pl.BlockSpec((N_EXPERTS, 1), lambda i: (0, 0))],
        out_specs=(k_spec, pl.BlockSpec((TOP_K, tp * ROW_TILE), lambda i: (0, i))),
        compiler_params=pltpu.CompilerParams(dimension_semantics=("arbitrary",)),
        name="positions",
    )(e_idx, rank, pstarts)


def _sc_row_scatter(rows, row_idx, n_out_rows):
    n_k = row_idx.shape[0]
    windows = row_idx.shape[1] // SC_GATHER_WINDOW
    mesh = plsc.VectorSubcoreMesh(core_axis_name="c", subcore_axis_name="s")

    @functools.partial(
        pl.kernel, out_type=jax.ShapeDtypeStruct((n_out_rows, LANES), rows.dtype),
        mesh=mesh, compiler_params=pltpu.CompilerParams(use_tc_tiling_on_sc=True), name="sc_row_scatter")
    def scatter(rows_hbm, idx_hbm, out_hbm):
        def body(rows_vmem, idx_vmem):
            pltpu.sync_copy(rows_vmem, out_hbm.at[idx_vmem.at[0]])
        pltpu.emit_pipeline(
            body, grid=(n_k, windows),
            in_specs=[pl.BlockSpec((SC_GATHER_WINDOW, LANES), lambda k, i: (i, 0)),
                      pl.BlockSpec((1, SC_GATHER_WINDOW), lambda k, i: (k, i))],
            out_specs=[],
            core_axis_name=("c", "s"), dimension_semantics=(pltpu.PARALLEL, pltpu.PARALLEL),
        )(rows_hbm, idx_hbm)
    return scatter(rows, row_idx)


def _pad_fill_kernel(cnt_ref, ps_ref, xs_in_ref, xs_ref, zbuf, zsem):
    del xs_in_ref

    def pad_copies(e, fn):
        cnt = cnt_ref[e]
        pad = jnp.bitwise_and(-cnt, MOE_BLOCK - 1)
        off = ps_ref[e] + cnt
        for b in range(int(math.log2(MOE_BLOCK))):
            size = 1 << b
            hit = jnp.bitwise_and(pad, size)

            @pl.when(hit != 0)
            def _(off=off, size=size):
                fn(pltpu.make_async_copy(
                    zbuf.at[pl.ds(0, size * ROW_TILE)],
                    xs_ref.at[pl.ds(pl.multiple_of(off * ROW_TILE, ROW_TILE), size * ROW_TILE)], zsem))
            off = off + hit

    zbuf[...] = jnp.zeros_like(zbuf)

    def start(e, carry):
        pad_copies(e, lambda cp: cp.start())
        return carry

    def wait(e, carry):
        pad_copies(e, lambda cp: cp.wait())
        return carry
    lax.fori_loop(0, N_EXPERTS, start, 0)
    lax.fori_loop(0, N_EXPERTS, wait, 0)


def _pad_fill_call(counts, pstarts, xs):
    grid_spec = pltpu.PrefetchScalarGridSpec(
        num_scalar_prefetch=2,
        grid=(1,),
        in_specs=[pl.BlockSpec(memory_space=pl.ANY)],
        out_specs=pl.BlockSpec(memory_space=pl.ANY),
        scratch_shapes=[
            pltpu.VMEM((MOE_BLOCK // 2 * ROW_TILE, LANES), U32),
            pltpu.SemaphoreType.DMA(()),
        ],
    )
    return pl.pallas_call(
        _pad_fill_kernel,
        out_shape=jax.ShapeDtypeStruct(xs.shape, xs.dtype),
        grid_spec=grid_spec,
        input_output_aliases={2: 0},
        compiler_params=pltpu.CompilerParams(dimension_semantics=("arbitrary",)),
        name="pad_fill",
    )(counts, pstarts, xs)


def _experts_kernel(ps_ref, nb_ref, xs_ref, wg_ref, wu_ref, wd_ref, y_ref,
                    xbuf, ybuf, wg_b, wu_b, wd_b, sem_in, sem_out):
    e = pl.program_id(0)
    nb = nb_ref[e]
    blk_rows = MOE_BLOCK * ROW_TILE
    g0 = ps_ref[e] // MOE_BLOCK
    n_total = (ps_ref[N_EXPERTS - 1] // MOE_BLOCK) + nb_ref[N_EXPERTS - 1]
    ahead = EXPERT_SLOTS - 1

    def rows_of(g):
        return pl.ds(pl.multiple_of(g * blk_rows, blk_rows), blk_rows)

    def slot_of(g):
        return jnp.bitwise_and(g, EXPERT_SLOTS - 1)

    def in_copy(g):
        s = slot_of(g)
        return pltpu.make_async_copy(xs_ref.at[rows_of(g)], xbuf.at[s], sem_in.at[s])

    def out_copy(g):
        s = slot_of(g)
        return pltpu.make_async_copy(ybuf.at[s], y_ref.at[rows_of(g)], sem_out.at[s])

    @pl.when(e == 0)
    def _prime():
        for u in range(ahead):
            @pl.when(u < n_total)
            def _(u=u):
                in_copy(u).start()

    @pl.when(nb > 0)
    def _run():
        wg_b[...] = wg_ref[0].astype(BF16)
        wu_b[...] = wu_ref[0].astype(BF16)
        wd_b[...] = wd_ref[0].astype(BF16)

        def block(j, carry):
            g = g0 + j
            s = slot_of(g)
            in_copy(g).wait()

            @pl.when(g + ahead < n_total)
            def _prefetch():
                in_copy(g + ahead).start()

            @pl.when(g >= EXPERT_SLOTS)
            def _reclaim():
                out_copy(g - EXPERT_SLOTS).wait()

            xb = _load_row_tiles(xbuf.at[s], MOE_BLOCK).astype(BF16)
            hg = jnp.dot(xb, wg_b[...], preferred_element_type=F32)
            hu = jnp.dot(xb, wu_b[...], preferred_element_type=F32)
            hh = (hg * jax.nn.sigmoid(hg) * hu).astype(BF16)
            _store_row_tiles(ybuf.at[s], jnp.dot(hh, wd_b[...], preferred_element_type=F32))
            out_copy(g).start()
            return carry
        lax.fori_loop(0, nb, block, 0)

    @pl.when(e == N_EXPERTS - 1)
    def _drain():
        for u in range(EXPERT_SLOTS):
            @pl.when(u < n_total)
            def _(u=u):
                out_copy(u).wait()


def _experts_call(pstarts, n_blk, xs, w_gate, w_up, w_down):
    d = D_MODEL

    def wsel(e, ps, nb):
        return (e, 0, 0)
    grid_spec = pltpu.PrefetchScalarGridSpec(
        num_scalar_prefetch=2,
        grid=(N_EXPERTS,),
        in_specs=[
            pl.BlockSpec(memory_space=pl.ANY),
            pl.BlockSpec((1, d, EXPERT_DIM), wsel),
            pl.BlockSpec((1, d, EXPERT_DIM), wsel),
            pl.BlockSpec((1, EXPERT_DIM, d), wsel),
        ],
        out_specs=pl.BlockSpec(memory_space=pl.ANY),
        scratch_shapes=[
            pltpu.VMEM((EXPERT_SLOTS, MOE_BLOCK * ROW_TILE, LANES), U32),
            pltpu.VMEM((EXPERT_SLOTS, MOE_BLOCK * ROW_TILE, LANES), U32),
            pltpu.VMEM((d, EXPERT_DIM), BF16),
            pltpu.VMEM((d, EXPERT_DIM), BF16),
            pltpu.VMEM((EXPERT_DIM, d), BF16),
            pltpu.SemaphoreType.DMA((EXPERT_SLOTS,)),
            pltpu.SemaphoreType.DMA((EXPERT_SLOTS,)),
        ],
    )
    return pl.pallas_call(
        _experts_kernel,
        out_shape=jax.ShapeDtypeStruct(xs.shape, U32),
        grid_spec=grid_spec,
        compiler_params=pltpu.CompilerParams(
            dimension_semantics=("arbitrary",), vmem_limit_bytes=VMEM_LIMIT_BYTES),
        name="experts",
    )(pstarts, n_blk, xs, w_gate, w_up, w_down)


def _sc_row_gather(table, row_idx, chunk, n_chunks):
    n_k = row_idx.shape[0]
    windows = row_idx.shape[1] // n_chunks // SC_GATHER_WINDOW
    mesh = plsc.VectorSubcoreMesh(core_axis_name="c", subcore_axis_name="s")

    @functools.partial(
        pl.kernel, out_type=jax.ShapeDtypeStruct((n_k * windows * SC_GATHER_WINDOW, LANES), table.dtype),
        mesh=mesh, compiler_params=pltpu.CompilerParams(use_tc_tiling_on_sc=True), name="sc_row_gather")
    def gather(table_hbm, idx_hbm, out_hbm):
        def body(idx_vmem, out_vmem):
            pltpu.sync_copy(table_hbm.at[idx_vmem.at[0]], out_vmem)
        pltpu.emit_pipeline(
            body, grid=(n_k, windows),
            in_specs=[pl.BlockSpec((1, SC_GATHER_WINDOW), lambda k, i: (k, chunk * windows + i))],
            out_specs=[pl.BlockSpec((SC_GATHER_WINDOW, LANES), lambda k, i: (k * windows + i, 0))],
            core_axis_name=("c", "s"), dimension_semantics=(pltpu.PARALLEL, pltpu.PARALLEL),
        )(idx_hbm, out_hbm)
    return gather(table, row_idx)


def _combine_kernel(gate_ref, x1_ref, *refs):
    yk_refs = refs[:TOP_K]
    sg_ref, su_ref, sd_ref, g_ref, b_ref = refs[TOP_K:TOP_K + 5]
    out_ref = refs[-1]
    tt = x1_ref.shape[0]
    x1 = x1_ref[...]
    xb = x1.astype(BF16)
    hg = jnp.dot(xb, sg_ref[...], preferred_element_type=F32)
    hu = jnp.dot(xb, su_ref[...], preferred_element_type=F32)
    hh = (hg * jax.nn.sigmoid(hg) * hu).astype(BF16)
    acc = DEEPNORM_ALPHA * x1 + jnp.dot(hh, sd_ref[...], preferred_element_type=F32)
    gates = gate_ref[...]
    for k in range(TOP_K):
        acc = acc + gates[:, k:k + 1] * _load_row_tiles(yk_refs[k], tt)
    out_ref[...] = _layer_norm(acc, g_ref[...], b_ref[...])


def _combine_call(chunk, n_chunks, gates_t, x1, y_chunk, sg, su, sd, ln_g, ln_b, carry):
    n_tok, d = x1.shape
    tt = COMBINE_TILE
    tiles = n_tok // n_chunks // tt
    first = chunk * tiles
    full = lambda a: pl.BlockSpec(a.shape, lambda i: (0,) * a.ndim)
    y_specs = [pl.BlockSpec((tt * ROW_TILE, LANES), lambda i, k=k: (k * tiles + i, 0)) for k in range(TOP_K)]
    in_specs = [
        pl.BlockSpec((tt, TOP_K), lambda i: (first + i, 0)),
        pl.BlockSpec((tt, d), lambda i: (first + i, 0)),
        *y_specs,
        full(sg), full(su), full(sd), full(ln_g), full(ln_b),
    ]
    operands = [gates_t, x1] + [y_chunk] * TOP_K + [sg, su, sd, ln_g, ln_b]
    aliases = {}
    if carry is not None:
        in_specs.append(pl.BlockSpec(memory_space=pl.ANY))
        operands.append(carry)
        aliases = {len(operands) - 1: 0}
    return pl.pallas_call(
        _combine_kernel,
        out_shape=jax.ShapeDtypeStruct((n_tok, d), F32),
        grid=(tiles,),
        in_specs=in_specs,
        out_specs=pl.BlockSpec((tt, d), lambda i: (first + i, 0)),
        input_output_aliases=aliases,
        compiler_params=pltpu.CompilerParams(
            dimension_semantics=("arbitrary",), vmem_limit_bytes=VMEM_LIMIT_BYTES),
        name="combine",
    )(*operands)


def _group_weights(w_in):
    blocks = w_in.reshape(D_MODEL, 12, GROUP_WIDTH)
    scale = HEAD_DIM ** -0.5
    groups = []
    for g in range(N_ATTN_GROUPS):
        groups.append(jnp.concatenate(
            [blocks[:, g] * scale, blocks[:, N_ATTN_GROUPS + g], blocks[:, 2 * N_ATTN_GROUPS + g]], axis=1))
    groups.append(jnp.concatenate([blocks[:, 9], blocks[:, 10], blocks[:, 11]], axis=1))
    return jnp.stack(groups).astype(BF16)


def kernel(x, w_in, conv_w, w_o, ln1_g, ln1_b, rel_bias, w_router, router_bias, exp_w_gate, exp_w_up,
           exp_w_down, sh_w_gate, sh_w_up, sh_w_down, ln2_g, ln2_b):
    batch, seq, d = x.shape
    assert d == D_MODEL and seq == DILATED_BRANCHES[-1][0] and w_in.shape[0] == DEPTH
    n_tok = batch * seq
    assert n_tok % POST_TILE == 0 and n_tok % DISPATCH_TILE == 0 and n_tok % COMBINE_TILE == 0

    mix = _mix_call(x, _group_weights(w_in[0]), _bias_table(rel_bias), conv_w[0])

    wr = w_router[0]
    wr_hi = wr.astype(BF16)
    wr_lo = (wr - wr_hi.astype(F32)).astype(BF16)
    x2d = x.reshape(n_tok, d)
    x1, x1p, e_idx, rank, gates, counts = _post_call(
        mix.reshape(n_tok, d), x2d, w_o[0].astype(BF16), ln1_g, ln1_b,
        wr_hi.T, wr_lo.T, router_bias[0].reshape(N_EXPERTS, 1))

    counts = counts[:, 0]
    padded = (counts + MOE_BLOCK - 1) // MOE_BLOCK * MOE_BLOCK
    pends = jnp.cumsum(padded)
    pstarts = (pends - padded).astype(I32)
    n_blocks = n_tok * TOP_K // MOE_BLOCK + N_EXPERTS
    n_blk = (padded // MOE_BLOCK).astype(I32)
    pos, row_idx = _pos_call(e_idx, rank, pstarts.reshape(N_EXPERTS, 1))

    xs = _sc_row_scatter(x1p, row_idx, n_blocks * MOE_BLOCK * ROW_TILE)
    xs = _pad_fill_call(counts, pstarts, xs)
    y = _experts_call(pstarts, n_blk, xs, exp_w_gate[0], exp_w_up[0], exp_w_down[0])

    tc = n_tok // COMBINE_CHUNKS
    gates_t = gates.T
    sh = (sh_w_gate[0].astype(BF16), sh_w_up[0].astype(BF16), sh_w_down[0].astype(BF16))
    out = None
    for c in range(COMBINE_CHUNKS):
        y_chunk = _sc_row_gather(y, row_idx, c, COMBINE_CHUNKS)
        out = _combine_call(c, COMBINE_CHUNKS, gates_t, x1, y_chunk, *sh, ln2_g, ln2_b, out)
    return out.reshape(batch, seq, d)
```

```python
import functools
import math

import numpy as np
import jax
import jax.numpy as jnp
from jax import lax
from jax.experimental import pallas as pl
from jax.experimental.pallas import tpu as pltpu
from jax.experimental.pallas import tpu_sc as plsc

F32 = jnp.float32
BF16 = jnp.bfloat16
I32 = jnp.int32
U32 = jnp.uint32

D_MODEL = 1024
HEAD_DIM = 64
N_ATTN_HEADS = 12
ATTN_WIDTH = N_ATTN_HEADS * HEAD_DIM
CONV_WIDTH = D_MODEL - ATTN_WIDTH
CONV_K = 3
DILATED_BRANCHES = ((128, 1), (512, 4), (2048, 16))
ATTN_BLOCK = 128
N_REL_BUCKETS = 32
REL_MAX_DISTANCE = 2048
N_EXPERTS = 256
TOP_K = 8
N_EXPERT_GROUPS = 8
TOPK_GROUPS = 4
GROUP_SIZE = N_EXPERTS // N_EXPERT_GROUPS
EXPERT_DIM = 256
SHARED_DIM = 256
ROUTED_SCALE = 2.5
DEPTH = 1
DEEPNORM_ALPHA = (2.0 * DEPTH) ** 0.25
LN_EPS = 1e-5

LANES = 128
GROUP_WIDTH = 2 * LANES
HEADS_PER_GROUP = GROUP_WIDTH // HEAD_DIM
N_GROUPS = D_MODEL // GROUP_WIDTH
N_ATTN_GROUPS = ATTN_WIDTH // GROUP_WIDTH
VMEM_LIMIT_BYTES = 56 * 1024 * 1024

MASK_VALUE = -1e30

ROW_CHUNK = 256
ATTN_UNROLL = 8
CONV_PAD = 8
ROW_TILE = 4
POS_TILE = 512
POST_TILE = 512
MOE_BLOCK = 512
EXPERT_SLOTS = 4
DISPATCH_TILE = 256
COMBINE_TILE = 512
SC_GATHER_WINDOW = 128
COMBINE_CHUNKS = 8


def _t5_bucket(dist):
    max_exact = N_REL_BUCKETS // 2
    dist = np.asarray(dist)
    log_part = np.log(np.maximum(dist, 1) / max_exact) / math.log(REL_MAX_DISTANCE / max_exact)
    large = max_exact + (log_part * (N_REL_BUCKETS - max_exact)).astype(np.int32)
    large = np.minimum(large, N_REL_BUCKETS - 1)
    return np.where(dist < max_exact, dist, large).astype(np.int32)


def _branch_bucket_and_band(window, dilation):
    w_sub = window // dilation
    qi = np.arange(ATTN_BLOCK)[:, None]
    ki = np.arange(2 * ATTN_BLOCK)[None, :]
    steps = qi + ATTN_BLOCK - ki
    band = (steps >= 0) & (steps <= w_sub)
    bucket = _t5_bucket(np.clip(steps, 0, w_sub) * dilation)
    return bucket, band


def _bias_table(rel_bias):
    tabs = []
    for window, dilation in DILATED_BRANCHES:
        bucket, band = _branch_bucket_and_band(window, dilation)
        onehot = (bucket[:, :, None] == np.arange(N_REL_BUCKETS)).astype(np.float32)
        b = jnp.einsum("qkb,bh->qkh", onehot, rel_bias.astype(F32), precision=lax.Precision.HIGHEST)
        b = jnp.where(band[:, :, None], b, MASK_VALUE)
        tabs.append(b.transpose(2, 0, 1))
    t = jnp.stack(tabs, axis=1)
    t = t.reshape(N_ATTN_GROUPS, HEADS_PER_GROUP, len(DILATED_BRANCHES), ATTN_BLOCK, 2 * ATTN_BLOCK)
    t = t.transpose(0, 2, 1, 3, 4)
    return t.reshape(N_ATTN_GROUPS, len(DILATED_BRANCHES), HEADS_PER_GROUP * ATTN_BLOCK, 2 * ATTN_BLOCK)


def _lane_head():
    return lax.shift_right_logical(lax.broadcasted_iota(I32, (ATTN_BLOCK, GROUP_WIDTH), 1), 6)


def _attn_steps(its, bi, dilation, first, q_s, k_s, v_s, bias_ref, o_s, m_s, l_s):
    logd = int(math.log2(dilation))

    def rows(st):
        return pl.ds(st, ATTN_BLOCK) if dilation == 1 else pl.ds(st, ATTN_BLOCK, stride=dilation)

    def ld(ref, st):
        return jnp.concatenate([ref[0, rows(st), :], ref[1, rows(st), :]], axis=1)

    lane_head = _lane_head()
    bias = bias_ref[0, bi, :, ATTN_BLOCK:] if first else bias_ref[0, bi]
    starts, operands = [], []
    for it in its:
        r = jnp.bitwise_and(it, dilation - 1)
        n = lax.shift_right_logical(it, logd)
        start = r + (dilation * ATTN_BLOCK) * n
        qf = ld(q_s, start)
        if first:
            kk = ld(k_s, start).astype(BF16)
            vv = ld(v_s, start).astype(BF16)
        else:
            prev = start - dilation * ATTN_BLOCK
            kk = jnp.concatenate([ld(k_s, prev), ld(k_s, start)], axis=0).astype(BF16)
            vv = jnp.concatenate([ld(v_s, prev), ld(v_s, start)], axis=0).astype(BF16)
        starts.append(start)
        operands.append((qf, kk, vv))

    results = []
    for qf, kk, vv in operands:
        q4 = jnp.concatenate(
            [jnp.where(lane_head == h, qf, 0.0) for h in range(HEADS_PER_GROUP)], axis=0).astype(BF16)
        s = lax.dot_general(q4, kk, (((1,), (1,)), ((), ())), preferred_element_type=F32) + bias
        m = jnp.max(s, axis=-1, keepdims=True)
        p = jnp.exp(s - m)
        l = jnp.sum(p, axis=-1, keepdims=True)
        pv = jnp.dot(p.astype(BF16), vv, preferred_element_type=F32)
        o = jnp.zeros((ATTN_BLOCK, GROUP_WIDTH), F32)
        mb = jnp.zeros((ATTN_BLOCK, GROUP_WIDTH), F32)
        lb = jnp.zeros((ATTN_BLOCK, GROUP_WIDTH), F32)
        for h in range(HEADS_PER_GROUP):
            sel = lane_head == h
            hr = slice(h * ATTN_BLOCK, (h + 1) * ATTN_BLOCK)
            o = jnp.where(sel, pv[hr], o)
            mb = jnp.where(sel, m[hr], mb)
            lb = jnp.where(sel, l[hr], lb)
        results.append((o, mb, lb))

    for start, (o, mb, lb) in zip(starts, results):
        for sl in range(2):
            lanes = slice(sl * LANES, (sl + 1) * LANES)
            o_s[bi, sl, rows(start), :] = o[:, lanes]
            m_s[bi, sl, rows(start), :] = mb[:, lanes]
            l_s[bi, sl, rows(start), :] = lb[:, lanes]


def _attn_range(lo, hi, **kw):
    n_groups = (hi - lo) // ATTN_UNROLL

    def body(j, carry):
        base = lo + j * ATTN_UNROLL
        _attn_steps([base + u for u in range(ATTN_UNROLL)], **kw)
        return carry
    if n_groups > 0:
        lax.fori_loop(0, n_groups, body, 0)
    tail = list(range(lo + n_groups * ATTN_UNROLL, hi))
    if tail:
        _attn_steps([jnp.int32(t) for t in tail], **kw)


def _mix_kernel(x_ref, w_ref, bias_ref, cw_ref, out_ref, q_s, k_s, v_s, o_s, m_s, l_s):
    g = pl.program_id(1)
    seq = x_ref.shape[1]
    n_chunks = seq // ROW_CHUNK

    def proj(c, carry):
        rows = pl.ds(pl.multiple_of(c * ROW_CHUNK, ROW_CHUNK), ROW_CHUNK)
        res = jnp.dot(x_ref[0, rows, :].astype(BF16), w_ref[0], preferred_element_type=F32)
        for j, dst in enumerate((q_s, k_s, v_s)):
            for sl in range(2):
                lo = j * GROUP_WIDTH + sl * LANES
                dst[sl, rows, :] = res[:, lo:lo + LANES]
        return carry
    lax.fori_loop(0, n_chunks, proj, 0)

    @pl.when(g < N_ATTN_GROUPS)
    def _attention():
        for bi, (window, dilation) in enumerate(DILATED_BRANCHES):
            n_blocks = seq // dilation // ATTN_BLOCK
            n_steps = n_blocks * dilation
            kw = dict(bi=bi, dilation=dilation, q_s=q_s, k_s=k_s, v_s=v_s,
                      bias_ref=bias_ref, o_s=o_s, m_s=m_s, l_s=l_s)
            _attn_range(0, dilation, first=True, **kw)
            _attn_range(dilation, n_steps, first=False, **kw)

        def combine(c, carry):
            rows = pl.ds(pl.multiple_of(c * ROW_CHUNK, ROW_CHUNK), ROW_CHUNK)
            for sl in range(2):
                ms = [m_s[bi, sl, rows, :] for bi in range(len(DILATED_BRANCHES))]
                mx = jnp.maximum(jnp.maximum(ms[0], ms[1]), ms[2])
                es = [jnp.exp(v - mx) for v in ms]
                den = es[0] * l_s[0, sl, rows, :] + es[1] * l_s[1, sl, rows, :] + es[2] * l_s[2, sl, rows, :]
                num = es[0] * o_s[0, sl, rows, :] + es[1] * o_s[1, sl, rows, :] + es[2] * o_s[2, sl, rows, :]
                out_ref[0, rows, sl * LANES:(sl + 1) * LANES] = (num / den).astype(BF16)
            return carry
        lax.fori_loop(0, n_chunks, combine, 0)

    @pl.when(g == N_ATTN_GROUPS)
    def _short_conv():
        pad = CONV_PAD
        u_s = o_s.at[0]
        for sl in range(2):
            u_s[sl, 0:pad, :] = jnp.zeros((pad, LANES), F32)
            for c in range(n_chunks):
                lo = c * ROW_CHUNK
                u_s[sl, pad + lo:pad + lo + ROW_CHUNK, :] = v_s[sl, lo:lo + ROW_CHUNK, :] * q_s[sl, lo:lo + ROW_CHUNK, :]
            w = [cw_ref[kk:kk + 1, sl * LANES:(sl + 1) * LANES] for kk in range(CONV_K)]
            for c in range(n_chunks):
                lo = c * ROW_CHUNK
                y = w[2] * u_s[sl, pad + lo:pad + lo + ROW_CHUNK, :]
                y = y + w[1] * u_s[sl, pad + lo - 1:pad + lo - 1 + ROW_CHUNK, :]
                y = y + w[0] * u_s[sl, pad + lo - 2:pad + lo - 2 + ROW_CHUNK, :]
                out_ref[0, lo:lo + ROW_CHUNK, sl * LANES:(sl + 1) * LANES] = (
                    k_s[sl, lo:lo + ROW_CHUNK, :] * y).astype(BF16)


def _mix_call(x, w_groups, bias_tbl, conv_w):
    batch, seq, d = x.shape
    return pl.pallas_call(
        _mix_kernel,
        out_shape=jax.ShapeDtypeStruct((batch, seq, d), BF16),
        grid=(batch, N_GROUPS),
        in_specs=[
            pl.BlockSpec((1, seq, d), lambda b, g: (b, 0, 0)),
            pl.BlockSpec((1, d, 3 * GROUP_WIDTH), lambda b, g: (g, 0, 0)),
            pl.BlockSpec((1,) + bias_tbl.shape[1:], lambda b, g: (jnp.minimum(g, N_ATTN_GROUPS - 1), 0, 0, 0)),
            pl.BlockSpec(conv_w.shape, lambda b, g: (0, 0)),
        ],
        out_specs=pl.BlockSpec((1, seq, GROUP_WIDTH), lambda b, g: (b, 0, g)),
        scratch_shapes=[
            pltpu.VMEM((2, seq, LANES), F32),
            pltpu.VMEM((2, seq, LANES), F32),
            pltpu.VMEM((2, seq, LANES), F32),
            pltpu.VMEM((len(DILATED_BRANCHES), 2, seq + CONV_PAD, LANES), F32),
            pltpu.VMEM((len(DILATED_BRANCHES), 2, seq, LANES), F32),
            pltpu.VMEM((len(DILATED_BRANCHES), 2, seq, LANES), F32),
        ],
        compiler_params=pltpu.CompilerParams(
            dimension_semantics=("arbitrary", "arbitrary"), vmem_limit_bytes=VMEM_LIMIT_BYTES),
        name="mix",
    )(x, w_groups, bias_tbl, conv_w)


def _layer_norm(h, g, b):
    mu = jnp.mean(h, axis=-1, keepdims=True)
    c = h - mu
    var = jnp.mean(c * c, axis=-1, keepdims=True)
    return c * lax.rsqrt(var + LN_EPS) * g + b


def _bf16_bits(v):
    return lax.bitcast_convert_type(v.astype(BF16).astype(F32), U32)


def _store_row_tiles(ref, val):
    n = val.shape[0]
    for j in range(ROW_TILE):
        lo = _bf16_bits(val[:, (2 * j) * LANES:(2 * j + 1) * LANES])
        hi = _bf16_bits(val[:, (2 * j + 1) * LANES:(2 * j + 2) * LANES])
        ref[pl.ds(j, n, stride=ROW_TILE), :] = jnp.bitwise_or(lax.shift_right_logical(lo, jnp.uint32(16)), hi)


def _load_row_tiles(ref, n):
    pieces = []
    for j in range(ROW_TILE):
        w = ref[pl.ds(j, n, stride=ROW_TILE), :]
        pieces.append(lax.bitcast_convert_type(lax.shift_left(w, jnp.uint32(16)), F32))
        pieces.append(lax.bitcast_convert_type(jnp.bitwise_and(w, jnp.uint32(0xFFFF0000)), F32))
    return jnp.concatenate(pieces, axis=1)


def _first_argmax_rows(v, row_ids, n_rows):
    m = jnp.max(v, axis=0, keepdims=True)
    idx = jnp.min(jnp.where(v == m, row_ids, n_rows), axis=0, keepdims=True)
    return m, idx


def _post_kernel(mix_ref, x_ref, wo_ref, g_ref, b_ref, wrh_ref, wrl_ref, rb_ref, sg_ref, su_ref, sd_ref,
                 base_ref, x1p_ref, e_ref, rank_ref, gate_ref, cnt_ref, carry_s):
    i = pl.program_id(0)
    tm = x_ref.shape[0]

    @pl.when(i == 0)
    def _init():
        carry_s[...] = jnp.zeros_like(carry_s)

    h = DEEPNORM_ALPHA * x_ref[...] + jnp.dot(mix_ref[...], wo_ref[...], preferred_element_type=F32)
    x1 = _layer_norm(h, g_ref[...], b_ref[...])
    _store_row_tiles(x1p_ref, x1)

    x_hi = x1.astype(BF16)
    hg = jnp.dot(x_hi, sg_ref[...], preferred_element_type=F32)
    hu = jnp.dot(x_hi, su_ref[...], preferred_element_type=F32)
    hh = (hg * jax.nn.sigmoid(hg) * hu).astype(BF16)
    base_ref[...] = DEEPNORM_ALPHA * x1 + jnp.dot(hh, sd_ref[...], preferred_element_type=F32)

    x_lo = (x1 - x_hi.astype(F32)).astype(BF16)
    dn = (((1,), (1,)), ((), ()))
    logits = lax.dot_general(wrh_ref[...], x_hi, dn, preferred_element_type=F32)
    logits = logits + lax.dot_general(wrh_ref[...], x_lo, dn, preferred_element_type=F32)
    logits = logits + lax.dot_general(wrl_ref[...], x_hi, dn, preferred_element_type=F32)
    scores = jax.nn.sigmoid(logits)
    biased = scores + rb_ref[...]

    neg_inf = -jnp.inf
    sub_ids = lax.broadcasted_iota(I32, (GROUP_SIZE, tm), 0)
    gs_rows = []
    for gi in range(N_EXPERT_GROUPS):
        bg = biased[gi * GROUP_SIZE:(gi + 1) * GROUP_SIZE]
        m1, i1 = _first_argmax_rows(bg, sub_ids, GROUP_SIZE)
        m2 = jnp.max(jnp.where(sub_ids == i1, neg_inf, bg), axis=0, keepdims=True)
        gs_rows.append(m1 + m2)
    gscore = jnp.concatenate(gs_rows, axis=0)
    grp_ids = lax.broadcasted_iota(I32, (N_EXPERT_GROUPS, tm), 0)
    keep_g = jnp.zeros((N_EXPERT_GROUPS, tm), jnp.bool_)
    for _ in range(TOPK_GROUPS):
        _, gi1 = _first_argmax_rows(gscore, grp_ids, N_EXPERT_GROUPS)
        hit = grp_ids == gi1
        keep_g = jnp.logical_or(keep_g, hit)
        gscore = jnp.where(hit, neg_inf, gscore)
    keep_f = keep_g.astype(F32)
    keep_e = jnp.concatenate(
        [jnp.broadcast_to(keep_f[gi:gi + 1], (GROUP_SIZE, tm)) for gi in range(N_EXPERT_GROUPS)], axis=0)
    masked = jnp.where(keep_e > 0.5, biased, neg_inf)

    exp_ids = lax.broadcasted_iota(I32, (N_EXPERTS, tm), 0)
    sel = jnp.zeros((N_EXPERTS, tm), F32)
    e_rows, g_rows = [], []
    for _ in range(TOP_K):
        _, e1 = _first_argmax_rows(masked, exp_ids, N_EXPERTS)
        hit = exp_ids == e1
        sel = jnp.where(hit, 1.0, sel)
        g_rows.append(jnp.sum(jnp.where(hit, scores, 0.0), axis=0, keepdims=True))
        e_rows.append(e1)
        masked = jnp.where(hit, neg_inf, masked)
    gates = jnp.concatenate(g_rows, axis=0)
    gates = gates / jnp.sum(gates, axis=0, keepdims=True) * ROUTED_SCALE
    gate_ref[...] = gates
    e_ref[...] = jnp.concatenate(e_rows, axis=0)

    t_row = lax.broadcasted_iota(I32, (tm, tm), 0)
    t_col = lax.broadcasted_iota(I32, (tm, tm), 1)
    upper = (t_row < t_col).astype(BF16)
    excl = jnp.dot(sel.astype(BF16), upper, preferred_element_type=F32)
    base = carry_s[...] + excl
    r_rows = [jnp.sum(jnp.where(exp_ids == e1, base, 0.0), axis=0, keepdims=True) for e1 in e_rows]
    rank_ref[...] = jnp.concatenate(r_rows, axis=0).astype(I32)
    carry = carry_s[...] + jnp.sum(sel, axis=1, keepdims=True)
    carry_s[...] = carry
    cnt_ref[...] = carry.astype(I32)


def _post_call(mix2d, x2d, wo, ln_g, ln_b, wr_hi, wr_lo, rb, sg, su, sd):
    n_tok, d = x2d.shape
    tm = POST_TILE
    tok_spec = pl.BlockSpec((tm, d), lambda i: (i, 0))
    k_spec = pl.BlockSpec((TOP_K, tm), lambda i: (0, i))
    full = lambda a: pl.BlockSpec(a.shape, lambda i: (0,) * a.ndim)
    return pl.pallas_call(
        _post_kernel,
        out_shape=(
            jax.ShapeDtypeStruct((n_tok, d), F32),
            jax.ShapeDtypeStruct((n_tok * ROW_TILE, LANES), U32),
            jax.ShapeDtypeStruct((TOP_K, n_tok), I32),
            jax.ShapeDtypeStruct((TOP_K, n_tok), I32),
            jax.ShapeDtypeStruct((TOP_K, n_tok), F32),
            jax.ShapeDtypeStruct((N_EXPERTS, 1), I32),
        ),
        grid=(n_tok // tm,),
        in_specs=[tok_spec, tok_spec, full(wo), full(ln_g), full(ln_b), full(wr_hi), full(wr_lo), full(rb),
                  full(sg), full(su), full(sd)],
        out_specs=(tok_spec, pl.BlockSpec((tm * ROW_TILE, LANES), lambda i: (i, 0)), k_spec, k_spec, k_spec,
                   pl.BlockSpec((N_EXPERTS, 1), lambda i: (0, 0))),
        scratch_shapes=[pltpu.VMEM((N_EXPERTS, 1), F32)],
        compiler_params=pltpu.CompilerParams(
            dimension_semantics=("arbitrary",), vmem_limit_bytes=VMEM_LIMIT_BYTES),
        name="post",
    )(mix2d, x2d, wo, ln_g, ln_b, wr_hi, wr_lo, rb, sg, su, sd)


def _pos_kernel(e_ref, rank_ref, ps_ref, pos_ref, rows_ref):
    tp = e_ref.shape[1]
    exp_ids = lax.broadcasted_iota(I32, (N_EXPERTS, tp), 0)
    ps = ps_ref[...]
    rows = []
    for k in range(TOP_K):
        start = jnp.sum(jnp.where(exp_ids == e_ref[k:k + 1, :], ps, 0), axis=0, keepdims=True)
        rows.append((start + rank_ref[k:k + 1, :]) * ROW_TILE)
    pos = jnp.concatenate(rows, axis=0)
    pos_ref[...] = pos
    src = lax.broadcasted_iota(I32, (tp, tp * ROW_TILE), 0)
    dst = lax.broadcasted_iota(I32, (tp, tp * ROW_TILE), 1)
    spread = (lax.shift_right_logical(dst, int(math.log2(ROW_TILE))) == src).astype(BF16)
    out = lax.broadcasted_iota(I32, (TOP_K, tp * ROW_TILE), 1) & (ROW_TILE - 1)
    for byte in range(3):
        part = jnp.bitwise_and(lax.shift_right_logical(pos, 8 * byte), 255).astype(F32).astype(BF16)
        wide = jnp.dot(part, spread, preferred_element_type=F32).astype(I32)
        out = out + lax.shift_left(wide, 8 * byte)
    rows_ref[...] = out


def _pos_call(e_idx, rank, pstarts):
    n_tok = e_idx.shape[1]
    tp = POS_TILE
    k_spec = pl.BlockSpec((TOP_K, tp), lambda i: (0, i))
    return pl.pallas_call(
        _pos_kernel,
        out_shape=(jax.ShapeDtypeStruct((TOP_K, n_tok), I32),
                   jax.ShapeDtypeStruct((TOP_K, n_tok * ROW_TILE), I32)),
        grid=(n_tok // tp,),
        in_specs=[k_spec, k_spec, pl.BlockSpec((N_EXPERTS, 1), lambda i: (0, 0))],
        out_specs=(k_spec, pl.BlockSpec((TOP_K, tp * ROW_TILE), lambda i: (0, i))),
        compiler_params=pltpu.CompilerParams(dimension_semantics=("arbitrary",)),
        name="positions",
    )(e_idx, rank, pstarts)


def _sc_row_scatter(rows, row_idx, n_out_rows):
    n_k = row_idx.shape[0]
    windows = row_idx.shape[1] // SC_GATHER_WINDOW
    mesh = plsc.VectorSubcoreMesh(core_axis_name="c", subcore_axis_name="s")

    @functools.partial(
        pl.kernel, out_type=jax.ShapeDtypeStruct((n_out_rows, LANES), rows.dtype),
        mesh=mesh, compiler_params=pltpu.CompilerParams(use_tc_tiling_on_sc=True), name="sc_row_scatter")
    def scatter(rows_hbm, idx_hbm, out_hbm):
        def body(rows_vmem, idx_vmem):
            pltpu.sync_copy(rows_vmem, out_hbm.at[idx_vmem.at[0]])
        pltpu.emit_pipeline(
            body, grid=(windows, n_k),
            in_specs=[pl.BlockSpec((SC_GATHER_WINDOW, LANES), lambda i, k: (i, 0)),
                      pl.BlockSpec((1, SC_GATHER_WINDOW), lambda i, k: (k, i))],
            out_specs=[],
            core_axis_name=("c", "s"), dimension_semantics=(pltpu.PARALLEL, pltpu.PARALLEL),
        )(rows_hbm, idx_hbm)
    return scatter(rows, row_idx)


def _pad_fill_kernel(cnt_ref, ps_ref, xs_in_ref, xs_ref, zbuf, zsem):
    del xs_in_ref

    def pad_copies(e, fn):
        cnt = cnt_ref[e]
        pad = jnp.bitwise_and(-cnt, MOE_BLOCK - 1)
        off = ps_ref[e] + cnt
        for b in range(int(math.log2(MOE_BLOCK))):
            size = 1 << b
            hit = jnp.bitwise_and(pad, size)

            @pl.when(hit != 0)
            def _(off=off, size=size):
                fn(pltpu.make_async_copy(
                    zbuf.at[pl.ds(0, size * ROW_TILE)],
                    xs_ref.at[pl.ds(pl.multiple_of(off * ROW_TILE, ROW_TILE), size * ROW_TILE)], zsem))
            off = off + hit

    zbuf[...] = jnp.zeros_like(zbuf)

    def start(e, carry):
        pad_copies(e, lambda cp: cp.start())
        return carry

    def wait(e, carry):
        pad_copies(e, lambda cp: cp.wait())
        return carry
    lax.fori_loop(0, N_EXPERTS, start, 0)
    lax.fori_loop(0, N_EXPERTS, wait, 0)


def _pad_fill_call(counts, pstarts, xs):
    grid_spec = pltpu.PrefetchScalarGridSpec(
        num_scalar_prefetch=2,
        grid=(1,),
        in_specs=[pl.BlockSpec(memory_space=pl.ANY)],
        out_specs=pl.BlockSpec(memory_space=pl.ANY),
        scratch_shapes=[
            pltpu.VMEM((MOE_BLOCK // 2 * ROW_TILE, LANES), U32),
            pltpu.SemaphoreType.DMA(()),
        ],
    )
    return pl.pallas_call(
        _pad_fill_kernel,
        out_shape=jax.ShapeDtypeStruct(xs.shape, xs.dtype),
        grid_spec=grid_spec,
        input_output_aliases={2: 0},
        compiler_params=pltpu.CompilerParams(dimension_semantics=("arbitrary",)),
        name="pad_fill",
    )(counts, pstarts, xs)


def _experts_kernel(ps_ref, nb_ref, xs_ref, wg_ref, wu_ref, wd_ref, y_ref,
                    xbuf, ybuf, wg_b, wu_b, wd_b, sem_in, sem_out):
    e = pl.program_id(0)
    nb = nb_ref[e]
    blk_rows = MOE_BLOCK * ROW_TILE
    g0 = ps_ref[e] // MOE_BLOCK
    n_total = (ps_ref[N_EXPERTS - 1] // MOE_BLOCK) + nb_ref[N_EXPERTS - 1]
    ahead = EXPERT_SLOTS - 1

    def rows_of(g):
        return pl.ds(pl.multiple_of(g * blk_rows, blk_rows), blk_rows)

    def slot_of(g):
        return jnp.bitwise_and(g, EXPERT_SLOTS - 1)

    def in_copy(g):
        s = slot_of(g)
        return pltpu.make_async_copy(xs_ref.at[rows_of(g)], xbuf.at[s], sem_in.at[s])

    def out_copy(g):
        s = slot_of(g)
        return pltpu.make_async_copy(ybuf.at[s], y_ref.at[rows_of(g)], sem_out.at[s])

    @pl.when(e == 0)
    def _prime():
        for u in range(ahead):
            @pl.when(u < n_total)
            def _(u=u):
                in_copy(u).start()

    @pl.when(nb > 0)
    def _run():
        wg_b[...] = wg_ref[0].astype(BF16)
        wu_b[...] = wu_ref[0].astype(BF16)
        wd_b[...] = wd_ref[0].astype(BF16)

        def block(j, carry):
            g = g0 + j
            s = slot_of(g)
            in_copy(g).wait()

            @pl.when(g + ahead < n_total)
            def _prefetch():
                in_copy(g + ahead).start()

            @pl.when(g >= EXPERT_SLOTS)
            def _reclaim():
                out_copy(g - EXPERT_SLOTS).wait()

            xb = _load_row_tiles(xbuf.at[s], MOE_BLOCK).astype(BF16)
            hg = jnp.dot(xb, wg_b[...], preferred_element_type=F32)
            hu = jnp.dot(xb, wu_b[...], preferred_element_type=F32)
            hh = (hg * jax.nn.sigmoid(hg) * hu).astype(BF16)
            _store_row_tiles(ybuf.at[s], jnp.dot(hh, wd_b[...], preferred_element_type=F32))
            out_copy(g).start()
            return carry
        lax.fori_loop(0, nb, block, 0)

    @pl.when(e == N_EXPERTS - 1)
    def _drain():
        for u in range(EXPERT_SLOTS):
            @pl.when(u < n_total)
            def _(u=u):
                out_copy(u).wait()


def _experts_call(pstarts, n_blk, xs, w_gate, w_up, w_down):
    d = D_MODEL

    def wsel(e, ps, nb):
        return (e, 0, 0)
    grid_spec = pltpu.PrefetchScalarGridSpec(
        num_scalar_prefetch=2,
        grid=(N_EXPERTS,),
        in_specs=[
            pl.BlockSpec(memory_space=pl.ANY),
            pl.BlockSpec((1, d, EXPERT_DIM), wsel),
            pl.BlockSpec((1, d, EXPERT_DIM), wsel),
            pl.BlockSpec((1, EXPERT_DIM, d), wsel),
        ],
        out_specs=pl.BlockSpec(memory_space=pl.ANY),
        scratch_shapes=[
            pltpu.VMEM((EXPERT_SLOTS, MOE_BLOCK * ROW_TILE, LANES), U32),
            pltpu.VMEM((EXPERT_SLOTS, MOE_BLOCK * ROW_TILE, LANES), U32),
            pltpu.VMEM((d, EXPERT_DIM), BF16),
            pltpu.VMEM((d, EXPERT_DIM), BF16),
            pltpu.VMEM((EXPERT_DIM, d), BF16),
            pltpu.SemaphoreType.DMA((EXPERT_SLOTS,)),
            pltpu.SemaphoreType.DMA((EXPERT_SLOTS,)),
        ],
    )
    return pl.pallas_call(
        _experts_kernel,
        out_shape=jax.ShapeDtypeStruct(xs.shape, U32),
        grid_spec=grid_spec,
        compiler_params=pltpu.CompilerParams(
            dimension_semantics=("arbitrary",), vmem_limit_bytes=VMEM_LIMIT_BYTES),
        name="experts",
    )(pstarts, n_blk, xs, w_gate, w_up, w_down)


def _sc_row_gather(table, row_idx, chunk, n_chunks):
    n_k = row_idx.shape[0]
    windows = row_idx.shape[1] // n_chunks // SC_GATHER_WINDOW
    mesh = plsc.VectorSubcoreMesh(core_axis_name="c", subcore_axis_name="s")

    @functools.partial(
        pl.kernel, out_type=jax.ShapeDtypeStruct((n_k * windows * SC_GATHER_WINDOW, LANES), table.dtype),
        mesh=mesh, compiler_params=pltpu.CompilerParams(use_tc_tiling_on_sc=True), name="sc_row_gather")
    def gather(table_hbm, idx_hbm, out_hbm):
        def body(idx_vmem, out_vmem):
            pltpu.sync_copy(table_hbm.at[idx_vmem.at[0]], out_vmem)
        pltpu.emit_pipeline(
            body, grid=(n_k, windows),
            in_specs=[pl.BlockSpec((1, SC_GATHER_WINDOW), lambda k, i: (k, chunk * windows + i))],
            out_specs=[pl.BlockSpec((SC_GATHER_WINDOW, LANES), lambda k, i: (k * windows + i, 0))],
            core_axis_name=("c", "s"), dimension_semantics=(pltpu.PARALLEL, pltpu.PARALLEL),
        )(idx_hbm, out_hbm)
    return gather(table, row_idx)


def _combine_kernel(gate_ref, base_ref, *refs):
    yk_refs = refs[:TOP_K]
    g_ref, b_ref = refs[TOP_K:TOP_K + 2]
    out_ref = refs[-1]
    tt = base_ref.shape[0]
    acc = base_ref[...]
    gates = gate_ref[...]
    for k in range(TOP_K):
        acc = acc + gates[:, k:k + 1] * _load_row_tiles(yk_refs[k], tt)
    out_ref[...] = _layer_norm(acc, g_ref[...], b_ref[...])


def _combine_call(chunk, n_chunks, gates_t, base, y_chunk, ln_g, ln_b, carry):
    n_tok, d = base.shape
    tt = COMBINE_TILE
    tiles = n_tok // n_chunks // tt
    first = chunk * tiles
    full = lambda a: pl.BlockSpec(a.shape, lambda i: (0,) * a.ndim)
    y_specs = [pl.BlockSpec((tt * ROW_TILE, LANES), lambda i, k=k: (k * tiles + i, 0)) for k in range(TOP_K)]
    in_specs = [
        pl.BlockSpec((tt, TOP_K), lambda i: (first + i, 0)),
        pl.BlockSpec((tt, d), lambda i: (first + i, 0)),
        *y_specs,
        full(ln_g), full(ln_b),
    ]
    operands = [gates_t, base] + [y_chunk] * TOP_K + [ln_g, ln_b]
    aliases = {}
    if carry is not None:
        in_specs.append(pl.BlockSpec(memory_space=pl.ANY))
        operands.append(carry)
        aliases = {len(operands) - 1: 0}
    return pl.pallas_call(
        _combine_kernel,
        out_shape=jax.ShapeDtypeStruct((n_tok, d), F32),
        grid=(tiles,),
        in_specs=in_specs,
        out_specs=pl.BlockSpec((tt, d), lambda i: (first + i, 0)),
        input_output_aliases=aliases,
        compiler_params=pltpu.CompilerParams(
            dimension_semantics=("arbitrary",), vmem_limit_bytes=VMEM_LIMIT_BYTES),
        name="combine",
    )(*operands)


def _group_weights(w_in):
    blocks = w_in.reshape(D_MODEL, 12, GROUP_WIDTH)
    scale = HEAD_DIM ** -0.5
    groups = []
    for g in range(N_ATTN_GROUPS):
        groups.append(jnp.concatenate(
            [blocks[:, g] * scale, blocks[:, N_ATTN_GROUPS + g], blocks[:, 2 * N_ATTN_GROUPS + g]], axis=1))
    groups.append(jnp.concatenate([blocks[:, 9], blocks[:, 10], blocks[:, 11]], axis=1))
    return jnp.stack(groups).astype(BF16)


def kernel(x, w_in, conv_w, w_o, ln1_g, ln1_b, rel_bias, w_router, router_bias, exp_w_gate, exp_w_up,
           exp_w_down, sh_w_gate, sh_w_up, sh_w_down, ln2_g, ln2_b):
    batch, seq, d = x.shape
    assert d == D_MODEL and seq == DILATED_BRANCHES[-1][0] and w_in.shape[0] == DEPTH
    n_tok = batch * seq
    assert n_tok % POST_TILE == 0 and n_tok % DISPATCH_TILE == 0 and n_tok % COMBINE_TILE == 0

    mix = _mix_call(x, _group_weights(w_in[0]), _bias_table(rel_bias), conv_w[0])

    wr = w_router[0]
    wr_hi = wr.astype(BF16)
    wr_lo = (wr - wr_hi.astype(F32)).astype(BF16)
    x2d = x.reshape(n_tok, d)
    base, x1p, e_idx, rank, gates, counts = _post_call(
        mix.reshape(n_tok, d), x2d, w_o[0].astype(BF16), ln1_g, ln1_b,
        wr_hi.T, wr_lo.T, router_bias[0].reshape(N_EXPERTS, 1),
        sh_w_gate[0].astype(BF16), sh_w_up[0].astype(BF16), sh_w_down[0].astype(BF16))

    counts = counts[:, 0]
    padded = (counts + MOE_BLOCK - 1) // MOE_BLOCK * MOE_BLOCK
    pends = jnp.cumsum(padded)
    pstarts = (pends - padded).astype(I32)
    n_blocks = n_tok * TOP_K // MOE_BLOCK + N_EXPERTS
    n_blk = (padded // MOE_BLOCK).astype(I32)
    pos, row_idx = _pos_call(e_idx, rank, pstarts.reshape(N_EXPERTS, 1))

    xs = _sc_row_scatter(x1p, row_idx, n_blocks * MOE_BLOCK * ROW_TILE)
    xs = _pad_fill_call(counts, pstarts, xs)
    y = _experts_call(pstarts, n_blk, xs, exp_w_gate[0], exp_w_up[0], exp_w_down[0])

    gates_t = gates.T
    out = None
    for c in range(COMBINE_CHUNKS):
        y_chunk = _sc_row_gather(y, row_idx, c, COMBINE_CHUNKS)
        out = _combine_call(c, COMBINE_CHUNKS, gates_t, base, y_chunk, ln2_g, ln2_b, out)
    return out.reshape(batch, seq, d)
```

```python
import functools
import math

import numpy as np
import jax
import jax.numpy as jnp
from jax import lax
from jax.experimental import pallas as pl
from jax.experimental.pallas import tpu as pltpu
from jax.experimental.pallas import tpu_sc as plsc

F32 = jnp.float32
BF16 = jnp.bfloat16
I32 = jnp.int32
U32 = jnp.uint32

D_MODEL = 1024
HEAD_DIM = 64
N_ATTN_HEADS = 12
ATTN_WIDTH = N_ATTN_HEADS * HEAD_DIM
CONV_WIDTH = D_MODEL - ATTN_WIDTH
CONV_K = 3
DILATED_BRANCHES = ((128, 1), (512, 4), (2048, 16))
ATTN_BLOCK = 128
N_REL_BUCKETS = 32
REL_MAX_DISTANCE = 2048
N_EXPERTS = 256
TOP_K = 8
N_EXPERT_GROUPS = 8
TOPK_GROUPS = 4
GROUP_SIZE = N_EXPERTS // N_EXPERT_GROUPS
EXPERT_DIM = 256
SHARED_DIM = 256
ROUTED_SCALE = 2.5
DEPTH = 1
DEEPNORM_ALPHA = (2.0 * DEPTH) ** 0.25
LN_EPS = 1e-5

LANES = 128
GROUP_WIDTH = 2 * LANES
HEADS_PER_GROUP = GROUP_WIDTH // HEAD_DIM
N_GROUPS = D_MODEL // GROUP_WIDTH
N_ATTN_GROUPS = ATTN_WIDTH // GROUP_WIDTH
VMEM_LIMIT_BYTES = 56 * 1024 * 1024

MASK_VALUE = -1e30

ROW_CHUNK = 2048
ATTN_UNROLL = 8
CONV_PAD = 8
ROW_TILE = 4
POS_TILE = 512
POST_TILE = 512
MOE_BLOCK = 512
EXPERT_SLOTS = 4
DISPATCH_TILE = 256
COMBINE_TILE = 512
SC_GATHER_WINDOW = 128
COMBINE_CHUNKS = 8


def _t5_bucket(dist):
    max_exact = N_REL_BUCKETS // 2
    dist = np.asarray(dist)
    log_part = np.log(np.maximum(dist, 1) / max_exact) / math.log(REL_MAX_DISTANCE / max_exact)
    large = max_exact + (log_part * (N_REL_BUCKETS - max_exact)).astype(np.int32)
    large = np.minimum(large, N_REL_BUCKETS - 1)
    return np.where(dist < max_exact, dist, large).astype(np.int32)


def _branch_bucket_and_band(window, dilation):
    w_sub = window // dilation
    qi = np.arange(ATTN_BLOCK)[:, None]
    ki = np.arange(2 * ATTN_BLOCK)[None, :]
    steps = qi + ATTN_BLOCK - ki
    band = (steps >= 0) & (steps <= w_sub)
    bucket = _t5_bucket(np.clip(steps, 0, w_sub) * dilation)
    return bucket, band


def _bias_table(rel_bias):
    tabs = []
    for window, dilation in DILATED_BRANCHES:
        bucket, band = _branch_bucket_and_band(window, dilation)
        onehot = (bucket[:, :, None] == np.arange(N_REL_BUCKETS)).astype(np.float32)
        b = jnp.einsum("qkb,bh->qkh", onehot, rel_bias.astype(F32), precision=lax.Precision.HIGHEST)
        b = jnp.where(band[:, :, None], b, MASK_VALUE)
        tabs.append(b.transpose(2, 0, 1))
    t = jnp.stack(tabs, axis=1)
    t = t.reshape(N_ATTN_GROUPS, HEADS_PER_GROUP, len(DILATED_BRANCHES), ATTN_BLOCK, 2 * ATTN_BLOCK)
    t = t.transpose(0, 2, 1, 3, 4)
    return t.reshape(N_ATTN_GROUPS, len(DILATED_BRANCHES), HEADS_PER_GROUP * ATTN_BLOCK, 2 * ATTN_BLOCK)


def _lane_head():
    return lax.shift_right_logical(lax.broadcasted_iota(I32, (ATTN_BLOCK, GROUP_WIDTH), 1), 6)


def _attn_steps(its, bi, dilation, first, q_s, k_s, v_s, bias_ref, o_s, m_s, l_s):
    logd = int(math.log2(dilation))

    def rows(st):
        return pl.ds(st, ATTN_BLOCK) if dilation == 1 else pl.ds(st, ATTN_BLOCK, stride=dilation)

    def ld(ref, st):
        return jnp.concatenate([ref[0, rows(st), :], ref[1, rows(st), :]], axis=1)

    lane_head = _lane_head()
    bias = bias_ref[0, bi, :, ATTN_BLOCK:] if first else bias_ref[0, bi]
    starts, operands = [], []
    for it in its:
        r = jnp.bitwise_and(it, dilation - 1)
        n = lax.shift_right_logical(it, logd)
        start = r + (dilation * ATTN_BLOCK) * n
        qf = ld(q_s, start)
        if first:
            kk = ld(k_s, start).astype(BF16)
            vv = ld(v_s, start).astype(BF16)
        else:
            prev = start - dilation * ATTN_BLOCK
            kk = jnp.concatenate([ld(k_s, prev), ld(k_s, start)], axis=0).astype(BF16)
            vv = jnp.concatenate([ld(v_s, prev), ld(v_s, start)], axis=0).astype(BF16)
        starts.append(start)
        operands.append((qf, kk, vv))

    results = []
    for qf, kk, vv in operands:
        q4 = jnp.concatenate(
            [jnp.where(lane_head == h, qf, 0.0) for h in range(HEADS_PER_GROUP)], axis=0).astype(BF16)
        s = lax.dot_general(q4, kk, (((1,), (1,)), ((), ())), preferred_element_type=F32) + bias
        m = jnp.max(s, axis=-1, keepdims=True)
        p = jnp.exp(s - m)
        l = jnp.sum(p, axis=-1, keepdims=True)
        pv = jnp.dot(p.astype(BF16), vv, preferred_element_type=F32)
        o = jnp.zeros((ATTN_BLOCK, GROUP_WIDTH), F32)
        mb = jnp.zeros((ATTN_BLOCK, GROUP_WIDTH), F32)
        lb = jnp.zeros((ATTN_BLOCK, GROUP_WIDTH), F32)
        for h in range(HEADS_PER_GROUP):
            sel = lane_head == h
            hr = slice(h * ATTN_BLOCK, (h + 1) * ATTN_BLOCK)
            o = jnp.where(sel, pv[hr], o)
            mb = jnp.where(sel, m[hr], mb)
            lb = jnp.where(sel, l[hr], lb)
        results.append((o, mb, lb))

    for start, (o, mb, lb) in zip(starts, results):
        for sl in range(2):
            lanes = slice(sl * LANES, (sl + 1) * LANES)
            o_s[bi, sl, rows(start), :] = o[:, lanes]
            m_s[bi, sl, rows(start), :] = mb[:, lanes]
            l_s[bi, sl, rows(start), :] = lb[:, lanes]


def _attn_range(lo, hi, **kw):
    n_groups = (hi - lo) // ATTN_UNROLL

    def body(j, carry):
        base = lo + j * ATTN_UNROLL
        _attn_steps([base + u for u in range(ATTN_UNROLL)], **kw)
        return carry
    if n_groups > 0:
        lax.fori_loop(0, n_groups, body, 0)
    tail = list(range(lo + n_groups * ATTN_UNROLL, hi))
    if tail:
        _attn_steps([jnp.int32(t) for t in tail], **kw)


def _mix_kernel(x_ref, w_ref, bias_ref, cw_ref, out_ref, q_s, k_s, v_s, o_s, m_s, l_s):
    g = pl.program_id(1)
    seq = x_ref.shape[1]
    n_chunks = seq // ROW_CHUNK

    def proj(c, carry):
        rows = pl.ds(pl.multiple_of(c * ROW_CHUNK, ROW_CHUNK), ROW_CHUNK)
        res = jnp.dot(x_ref[0, rows, :].astype(BF16), w_ref[0], preferred_element_type=F32)
        for j, dst in enumerate((q_s, k_s, v_s)):
            for sl in range(2):
                lo = j * GROUP_WIDTH + sl * LANES
                dst[sl, rows, :] = res[:, lo:lo + LANES]
        return carry
    lax.fori_loop(0, n_chunks, proj, 0)

    @pl.when(g < N_ATTN_GROUPS)
    def _attention():
        for bi, (window, dilation) in enumerate(DILATED_BRANCHES):
            n_blocks = seq // dilation // ATTN_BLOCK
            n_steps = n_blocks * dilation
            kw = dict(bi=bi, dilation=dilation, q_s=q_s, k_s=k_s, v_s=v_s,
                      bias_ref=bias_ref, o_s=o_s, m_s=m_s, l_s=l_s)
            _attn_range(0, dilation, first=True, **kw)
            _attn_range(dilation, n_steps, first=False, **kw)

        def combine(c, carry):
            rows = pl.ds(pl.multiple_of(c * ROW_CHUNK, ROW_CHUNK), ROW_CHUNK)
            for sl in range(2):
                ms = [m_s[bi, sl, rows, :] for bi in range(len(DILATED_BRANCHES))]
                mx = jnp.maximum(jnp.maximum(ms[0], ms[1]), ms[2])
                es = [jnp.exp(v - mx) for v in ms]
                den = es[0] * l_s[0, sl, rows, :] + es[1] * l_s[1, sl, rows, :] + es[2] * l_s[2, sl, rows, :]
                num = es[0] * o_s[0, sl, rows, :] + es[1] * o_s[1, sl, rows, :] + es[2] * o_s[2, sl, rows, :]
                out_ref[0, rows, sl * LANES:(sl + 1) * LANES] = (num / den).astype(BF16)
            return carry
        lax.fori_loop(0, n_chunks, combine, 0)

    @pl.when(g == N_ATTN_GROUPS)
    def _short_conv():
        pad = CONV_PAD
        u_s = o_s.at[0]
        for sl in range(2):
            u_s[sl, 0:pad, :] = jnp.zeros((pad, LANES), F32)
            for c in range(n_chunks):
                lo = c * ROW_CHUNK
                u_s[sl, pad + lo:pad + lo + ROW_CHUNK, :] = v_s[sl, lo:lo + ROW_CHUNK, :] * q_s[sl, lo:lo + ROW_CHUNK, :]
            w = [cw_ref[kk:kk + 1, sl * LANES:(sl + 1) * LANES] for kk in range(CONV_K)]
            for c in range(n_chunks):
                lo = c * ROW_CHUNK
                y = w[2] * u_s[sl, pad + lo:pad + lo + ROW_CHUNK, :]
                y = y + w[1] * u_s[sl, pad + lo - 1:pad + lo - 1 + ROW_CHUNK, :]
                y = y + w[0] * u_s[sl, pad + lo - 2:pad + lo - 2 + ROW_CHUNK, :]
                out_ref[0, lo:lo + ROW_CHUNK, sl * LANES:(sl + 1) * LANES] = (
                    k_s[sl, lo:lo + ROW_CHUNK, :] * y).astype(BF16)


def _mix_call(x, w_groups, bias_tbl, conv_w):
    batch, seq, d = x.shape
    return pl.pallas_call(
        _mix_kernel,
        out_shape=jax.ShapeDtypeStruct((batch, seq, d), BF16),
        grid=(batch, N_GROUPS),
        in_specs=[
            pl.BlockSpec((1, seq, d), lambda b, g: (b, 0, 0)),
            pl.BlockSpec((1, d, 3 * GROUP_WIDTH), lambda b, g: (g, 0, 0)),
            pl.BlockSpec((1,) + bias_tbl.shape[1:], lambda b, g: (jnp.minimum(g, N_ATTN_GROUPS - 1), 0, 0, 0)),
            pl.BlockSpec(conv_w.shape, lambda b, g: (0, 0)),
        ],
        out_specs=pl.BlockSpec((1, seq, GROUP_WIDTH), lambda b, g: (b, 0, g)),
        scratch_shapes=[
            pltpu.VMEM((2, seq, LANES), F32),
            pltpu.VMEM((2, seq, LANES), F32),
            pltpu.VMEM((2, seq, LANES), F32),
            pltpu.VMEM((len(DILATED_BRANCHES), 2, seq + CONV_PAD, LANES), F32),
            pltpu.VMEM((len(DILATED_BRANCHES), 2, seq, LANES), F32),
            pltpu.VMEM((len(DILATED_BRANCHES), 2, seq, LANES), F32),
        ],
        compiler_params=pltpu.CompilerParams(
            dimension_semantics=("arbitrary", "arbitrary"), vmem_limit_bytes=VMEM_LIMIT_BYTES),
        name="mix",
    )(x, w_groups, bias_tbl, conv_w)


def _layer_norm(h, g, b):
    mu = jnp.mean(h, axis=-1, keepdims=True)
    c = h - mu
    var = jnp.mean(c * c, axis=-1, keepdims=True)
    return c * lax.rsqrt(var + LN_EPS) * g + b


def _bf16_bits(v):
    return lax.bitcast_convert_type(v.astype(BF16).astype(F32), U32)


def _store_row_tiles(ref, val):
    n = val.shape[0]
    for j in range(ROW_TILE):
        lo = _bf16_bits(val[:, (2 * j) * LANES:(2 * j + 1) * LANES])
        hi = _bf16_bits(val[:, (2 * j + 1) * LANES:(2 * j + 2) * LANES])
        ref[pl.ds(j, n, stride=ROW_TILE), :] = jnp.bitwise_or(lax.shift_right_logical(lo, jnp.uint32(16)), hi)


def _load_row_tiles(ref, n):
    pieces = []
    for j in range(ROW_TILE):
        w = ref[pl.ds(j, n, stride=ROW_TILE), :]
        pieces.append(lax.bitcast_convert_type(lax.shift_left(w, jnp.uint32(16)), F32))
        pieces.append(lax.bitcast_convert_type(jnp.bitwise_and(w, jnp.uint32(0xFFFF0000)), F32))
    return jnp.concatenate(pieces, axis=1)


def _first_argmax_rows(v, row_ids, n_rows):
    m = jnp.max(v, axis=0, keepdims=True)
    idx = jnp.min(jnp.where(v == m, row_ids, n_rows), axis=0, keepdims=True)
    return m, idx


def _post_kernel(mix_ref, x_ref, wo_ref, g_ref, b_ref, wrh_ref, wrl_ref, rb_ref, sg_ref, su_ref, sd_ref,
                 base_ref, x1p_ref, e_ref, rank_ref, gate_ref, cnt_ref, carry_s):
    i = pl.program_id(0)
    tm = x_ref.shape[0]

    @pl.when(i == 0)
    def _init():
        carry_s[...] = jnp.zeros_like(carry_s)

    h = DEEPNORM_ALPHA * x_ref[...] + jnp.dot(mix_ref[...], wo_ref[...], preferred_element_type=F32)
    x1 = _layer_norm(h, g_ref[...], b_ref[...])
    _store_row_tiles(x1p_ref, x1)

    x_hi = x1.astype(BF16)
    hg = jnp.dot(x_hi, sg_ref[...], preferred_element_type=F32)
    hu = jnp.dot(x_hi, su_ref[...], preferred_element_type=F32)
    hh = (hg * jax.nn.sigmoid(hg) * hu).astype(BF16)
    base_ref[...] = DEEPNORM_ALPHA * x1 + jnp.dot(hh, sd_ref[...], preferred_element_type=F32)

    x_lo = (x1 - x_hi.astype(F32)).astype(BF16)
    dn = (((1,), (1,)), ((), ()))
    logits = lax.dot_general(wrh_ref[...], x_hi, dn, preferred_element_type=F32)
    logits = logits + lax.dot_general(wrh_ref[...], x_lo, dn, preferred_element_type=F32)
    logits = logits + lax.dot_general(wrl_ref[...], x_hi, dn, preferred_element_type=F32)
    scores = jax.nn.sigmoid(logits)
    biased = scores + rb_ref[...]

    neg_inf = -jnp.inf
    sub_ids = lax.broadcasted_iota(I32, (GROUP_SIZE, tm), 0)
    gs_rows = []
    for gi in range(N_EXPERT_GROUPS):
        bg = biased[gi * GROUP_SIZE:(gi + 1) * GROUP_SIZE]
        m1, i1 = _first_argmax_rows(bg, sub_ids, GROUP_SIZE)
        m2 = jnp.max(jnp.where(sub_ids == i1, neg_inf, bg), axis=0, keepdims=True)
        gs_rows.append(m1 + m2)
    gscore = jnp.concatenate(gs_rows, axis=0)
    grp_ids = lax.broadcasted_iota(I32, (N_EXPERT_GROUPS, tm), 0)
    keep_g = jnp.zeros((N_EXPERT_GROUPS, tm), jnp.bool_)
    for _ in range(TOPK_GROUPS):
        _, gi1 = _first_argmax_rows(gscore, grp_ids, N_EXPERT_GROUPS)
        hit = grp_ids == gi1
        keep_g = jnp.logical_or(keep_g, hit)
        gscore = jnp.where(hit, neg_inf, gscore)
    keep_f = keep_g.astype(F32)
    keep_e = jnp.concatenate(
        [jnp.broadcast_to(keep_f[gi:gi + 1], (GROUP_SIZE, tm)) for gi in range(N_EXPERT_GROUPS)], axis=0)
    masked = jnp.where(keep_e > 0.5, biased, neg_inf)

    exp_ids = lax.broadcasted_iota(I32, (N_EXPERTS, tm), 0)
    sel = jnp.zeros((N_EXPERTS, tm), F32)
    e_rows, g_rows = [], []
    for _ in range(TOP_K):
        _, e1 = _first_argmax_rows(masked, exp_ids, N_EXPERTS)
        hit = exp_ids == e1
        sel = jnp.where(hit, 1.0, sel)
        g_rows.append(jnp.sum(jnp.where(hit, scores, 0.0), axis=0, keepdims=True))
        e_rows.append(e1)
        masked = jnp.where(hit, neg_inf, masked)
    gates = jnp.concatenate(g_rows, axis=0)
    gates = gates / jnp.sum(gates, axis=0, keepdims=True) * ROUTED_SCALE
    gate_ref[...] = gates
    e_ref[...] = jnp.concatenate(e_rows, axis=0)

    t_row = lax.broadcasted_iota(I32, (tm, tm), 0)
    t_col = lax.broadcasted_iota(I32, (tm, tm), 1)
    upper = (t_row < t_col).astype(BF16)
    excl = jnp.dot(sel.astype(BF16), upper, preferred_element_type=F32)
    base = carry_s[...] + excl
    r_rows = [jnp.sum(jnp.where(exp_ids == e1, base, 0.0), axis=0, keepdims=True) for e1 in e_rows]
    rank_ref[...] = jnp.concatenate(r_rows, axis=0).astype(I32)
    carry = carry_s[...] + jnp.sum(sel, axis=1, keepdims=True)
    carry_s[...] = carry
    cnt_ref[...] = carry.astype(I32)


def _post_call(mix2d, x2d, wo, ln_g, ln_b, wr_hi, wr_lo, rb, sg, su, sd):
    n_tok, d = x2d.shape
    tm = POST_TILE
    tok_spec = pl.BlockSpec((tm, d), lambda i: (i, 0))
    k_spec = pl.BlockSpec((TOP_K, tm), lambda i: (0, i))
    full = lambda a: pl.BlockSpec(a.shape, lambda i: (0,) * a.ndim)
    return pl.pallas_call(
        _post_kernel,
        out_shape=(
            jax.ShapeDtypeStruct((n_tok, d), F32),
            jax.ShapeDtypeStruct((n_tok * ROW_TILE, LANES), U32),
            jax.ShapeDtypeStruct((TOP_K, n_tok), I32),
            jax.ShapeDtypeStruct((TOP_K, n_tok), I32),
            jax.ShapeDtypeStruct((TOP_K, n_tok), F32),
            jax.ShapeDtypeStruct((N_EXPERTS, 1), I32),
        ),
        grid=(n_tok // tm,),
        in_specs=[tok_spec, tok_spec, full(wo), full(ln_g), full(ln_b), full(wr_hi), full(wr_lo), full(rb),
                  full(sg), full(su), full(sd)],
        out_specs=(tok_spec, pl.BlockSpec((tm * ROW_TILE, LANES), lambda i: (i, 0)), k_spec, k_spec, k_spec,
                   pl.BlockSpec((N_EXPERTS, 1), lambda i: (0, 0))),
        scratch_shapes=[pltpu.VMEM((N_EXPERTS, 1), F32)],
        compiler_params=pltpu.CompilerParams(
            dimension_semantics=("arbitrary",), vmem_limit_bytes=VMEM_LIMIT_BYTES),
        name="post",
    )(mix2d, x2d, wo, ln_g, ln_b, wr_hi, wr_lo, rb, sg, su, sd)


def _pos_kernel(e_ref, rank_ref, ps_ref, pos_ref, rows_ref):
    tp = e_ref.shape[1]
    exp_ids = lax.broadcasted_iota(I32, (N_EXPERTS, tp), 0)
    ps = ps_ref[...]
    rows = []
    for k in range(TOP_K):
        start = jnp.sum(jnp.where(exp_ids == e_ref[k:k + 1, :], ps, 0), axis=0, keepdims=True)
        rows.append((start + rank_ref[k:k + 1, :]) * ROW_TILE)
    pos = jnp.concatenate(rows, axis=0)
    pos_ref[...] = pos
    src = lax.broadcasted_iota(I32, (tp, tp * ROW_TILE), 0)
    dst = lax.broadcasted_iota(I32, (tp, tp * ROW_TILE), 1)
    spread = (lax.shift_right_logical(dst, int(math.log2(ROW_TILE))) == src).astype(BF16)
    out = lax.broadcasted_iota(I32, (TOP_K, tp * ROW_TILE), 1) & (ROW_TILE - 1)
    for byte in range(3):
        part = jnp.bitwise_and(lax.shift_right_logical(pos, 8 * byte), 255).astype(F32).astype(BF16)
        wide = jnp.dot(part, spread, preferred_element_type=F32).astype(I32)
        out = out + lax.shift_left(wide, 8 * byte)
    rows_ref[...] = out


def _pos_call(e_idx, rank, pstarts):
    n_tok = e_idx.shape[1]
    tp = POS_TILE
    k_spec = pl.BlockSpec((TOP_K, tp), lambda i: (0, i))
    return pl.pallas_call(
        _pos_kernel,
        out_shape=(jax.ShapeDtypeStruct((TOP_K, n_tok), I32),
                   jax.ShapeDtypeStruct((TOP_K, n_tok * ROW_TILE), I32)),
        grid=(n_tok // tp,),
        in_specs=[k_spec, k_spec, pl.BlockSpec((N_EXPERTS, 1), lambda i: (0, 0))],
        out_specs=(k_spec, pl.BlockSpec((TOP_K, tp * ROW_TILE), lambda i: (0, i))),
        compiler_params=pltpu.CompilerParams(dimension_semantics=("arbitrary",)),
        name="positions",
    )(e_idx, rank, pstarts)


def _sc_row_scatter(rows, row_idx, n_out_rows):
    n_k = row_idx.shape[0]
    windows = row_idx.shape[1] // SC_GATHER_WINDOW
    mesh = plsc.VectorSubcoreMesh(core_axis_name="c", subcore_axis_name="s")

    @functools.partial(
        pl.kernel, out_type=jax.ShapeDtypeStruct((n_out_rows, LANES), rows.dtype),
        mesh=mesh, compiler_params=pltpu.CompilerParams(use_tc_tiling_on_sc=True), name="sc_row_scatter")
    def scatter(rows_hbm, idx_hbm, out_hbm):
        def body(rows_vmem, idx_vmem):
            pltpu.sync_copy(rows_vmem, out_hbm.at[idx_vmem.at[0]])
        pltpu.emit_pipeline(
            body, grid=(windows, n_k),
            in_specs=[pl.BlockSpec((SC_GATHER_WINDOW, LANES), lambda i, k: (i, 0)),
                      pl.BlockSpec((1, SC_GATHER_WINDOW), lambda i, k: (k, i))],
            out_specs=[],
            core_axis_name=("c", "s"), dimension_semantics=(pltpu.PARALLEL, pltpu.PARALLEL),
        )(rows_hbm, idx_hbm)
    return scatter(rows, row_idx)


def _pad_fill_kernel(cnt_ref, ps_ref, xs_in_ref, xs_ref, zbuf, zsem):
    del xs_in_ref

    def pad_copies(e, fn):
        cnt = cnt_ref[e]
        pad = jnp.bitwise_and(-cnt, MOE_BLOCK - 1)
        off = ps_ref[e] + cnt
        for b in range(int(math.log2(MOE_BLOCK))):
            size = 1 << b
            hit = jnp.bitwise_and(pad, size)

            @pl.when(hit != 0)
            def _(off=off, size=size):
                fn(pltpu.make_async_copy(
                    zbuf.at[pl.ds(0, size * ROW_TILE)],
                    xs_ref.at[pl.ds(pl.multiple_of(off * ROW_TILE, ROW_TILE), size * ROW_TILE)], zsem))
            off = off + hit

    zbuf[...] = jnp.zeros_like(zbuf)

    def start(e, carry):
        pad_copies(e, lambda cp: cp.start())
        return carry

    def wait(e, carry):
        pad_copies(e, lambda cp: cp.wait())
        return carry
    lax.fori_loop(0, N_EXPERTS, start, 0)
    lax.fori_loop(0, N_EXPERTS, wait, 0)


def _pad_fill_call(counts, pstarts, xs):
    grid_spec = pltpu.PrefetchScalarGridSpec(
        num_scalar_prefetch=2,
        grid=(1,),
        in_specs=[pl.BlockSpec(memory_space=pl.ANY)],
        out_specs=pl.BlockSpec(memory_space=pl.ANY),
        scratch_shapes=[
            pltpu.VMEM((MOE_BLOCK // 2 * ROW_TILE, LANES), U32),
            pltpu.SemaphoreType.DMA(()),
        ],
    )
    return pl.pallas_call(
        _pad_fill_kernel,
        out_shape=jax.ShapeDtypeStruct(xs.shape, xs.dtype),
        grid_spec=grid_spec,
        input_output_aliases={2: 0},
        compiler_params=pltpu.CompilerParams(dimension_semantics=("arbitrary",)),
        name="pad_fill",
    )(counts, pstarts, xs)


def _experts_kernel(ps_ref, nb_ref, xs_ref, wg_ref, wu_ref, wd_ref, y_ref,
                    xbuf, ybuf, wg_b, wu_b, wd_b, sem_in, sem_out):
    e = pl.program_id(0)
    nb = nb_ref[e]
    blk_rows = MOE_BLOCK * ROW_TILE
    g0 = ps_ref[e] // MOE_BLOCK
    n_total = (ps_ref[N_EXPERTS - 1] // MOE_BLOCK) + nb_ref[N_EXPERTS - 1]
    ahead = EXPERT_SLOTS - 1

    def rows_of(g):
        return pl.ds(pl.multiple_of(g * blk_rows, blk_rows), blk_rows)

    def slot_of(g):
        return jnp.bitwise_and(g, EXPERT_SLOTS - 1)

    def in_copy(g):
        s = slot_of(g)
        return pltpu.make_async_copy(xs_ref.at[rows_of(g)], xbuf.at[s], sem_in.at[s])

    def out_copy(g):
        s = slot_of(g)
        return pltpu.make_async_copy(ybuf.at[s], y_ref.at[rows_of(g)], sem_out.at[s])

    @pl.when(e == 0)
    def _prime():
        for u in range(ahead):
            @pl.when(u < n_total)
            def _(u=u):
                in_copy(u).start()

    @pl.when(nb > 0)
    def _run():
        wg_b[...] = wg_ref[0].astype(BF16)
        wu_b[...] = wu_ref[0].astype(BF16)
        wd_b[...] = wd_ref[0].astype(BF16)

        def block(j, carry):
            g = g0 + j
            s = slot_of(g)
            in_copy(g).wait()

            @pl.when(g + ahead < n_total)
            def _prefetch():
                in_copy(g + ahead).start()

            @pl.when(g >= EXPERT_SLOTS)
            def _reclaim():
                out_copy(g - EXPERT_SLOTS).wait()

            xb = _load_row_tiles(xbuf.at[s], MOE_BLOCK).astype(BF16)
            hg = jnp.dot(xb, wg_b[...], preferred_element_type=F32)
            hu = jnp.dot(xb, wu_b[...], preferred_element_type=F32)
            hh = (hg * jax.nn.sigmoid(hg) * hu).astype(BF16)
            _store_row_tiles(ybuf.at[s], jnp.dot(hh, wd_b[...], preferred_element_type=F32))
            out_copy(g).start()
            return carry
        lax.fori_loop(0, nb, block, 0)

    @pl.when(e == N_EXPERTS - 1)
    def _drain():
        for u in range(EXPERT_SLOTS):
            @pl.when(u < n_total)
            def _(u=u):
                out_copy(u).wait()


def _experts_call(pstarts, n_blk, xs, w_gate, w_up, w_down):
    d = D_MODEL

    def wsel(e, ps, nb):
        return (e, 0, 0)
    grid_spec = pltpu.PrefetchScalarGridSpec(
        num_scalar_prefetch=2,
        grid=(N_EXPERTS,),
        in_specs=[
            pl.BlockSpec(memory_space=pl.ANY),
            pl.BlockSpec((1, d, EXPERT_DIM), wsel),
            pl.BlockSpec((1, d, EXPERT_DIM), wsel),
            pl.BlockSpec((1, EXPERT_DIM, d), wsel),
        ],
        out_specs=pl.BlockSpec(memory_space=pl.ANY),
        scratch_shapes=[
            pltpu.VMEM((EXPERT_SLOTS, MOE_BLOCK * ROW_TILE, LANES), U32),
            pltpu.VMEM((EXPERT_SLOTS, MOE_BLOCK * ROW_TILE, LANES), U32),
            pltpu.VMEM((d, EXPERT_DIM), BF16),
            pltpu.VMEM((d, EXPERT_DIM), BF16),
            pltpu.VMEM((EXPERT_DIM, d), BF16),
            pltpu.SemaphoreType.DMA((EXPERT_SLOTS,)),
            pltpu.SemaphoreType.DMA((EXPERT_SLOTS,)),
        ],
    )
    return pl.pallas_call(
        _experts_kernel,
        out_shape=jax.ShapeDtypeStruct(xs.shape, U32),
        grid_spec=grid_spec,
        compiler_params=pltpu.CompilerParams(
            dimension_semantics=("arbitrary",), vmem_limit_bytes=VMEM_LIMIT_BYTES),
        name="experts",
    )(pstarts, n_blk, xs, w_gate, w_up, w_down)


def _sc_row_gather(table, row_idx, chunk, n_chunks):
    n_k = row_idx.shape[0]
    windows = row_idx.shape[1] // n_chunks // SC_GATHER_WINDOW
    mesh = plsc.VectorSubcoreMesh(core_axis_name="c", subcore_axis_name="s")

    @functools.partial(
        pl.kernel, out_type=jax.ShapeDtypeStruct((n_k * windows * SC_GATHER_WINDOW, LANES), table.dtype),
        mesh=mesh, compiler_params=pltpu.CompilerParams(use_tc_tiling_on_sc=True), name="sc_row_gather")
    def gather(table_hbm, idx_hbm, out_hbm):
        def body(idx_vmem, out_vmem):
            pltpu.sync_copy(table_hbm.at[idx_vmem.at[0]], out_vmem)
        pltpu.emit_pipeline(
            body, grid=(n_k, windows),
            in_specs=[pl.BlockSpec((1, SC_GATHER_WINDOW), lambda k, i: (k, chunk * windows + i))],
            out_specs=[pl.BlockSpec((SC_GATHER_WINDOW, LANES), lambda k, i: (k * windows + i, 0))],
            core_axis_name=("c", "s"), dimension_semantics=(pltpu.PARALLEL, pltpu.PARALLEL),
        )(idx_hbm, out_hbm)
    return gather(table, row_idx)


def _combine_kernel(gate_ref, base_ref, *refs):
    yk_refs = refs[:TOP_K]
    g_ref, b_ref = refs[TOP_K:TOP_K + 2]
    out_ref = refs[-1]
    tt = base_ref.shape[0]
    acc = base_ref[...]
    gates = gate_ref[...]
    for k in range(TOP_K):
        acc = acc + gates[:, k:k + 1] * _load_row_tiles(yk_refs[k], tt)
    out_ref[...] = _layer_norm(acc, g_ref[...], b_ref[...])


def _combine_call(chunk, n_chunks, gates_t, base, y_chunk, ln_g, ln_b, carry):
    n_tok, d = base.shape
    tt = COMBINE_TILE
    tiles = n_tok // n_chunks // tt
    first = chunk * tiles
    full = lambda a: pl.BlockSpec(a.shape, lambda i: (0,) * a.ndim)
    y_specs = [pl.BlockSpec((tt * ROW_TILE, LANES), lambda i, k=k: (k * tiles + i, 0)) for k in range(TOP_K)]
    in_specs = [
        pl.BlockSpec((tt, TOP_K), lambda i: (first + i, 0)),
        pl.BlockSpec((tt, d), lambda i: (first + i, 0)),
        *y_specs,
        full(ln_g), full(ln_b),
    ]
    operands = [gates_t, base] + [y_chunk] * TOP_K + [ln_g, ln_b]
    aliases = {}
    if carry is not None:
        in_specs.append(pl.BlockSpec(memory_space=pl.ANY))
        operands.append(carry)
        aliases = {len(operands) - 1: 0}
    return pl.pallas_call(
        _combine_kernel,
        out_shape=jax.ShapeDtypeStruct((n_tok, d), F32),
        grid=(tiles,),
        in_specs=in_specs,
        out_specs=pl.BlockSpec((tt, d), lambda i: (first + i, 0)),
        input_output_aliases=aliases,
        compiler_params=pltpu.CompilerParams(
            dimension_semantics=("arbitrary",), vmem_limit_bytes=VMEM_LIMIT_BYTES),
        name="combine",
    )(*operands)


def _group_weights(w_in):
    blocks = w_in.reshape(D_MODEL, 12, GROUP_WIDTH)
    scale = HEAD_DIM ** -0.5
    groups = []
    for g in range(N_ATTN_GROUPS):
        groups.append(jnp.concatenate(
            [blocks[:, g] * scale, blocks[:, N_ATTN_GROUPS + g], blocks[:, 2 * N_ATTN_GROUPS + g]], axis=1))
    groups.append(jnp.concatenate([blocks[:, 9], blocks[:, 10], blocks[:, 11]], axis=1))
    return jnp.stack(groups).astype(BF16)


def kernel(x, w_in, conv_w, w_o, ln1_g, ln1_b, rel_bias, w_router, router_bias, exp_w_gate, exp_w_up,
           exp_w_down, sh_w_gate, sh_w_up, sh_w_down, ln2_g, ln2_b):
    batch, seq, d = x.shape
    assert d == D_MODEL and seq == DILATED_BRANCHES[-1][0] and w_in.shape[0] == DEPTH
    n_tok = batch * seq
    assert n_tok % POST_TILE == 0 and n_tok % DISPATCH_TILE == 0 and n_tok % COMBINE_TILE == 0

    mix = _mix_call(x, _group_weights(w_in[0]), _bias_table(rel_bias), conv_w[0])

    wr = w_router[0]
    wr_hi = wr.astype(BF16)
    wr_lo = (wr - wr_hi.astype(F32)).astype(BF16)
    x2d = x.reshape(n_tok, d)
    base, x1p, e_idx, rank, gates, counts = _post_call(
        mix.reshape(n_tok, d), x2d, w_o[0].astype(BF16), ln1_g, ln1_b,
        wr_hi.T, wr_lo.T, router_bias[0].reshape(N_EXPERTS, 1),
        sh_w_gate[0].astype(BF16), sh_w_up[0].astype(BF16), sh_w_down[0].astype(BF16))

    counts = counts[:, 0]
    padded = (counts + MOE_BLOCK - 1) // MOE_BLOCK * MOE_BLOCK
    pends = jnp.cumsum(padded)
    pstarts = (pends - padded).astype(I32)
    n_blocks = n_tok * TOP_K // MOE_BLOCK + N_EXPERTS
    n_blk = (padded // MOE_BLOCK).astype(I32)
    pos, row_idx = _pos_call(e_idx, rank, pstarts.reshape(N_EXPERTS, 1))

    xs = _sc_row_scatter(x1p, row_idx, n_blocks * MOE_BLOCK * ROW_TILE)
    xs = _pad_fill_call(counts, pstarts, xs)
    y = _experts_call(pstarts, n_blk, xs, exp_w_gate[0], exp_w_up[0], exp_w_down[0])

    gates_t = gates.T
    out = None
    for c in range(COMBINE_CHUNKS):
        y_chunk = _sc_row_gather(y, row_idx, c, COMBINE_CHUNKS)
        out = _combine_call(c, COMBINE_CHUNKS, gates_t, base, y_chunk, ln2_g, ln2_b, out)
    return out.reshape(batch, seq, d)
```

```python
import functools
import math

import numpy as np
import jax
import jax.numpy as jnp
from jax import lax
from jax.experimental import pallas as pl
from jax.experimental.pallas import tpu as pltpu
from jax.experimental.pallas import tpu_sc as plsc

F32 = jnp.float32
BF16 = jnp.bfloat16
I32 = jnp.int32
U32 = jnp.uint32

D_MODEL = 1024
HEAD_DIM = 64
N_ATTN_HEADS = 12
ATTN_WIDTH = N_ATTN_HEADS * HEAD_DIM
CONV_WIDTH = D_MODEL - ATTN_WIDTH
CONV_K = 3
DILATED_BRANCHES = ((128, 1), (512, 4), (2048, 16))
ATTN_BLOCK = 128
N_REL_BUCKETS = 32
REL_MAX_DISTANCE = 2048
N_EXPERTS = 256
TOP_K = 8
N_EXPERT_GROUPS = 8
TOPK_GROUPS = 4
GROUP_SIZE = N_EXPERTS // N_EXPERT_GROUPS
EXPERT_DIM = 256
SHARED_DIM = 256
ROUTED_SCALE = 2.5
DEPTH = 1
DEEPNORM_ALPHA = (2.0 * DEPTH) ** 0.25
LN_EPS = 1e-5

LANES = 128
GROUP_WIDTH = 2 * LANES
HEADS_PER_GROUP = GROUP_WIDTH // HEAD_DIM
N_GROUPS = D_MODEL // GROUP_WIDTH
N_ATTN_GROUPS = ATTN_WIDTH // GROUP_WIDTH
VMEM_LIMIT_BYTES = 56 * 1024 * 1024

MASK_VALUE = -1e30

ROW_CHUNK = 2048
ATTN_UNROLL = 8
CONV_PAD = 8
ROW_TILE = 4
POS_TILE = 512
POST_TILE = 1024
MOE_BLOCK = 512
EXPERT_SLOTS = 4
DISPATCH_TILE = 256
COMBINE_TILE = 512
SC_GATHER_WINDOW = 128
COMBINE_CHUNKS = 8


def _t5_bucket(dist):
    max_exact = N_REL_BUCKETS // 2
    dist = np.asarray(dist)
    log_part = np.log(np.maximum(dist, 1) / max_exact) / math.log(REL_MAX_DISTANCE / max_exact)
    large = max_exact + (log_part * (N_REL_BUCKETS - max_exact)).astype(np.int32)
    large = np.minimum(large, N_REL_BUCKETS - 1)
    return np.where(dist < max_exact, dist, large).astype(np.int32)


def _branch_bucket_and_band(window, dilation):
    w_sub = window // dilation
    qi = np.arange(ATTN_BLOCK)[:, None]
    ki = np.arange(2 * ATTN_BLOCK)[None, :]
    steps = qi + ATTN_BLOCK - ki
    band = (steps >= 0) & (steps <= w_sub)
    bucket = _t5_bucket(np.clip(steps, 0, w_sub) * dilation)
    return bucket, band


def _bias_table(rel_bias):
    tabs = []
    for window, dilation in DILATED_BRANCHES:
        bucket, band = _branch_bucket_and_band(window, dilation)
        onehot = (bucket[:, :, None] == np.arange(N_REL_BUCKETS)).astype(np.float32)
        b = jnp.einsum("qkb,bh->qkh", onehot, rel_bias.astype(F32), precision=lax.Precision.HIGHEST)
        b = jnp.where(band[:, :, None], b, MASK_VALUE)
        tabs.append(b.transpose(2, 0, 1))
    t = jnp.stack(tabs, axis=1)
    t = t.reshape(N_ATTN_GROUPS, HEADS_PER_GROUP, len(DILATED_BRANCHES), ATTN_BLOCK, 2 * ATTN_BLOCK)
    t = t.transpose(0, 2, 1, 3, 4)
    return t.reshape(N_ATTN_GROUPS, len(DILATED_BRANCHES), HEADS_PER_GROUP * ATTN_BLOCK, 2 * ATTN_BLOCK)


def _lane_head():
    return lax.shift_right_logical(lax.broadcasted_iota(I32, (ATTN_BLOCK, GROUP_WIDTH), 1), 6)


def _attn_steps(its, bi, dilation, first, q_s, k_s, v_s, bias_ref, o_s, m_s, l_s):
    logd = int(math.log2(dilation))

    def rows(st):
        return pl.ds(st, ATTN_BLOCK) if dilation == 1 else pl.ds(st, ATTN_BLOCK, stride=dilation)

    def ld(ref, st):
        return jnp.concatenate([ref[0, rows(st), :], ref[1, rows(st), :]], axis=1)

    lane_head = _lane_head()
    bias = bias_ref[0, bi, :, ATTN_BLOCK:] if first else bias_ref[0, bi]
    starts, operands = [], []
    for it in its:
        r = jnp.bitwise_and(it, dilation - 1)
        n = lax.shift_right_logical(it, logd)
        start = r + (dilation * ATTN_BLOCK) * n
        qf = ld(q_s, start)
        if first:
            kk = ld(k_s, start).astype(BF16)
            vv = ld(v_s, start).astype(BF16)
        else:
            prev = start - dilation * ATTN_BLOCK
            kk = jnp.concatenate([ld(k_s, prev), ld(k_s, start)], axis=0).astype(BF16)
            vv = jnp.concatenate([ld(v_s, prev), ld(v_s, start)], axis=0).astype(BF16)
        starts.append(start)
        operands.append((qf, kk, vv))

    results = []
    for qf, kk, vv in operands:
        q4 = jnp.concatenate(
            [jnp.where(lane_head == h, qf, 0.0) for h in range(HEADS_PER_GROUP)], axis=0).astype(BF16)
        s = lax.dot_general(q4, kk, (((1,), (1,)), ((), ())), preferred_element_type=F32) + bias
        m = jnp.max(s, axis=-1, keepdims=True)
        p = jnp.exp(s - m)
        l = jnp.sum(p, axis=-1, keepdims=True)
        pv = jnp.dot(p.astype(BF16), vv, preferred_element_type=F32)
        o = jnp.zeros((ATTN_BLOCK, GROUP_WIDTH), F32)
        mb = jnp.zeros((ATTN_BLOCK, GROUP_WIDTH), F32)
        lb = jnp.zeros((ATTN_BLOCK, GROUP_WIDTH), F32)
        for h in range(HEADS_PER_GROUP):
            sel = lane_head == h
            hr = slice(h * ATTN_BLOCK, (h + 1) * ATTN_BLOCK)
            o = jnp.where(sel, pv[hr], o)
            mb = jnp.where(sel, m[hr], mb)
            lb = jnp.where(sel, l[hr], lb)
        results.append((o, mb, lb))

    for start, (o, mb, lb) in zip(starts, results):
        for sl in range(2):
            lanes = slice(sl * LANES, (sl + 1) * LANES)
            o_s[bi, sl, rows(start), :] = o[:, lanes]
            m_s[bi, sl, rows(start), :] = mb[:, lanes]
            l_s[bi, sl, rows(start), :] = lb[:, lanes]


def _attn_range(lo, hi, **kw):
    n_groups = (hi - lo) // ATTN_UNROLL

    def body(j, carry):
        base = lo + j * ATTN_UNROLL
        _attn_steps([base + u for u in range(ATTN_UNROLL)], **kw)
        return carry
    if n_groups > 0:
        lax.fori_loop(0, n_groups, body, 0)
    tail = list(range(lo + n_groups * ATTN_UNROLL, hi))
    if tail:
        _attn_steps([jnp.int32(t) for t in tail], **kw)


def _mix_kernel(x_ref, w_ref, bias_ref, cw_ref, out_ref, q_s, k_s, v_s, o_s, m_s, l_s):
    g = pl.program_id(1)
    seq = x_ref.shape[1]
    n_chunks = seq // ROW_CHUNK

    def proj(c, carry):
        rows = pl.ds(pl.multiple_of(c * ROW_CHUNK, ROW_CHUNK), ROW_CHUNK)
        res = jnp.dot(x_ref[0, rows, :].astype(BF16), w_ref[0], preferred_element_type=F32)
        for j, dst in enumerate((q_s, k_s, v_s)):
            for sl in range(2):
                lo = j * GROUP_WIDTH + sl * LANES
                dst[sl, rows, :] = res[:, lo:lo + LANES]
        return carry
    lax.fori_loop(0, n_chunks, proj, 0)

    @pl.when(g < N_ATTN_GROUPS)
    def _attention():
        for bi, (window, dilation) in enumerate(DILATED_BRANCHES):
            n_blocks = seq // dilation // ATTN_BLOCK
            n_steps = n_blocks * dilation
            kw = dict(bi=bi, dilation=dilation, q_s=q_s, k_s=k_s, v_s=v_s,
                      bias_ref=bias_ref, o_s=o_s, m_s=m_s, l_s=l_s)
            _attn_range(0, dilation, first=True, **kw)
            _attn_range(dilation, n_steps, first=False, **kw)

        def combine(c, carry):
            rows = pl.ds(pl.multiple_of(c * ROW_CHUNK, ROW_CHUNK), ROW_CHUNK)
            for sl in range(2):
                ms = [m_s[bi, sl, rows, :] for bi in range(len(DILATED_BRANCHES))]
                mx = jnp.maximum(jnp.maximum(ms[0], ms[1]), ms[2])
                es = [jnp.exp(v - mx) for v in ms]
                den = es[0] * l_s[0, sl, rows, :] + es[1] * l_s[1, sl, rows, :] + es[2] * l_s[2, sl, rows, :]
                num = es[0] * o_s[0, sl, rows, :] + es[1] * o_s[1, sl, rows, :] + es[2] * o_s[2, sl, rows, :]
                out_ref[0, rows, sl * LANES:(sl + 1) * LANES] = (num / den).astype(BF16)
            return carry
        lax.fori_loop(0, n_chunks, combine, 0)

    @pl.when(g == N_ATTN_GROUPS)
    def _short_conv():
        pad = CONV_PAD
        u_s = o_s.at[0]
        for sl in range(2):
            u_s[sl, 0:pad, :] = jnp.zeros((pad, LANES), F32)
            for c in range(n_chunks):
                lo = c * ROW_CHUNK
                u_s[sl, pad + lo:pad + lo + ROW_CHUNK, :] = v_s[sl, lo:lo + ROW_CHUNK, :] * q_s[sl, lo:lo + ROW_CHUNK, :]
            w = [cw_ref[kk:kk + 1, sl * LANES:(sl + 1) * LANES] for kk in range(CONV_K)]
            for c in range(n_chunks):
                lo = c * ROW_CHUNK
                y = w[2] * u_s[sl, pad + lo:pad + lo + ROW_CHUNK, :]
                y = y + w[1] * u_s[sl, pad + lo - 1:pad + lo - 1 + ROW_CHUNK, :]
                y = y + w[0] * u_s[sl, pad + lo - 2:pad + lo - 2 + ROW_CHUNK, :]
                out_ref[0, lo:lo + ROW_CHUNK, sl * LANES:(sl + 1) * LANES] = (
                    k_s[sl, lo:lo + ROW_CHUNK, :] * y).astype(BF16)


def _mix_call(x, w_groups, bias_tbl, conv_w):
    batch, seq, d = x.shape
    return pl.pallas_call(
        _mix_kernel,
        out_shape=jax.ShapeDtypeStruct((batch, seq, d), BF16),
        grid=(batch, N_GROUPS),
        in_specs=[
            pl.BlockSpec((1, seq, d), lambda b, g: (b, 0, 0)),
            pl.BlockSpec((1, d, 3 * GROUP_WIDTH), lambda b, g: (g, 0, 0)),
            pl.BlockSpec((1,) + bias_tbl.shape[1:], lambda b, g: (jnp.minimum(g, N_ATTN_GROUPS - 1), 0, 0, 0)),
            pl.BlockSpec(conv_w.shape, lambda b, g: (0, 0)),
        ],
        out_specs=pl.BlockSpec((1, seq, GROUP_WIDTH), lambda b, g: (b, 0, g)),
        scratch_shapes=[
            pltpu.VMEM((2, seq, LANES), F32),
            pltpu.VMEM((2, seq, LANES), F32),
            pltpu.VMEM((2, seq, LANES), F32),
            pltpu.VMEM((len(DILATED_BRANCHES), 2, seq + CONV_PAD, LANES), F32),
            pltpu.VMEM((len(DILATED_BRANCHES), 2, seq, LANES), F32),
            pltpu.VMEM((len(DILATED_BRANCHES), 2, seq, LANES), F32),
        ],
        compiler_params=pltpu.CompilerParams(
            dimension_semantics=("arbitrary", "arbitrary"), vmem_limit_bytes=VMEM_LIMIT_BYTES),
        name="mix",
    )(x, w_groups, bias_tbl, conv_w)


def _layer_norm(h, g, b):
    mu = jnp.mean(h, axis=-1, keepdims=True)
    c = h - mu
    var = jnp.mean(c * c, axis=-1, keepdims=True)
    return c * lax.rsqrt(var + LN_EPS) * g + b


def _bf16_bits(v):
    return lax.bitcast_convert_type(v.astype(BF16).astype(F32), U32)


def _store_row_tiles(ref, val):
    n = val.shape[0]
    for j in range(ROW_TILE):
        lo = _bf16_bits(val[:, (2 * j) * LANES:(2 * j + 1) * LANES])
        hi = _bf16_bits(val[:, (2 * j + 1) * LANES:(2 * j + 2) * LANES])
        ref[pl.ds(j, n, stride=ROW_TILE), :] = jnp.bitwise_or(lax.shift_right_logical(lo, jnp.uint32(16)), hi)


def _load_row_tiles(ref, n):
    pieces = []
    for j in range(ROW_TILE):
        w = ref[pl.ds(j, n, stride=ROW_TILE), :]
        pieces.append(lax.bitcast_convert_type(lax.shift_left(w, jnp.uint32(16)), F32))
        pieces.append(lax.bitcast_convert_type(jnp.bitwise_and(w, jnp.uint32(0xFFFF0000)), F32))
    return jnp.concatenate(pieces, axis=1)


def _first_argmax_rows(v, row_ids, n_rows):
    m = jnp.max(v, axis=0, keepdims=True)
    idx = jnp.min(jnp.where(v == m, row_ids, n_rows), axis=0, keepdims=True)
    return m, idx


def _post_kernel(mix_ref, x_ref, wo_ref, g_ref, b_ref, wrh_ref, wrl_ref, rb_ref, sg_ref, su_ref, sd_ref,
                 base_ref, x1p_ref, e_ref, rank_ref, gate_ref, cnt_ref, carry_s):
    i = pl.program_id(0)
    tm = x_ref.shape[0]

    @pl.when(i == 0)
    def _init():
        carry_s[...] = jnp.zeros_like(carry_s)

    h = DEEPNORM_ALPHA * x_ref[...] + jnp.dot(mix_ref[...], wo_ref[...], preferred_element_type=F32)
    x1 = _layer_norm(h, g_ref[...], b_ref[...])
    _store_row_tiles(x1p_ref, x1)

    x_hi = x1.astype(BF16)
    hg = jnp.dot(x_hi, sg_ref[...], preferred_element_type=F32)
    hu = jnp.dot(x_hi, su_ref[...], preferred_element_type=F32)
    hh = (hg * jax.nn.sigmoid(hg) * hu).astype(BF16)
    base_ref[...] = DEEPNORM_ALPHA * x1 + jnp.dot(hh, sd_ref[...], preferred_element_type=F32)

    x_lo = (x1 - x_hi.astype(F32)).astype(BF16)
    dn = (((1,), (1,)), ((), ()))
    logits = lax.dot_general(wrh_ref[...], x_hi, dn, preferred_element_type=F32)
    logits = logits + lax.dot_general(wrh_ref[...], x_lo, dn, preferred_element_type=F32)
    logits = logits + lax.dot_general(wrl_ref[...], x_hi, dn, preferred_element_type=F32)
    scores = jax.nn.sigmoid(logits)
    biased = scores + rb_ref[...]

    neg_inf = -jnp.inf
    sub_ids = lax.broadcasted_iota(I32, (GROUP_SIZE, tm), 0)
    gs_rows = []
    for gi in range(N_EXPERT_GROUPS):
        bg = biased[gi * GROUP_SIZE:(gi + 1) * GROUP_SIZE]
        m1, i1 = _first_argmax_rows(bg, sub_ids, GROUP_SIZE)
        m2 = jnp.max(jnp.where(sub_ids == i1, neg_inf, bg), axis=0, keepdims=True)
        gs_rows.append(m1 + m2)
    gscore = jnp.concatenate(gs_rows, axis=0)
    grp_ids = lax.broadcasted_iota(I32, (N_EXPERT_GROUPS, tm), 0)
    keep_g = jnp.zeros((N_EXPERT_GROUPS, tm), jnp.bool_)
    for _ in range(TOPK_GROUPS):
        _, gi1 = _first_argmax_rows(gscore, grp_ids, N_EXPERT_GROUPS)
        hit = grp_ids == gi1
        keep_g = jnp.logical_or(keep_g, hit)
        gscore = jnp.where(hit, neg_inf, gscore)
    keep_f = keep_g.astype(F32)
    keep_e = jnp.concatenate(
        [jnp.broadcast_to(keep_f[gi:gi + 1], (GROUP_SIZE, tm)) for gi in range(N_EXPERT_GROUPS)], axis=0)
    masked = jnp.where(keep_e > 0.5, biased, neg_inf)

    exp_ids = lax.broadcasted_iota(I32, (N_EXPERTS, tm), 0)
    sel = jnp.zeros((N_EXPERTS, tm), F32)
    e_rows, g_rows = [], []
    for _ in range(TOP_K):
        _, e1 = _first_argmax_rows(masked, exp_ids, N_EXPERTS)
        hit = exp_ids == e1
        sel = jnp.where(hit, 1.0, sel)
        g_rows.append(jnp.sum(jnp.where(hit, scores, 0.0), axis=0, keepdims=True))
        e_rows.append(e1)
        masked = jnp.where(hit, neg_inf, masked)
    gates = jnp.concatenate(g_rows, axis=0)
    gates = gates / jnp.sum(gates, axis=0, keepdims=True) * ROUTED_SCALE
    gate_ref[...] = gates
    e_ref[...] = jnp.concatenate(e_rows, axis=0)

    t_row = lax.broadcasted_iota(I32, (tm, tm), 0)
    t_col = lax.broadcasted_iota(I32, (tm, tm), 1)
    upper = (t_row < t_col).astype(BF16)
    excl = jnp.dot(sel.astype(BF16), upper, preferred_element_type=F32)
    base = carry_s[...] + excl
    r_rows = [jnp.sum(jnp.where(exp_ids == e1, base, 0.0), axis=0, keepdims=True) for e1 in e_rows]
    rank_ref[...] = jnp.concatenate(r_rows, axis=0).astype(I32)
    carry = carry_s[...] + jnp.sum(sel, axis=1, keepdims=True)
    carry_s[...] = carry
    cnt_ref[...] = carry.astype(I32)


def _post_call(mix2d, x2d, wo, ln_g, ln_b, wr_hi, wr_lo, rb, sg, su, sd):
    n_tok, d = x2d.shape
    tm = POST_TILE
    tok_spec = pl.BlockSpec((tm, d), lambda i: (i, 0))
    k_spec = pl.BlockSpec((TOP_K, tm), lambda i: (0, i))
    full = lambda a: pl.BlockSpec(a.shape, lambda i: (0,) * a.ndim)
    return pl.pallas_call(
        _post_kernel,
        out_shape=(
            jax.ShapeDtypeStruct((n_tok, d), F32),
            jax.ShapeDtypeStruct((n_tok * ROW_TILE, LANES), U32),
            jax.ShapeDtypeStruct((TOP_K, n_tok), I32),
            jax.ShapeDtypeStruct((TOP_K, n_tok), I32),
            jax.ShapeDtypeStruct((TOP_K, n_tok), F32),
            jax.ShapeDtypeStruct((N_EXPERTS, 1), I32),
        ),
        grid=(n_tok // tm,),
        in_specs=[tok_spec, tok_spec, full(wo), full(ln_g), full(ln_b), full(wr_hi), full(wr_lo), full(rb),
                  full(sg), full(su), full(sd)],
        out_specs=(tok_spec, pl.BlockSpec((tm * ROW_TILE, LANES), lambda i: (i, 0)), k_spec, k_spec, k_spec,
                   pl.BlockSpec((N_EXPERTS, 1), lambda i: (0, 0))),
        scratch_shapes=[pltpu.VMEM((N_EXPERTS, 1), F32)],
        compiler_params=pltpu.CompilerParams(
            dimension_semantics=("arbitrary",), vmem_limit_bytes=VMEM_LIMIT_BYTES),
        name="post",
    )(mix2d, x2d, wo, ln_g, ln_b, wr_hi, wr_lo, rb, sg, su, sd)


def _pos_kernel(e_ref, rank_ref, ps_ref, pos_ref, rows_ref):
    tp = e_ref.shape[1]
    exp_ids = lax.broadcasted_iota(I32, (N_EXPERTS, tp), 0)
    ps = ps_ref[...]
    rows = []
    for k in range(TOP_K):
        start = jnp.sum(jnp.where(exp_ids == e_ref[k:k + 1, :], ps, 0), axis=0, keepdims=True)
        rows.append((start + rank_ref[k:k + 1, :]) * ROW_TILE)
    pos = jnp.concatenate(rows, axis=0)
    pos_ref[...] = pos
    src = lax.broadcasted_iota(I32, (tp, tp * ROW_TILE), 0)
    dst = lax.broadcasted_iota(I32, (tp, tp * ROW_TILE), 1)
    spread = (lax.shift_right_logical(dst, int(math.log2(ROW_TILE))) == src).astype(BF16)
    out = lax.broadcasted_iota(I32, (TOP_K, tp * ROW_TILE), 1) & (ROW_TILE - 1)
    for byte in range(3):
        part = jnp.bitwise_and(lax.shift_right_logical(pos, 8 * byte), 255).astype(F32).astype(BF16)
        wide = jnp.dot(part, spread, preferred_element_type=F32).astype(I32)
        out = out + lax.shift_left(wide, 8 * byte)
    rows_ref[...] = out


def _pos_call(e_idx, rank, pstarts):
    n_tok = e_idx.shape[1]
    tp = POS_TILE
    k_spec = pl.BlockSpec((TOP_K, tp), lambda i: (0, i))
    return pl.pallas_call(
        _pos_kernel,
        out_shape=(jax.ShapeDtypeStruct((TOP_K, n_tok), I32),
                   jax.ShapeDtypeStruct((TOP_K, n_tok * ROW_TILE), I32)),
        grid=(n_tok // tp,),
        in_specs=[k_spec, k_spec, pl.BlockSpec((N_EXPERTS, 1), lambda i: (0, 0))],
        out_specs=(k_spec, pl.BlockSpec((TOP_K, tp * ROW_TILE), lambda i: (0, i))),
        compiler_params=pltpu.CompilerParams(dimension_semantics=("arbitrary",)),
        name="positions",
    )(e_idx, rank, pstarts)


def _sc_row_scatter(rows, row_idx, n_out_rows):
    n_k = row_idx.shape[0]
    windows = row_idx.shape[1] // SC_GATHER_WINDOW
    mesh = plsc.VectorSubcoreMesh(core_axis_name="c", subcore_axis_name="s")

    @functools.partial(
        pl.kernel, out_type=jax.ShapeDtypeStruct((n_out_rows, LANES), rows.dtype),
        mesh=mesh, compiler_params=pltpu.CompilerParams(use_tc_tiling_on_sc=True), name="sc_row_scatter")
    def scatter(rows_hbm, idx_hbm, out_hbm):
        def body(rows_vmem, idx_vmem):
            pltpu.sync_copy(rows_vmem, out_hbm.at[idx_vmem.at[0]])
        pltpu.emit_pipeline(
            body, grid=(windows, n_k),
            in_specs=[pl.BlockSpec((SC_GATHER_WINDOW, LANES), lambda i, k: (i, 0)),
                      pl.BlockSpec((1, SC_GATHER_WINDOW), lambda i, k: (k, i))],
            out_specs=[],
            core_axis_name=("c", "s"), dimension_semantics=(pltpu.PARALLEL, pltpu.PARALLEL),
        )(rows_hbm, idx_hbm)
    return scatter(rows, row_idx)


def _pad_fill_kernel(cnt_ref, ps_ref, xs_in_ref, xs_ref, zbuf, zsem):
    del xs_in_ref

    def pad_copies(e, fn):
        cnt = cnt_ref[e]
        pad = jnp.bitwise_and(-cnt, MOE_BLOCK - 1)
        off = ps_ref[e] + cnt
        for b in range(int(math.log2(MOE_BLOCK))):
            size = 1 << b
            hit = jnp.bitwise_and(pad, size)

            @pl.when(hit != 0)
            def _(off=off, size=size):
                fn(pltpu.make_async_copy(
                    zbuf.at[pl.ds(0, size * ROW_TILE)],
                    xs_ref.at[pl.ds(pl.multiple_of(off * ROW_TILE, ROW_TILE), size * ROW_TILE)], zsem))
            off = off + hit

    zbuf[...] = jnp.zeros_like(zbuf)

    def start(e, carry):
        pad_copies(e, lambda cp: cp.start())
        return carry

    def wait(e, carry):
        pad_copies(e, lambda cp: cp.wait())
        return carry
    lax.fori_loop(0, N_EXPERTS, start, 0)
    lax.fori_loop(0, N_EXPERTS, wait, 0)


def _pad_fill_call(counts, pstarts, xs):
    grid_spec = pltpu.PrefetchScalarGridSpec(
        num_scalar_prefetch=2,
        grid=(1,),
        in_specs=[pl.BlockSpec(memory_space=pl.ANY)],
        out_specs=pl.BlockSpec(memory_space=pl.ANY),
        scratch_shapes=[
            pltpu.VMEM((MOE_BLOCK // 2 * ROW_TILE, LANES), U32),
            pltpu.SemaphoreType.DMA(()),
        ],
    )
    return pl.pallas_call(
        _pad_fill_kernel,
        out_shape=jax.ShapeDtypeStruct(xs.shape, xs.dtype),
        grid_spec=grid_spec,
        input_output_aliases={2: 0},
        compiler_params=pltpu.CompilerParams(dimension_semantics=("arbitrary",)),
        name="pad_fill",
    )(counts, pstarts, xs)


def _experts_kernel(ps_ref, nb_ref, xs_ref, wg_ref, wu_ref, wd_ref, y_ref,
                    xbuf, ybuf, wg_b, wu_b, wd_b, sem_in, sem_out):
    e = pl.program_id(0)
    nb = nb_ref[e]
    blk_rows = MOE_BLOCK * ROW_TILE
    g0 = ps_ref[e] // MOE_BLOCK
    n_total = (ps_ref[N_EXPERTS - 1] // MOE_BLOCK) + nb_ref[N_EXPERTS - 1]
    ahead = EXPERT_SLOTS - 1

    def rows_of(g):
        return pl.ds(pl.multiple_of(g * blk_rows, blk_rows), blk_rows)

    def slot_of(g):
        return jnp.bitwise_and(g, EXPERT_SLOTS - 1)

    def in_copy(g):
        s = slot_of(g)
        return pltpu.make_async_copy(xs_ref.at[rows_of(g)], xbuf.at[s], sem_in.at[s])

    def out_copy(g):
        s = slot_of(g)
        return pltpu.make_async_copy(ybuf.at[s], y_ref.at[rows_of(g)], sem_out.at[s])

    @pl.when(e == 0)
    def _prime():
        for u in range(ahead):
            @pl.when(u < n_total)
            def _(u=u):
                in_copy(u).start()

    @pl.when(nb > 0)
    def _run():
        wg_b[...] = wg_ref[0].astype(BF16)
        wu_b[...] = wu_ref[0].astype(BF16)
        wd_b[...] = wd_ref[0].astype(BF16)

        def block(j, carry):
            g = g0 + j
            s = slot_of(g)
            in_copy(g).wait()

            @pl.when(g + ahead < n_total)
            def _prefetch():
                in_copy(g + ahead).start()

            @pl.when(g >= EXPERT_SLOTS)
            def _reclaim():
                out_copy(g - EXPERT_SLOTS).wait()

            xb = _load_row_tiles(xbuf.at[s], MOE_BLOCK).astype(BF16)
            hg = jnp.dot(xb, wg_b[...], preferred_element_type=F32)
            hu = jnp.dot(xb, wu_b[...], preferred_element_type=F32)
            hh = (hg * jax.nn.sigmoid(hg) * hu).astype(BF16)
            _store_row_tiles(ybuf.at[s], jnp.dot(hh, wd_b[...], preferred_element_type=F32))
            out_copy(g).start()
            return carry
        lax.fori_loop(0, nb, block, 0)

    @pl.when(e == N_EXPERTS - 1)
    def _drain():
        for u in range(EXPERT_SLOTS):
            @pl.when(u < n_total)
            def _(u=u):
                out_copy(u).wait()


def _experts_call(pstarts, n_blk, xs, w_gate, w_up, w_down):
    d = D_MODEL

    def wsel(e, ps, nb):
        return (e, 0, 0)
    grid_spec = pltpu.PrefetchScalarGridSpec(
        num_scalar_prefetch=2,
        grid=(N_EXPERTS,),
        in_specs=[
            pl.BlockSpec(memory_space=pl.ANY),
            pl.BlockSpec((1, d, EXPERT_DIM), wsel),
            pl.BlockSpec((1, d, EXPERT_DIM), wsel),
            pl.BlockSpec((1, EXPERT_DIM, d), wsel),
        ],
        out_specs=pl.BlockSpec(memory_space=pl.ANY),
        scratch_shapes=[
            pltpu.VMEM((EXPERT_SLOTS, MOE_BLOCK * ROW_TILE, LANES), U32),
            pltpu.VMEM((EXPERT_SLOTS, MOE_BLOCK * ROW_TILE, LANES), U32),
            pltpu.VMEM((d, EXPERT_DIM), BF16),
            pltpu.VMEM((d, EXPERT_DIM), BF16),
            pltpu.VMEM((EXPERT_DIM, d), BF16),
            pltpu.SemaphoreType.DMA((EXPERT_SLOTS,)),
            pltpu.SemaphoreType.DMA((EXPERT_SLOTS,)),
        ],
    )
    return pl.pallas_call(
        _experts_kernel,
        out_shape=jax.ShapeDtypeStruct(xs.shape, U32),
        grid_spec=grid_spec,
        compiler_params=pltpu.CompilerParams(
            dimension_semantics=("arbitrary",), vmem_limit_bytes=VMEM_LIMIT_BYTES),
        name="experts",
    )(pstarts, n_blk, xs, w_gate, w_up, w_down)


def _sc_row_gather(table, row_idx, chunk, n_chunks):
    n_k = row_idx.shape[0]
    windows = row_idx.shape[1] // n_chunks // SC_GATHER_WINDOW
    mesh = plsc.VectorSubcoreMesh(core_axis_name="c", subcore_axis_name="s")

    @functools.partial(
        pl.kernel, out_type=jax.ShapeDtypeStruct((n_k * windows * SC_GATHER_WINDOW, LANES), table.dtype),
        mesh=mesh, compiler_params=pltpu.CompilerParams(use_tc_tiling_on_sc=True), name="sc_row_gather")
    def gather(table_hbm, idx_hbm, out_hbm):
        def body(idx_vmem, out_vmem):
            pltpu.sync_copy(table_hbm.at[idx_vmem.at[0]], out_vmem)
        pltpu.emit_pipeline(
            body, grid=(n_k, windows),
            in_specs=[pl.BlockSpec((1, SC_GATHER_WINDOW), lambda k, i: (k, chunk * windows + i))],
            out_specs=[pl.BlockSpec((SC_GATHER_WINDOW, LANES), lambda k, i: (k * windows + i, 0))],
            core_axis_name=("c", "s"), dimension_semantics=(pltpu.PARALLEL, pltpu.PARALLEL),
        )(idx_hbm, out_hbm)
    return gather(table, row_idx)


def _combine_kernel(gate_ref, base_ref, *refs):
    yk_refs = refs[:TOP_K]
    g_ref, b_ref = refs[TOP_K:TOP_K + 2]
    out_ref = refs[-1]
    tt = base_ref.shape[0]
    acc = base_ref[...]
    gates = gate_ref[...]
    for k in range(TOP_K):
        acc = acc + gates[:, k:k + 1] * _load_row_tiles(yk_refs[k], tt)
    out_ref[...] = _layer_norm(acc, g_ref[...], b_ref[...])


def _combine_call(chunk, n_chunks, gates_t, base, y_chunk, ln_g, ln_b, carry):
    n_tok, d = base.shape
    tt = COMBINE_TILE
    tiles = n_tok // n_chunks // tt
    first = chunk * tiles
    full = lambda a: pl.BlockSpec(a.shape, lambda i: (0,) * a.ndim)
    y_specs = [pl.BlockSpec((tt * ROW_TILE, LANES), lambda i, k=k: (k * tiles + i, 0)) for k in range(TOP_K)]
    in_specs = [
        pl.BlockSpec((tt, TOP_K), lambda i: (first + i, 0)),
        pl.BlockSpec((tt, d), lambda i: (first + i, 0)),
        *y_specs,
        full(ln_g), full(ln_b),
    ]
    operands = [gates_t, base] + [y_chunk] * TOP_K + [ln_g, ln_b]
    aliases = {}
    if carry is not None:
        in_specs.append(pl.BlockSpec(memory_space=pl.ANY))
        operands.append(carry)
        aliases = {len(operands) - 1: 0}
    return pl.pallas_call(
        _combine_kernel,
        out_shape=jax.ShapeDtypeStruct((n_tok, d), F32),
        grid=(tiles,),
        in_specs=in_specs,
        out_specs=pl.BlockSpec((tt, d), lambda i: (first + i, 0)),
        input_output_aliases=aliases,
        compiler_params=pltpu.CompilerParams(
            dimension_semantics=("arbitrary",), vmem_limit_bytes=VMEM_LIMIT_BYTES),
        name="combine",
    )(*operands)


def _group_weights(w_in):
    blocks = w_in.reshape(D_MODEL, 12, GROUP_WIDTH)
    scale = HEAD_DIM ** -0.5
    groups = []
    for g in range(N_ATTN_GROUPS):
        groups.append(jnp.concatenate(
            [blocks[:, g] * scale, blocks[:, N_ATTN_GROUPS + g], blocks[:, 2 * N_ATTN_GROUPS + g]], axis=1))
    groups.append(jnp.concatenate([blocks[:, 9], blocks[:, 10], blocks[:, 11]], axis=1))
    return jnp.stack(groups).astype(BF16)


def kernel(x, w_in, conv_w, w_o, ln1_g, ln1_b, rel_bias, w_router, router_bias, exp_w_gate, exp_w_up,
           exp_w_down, sh_w_gate, sh_w_up, sh_w_down, ln2_g, ln2_b):
    batch, seq, d = x.shape
    assert d == D_MODEL and seq == DILATED_BRANCHES[-1][0] and w_in.shape[0] == DEPTH
    n_tok = batch * seq
    assert n_tok % POST_TILE == 0 and n_tok % DISPATCH_TILE == 0 and n_tok % COMBINE_TILE == 0

    mix = _mix_call(x, _group_weights(w_in[0]), _bias_table(rel_bias), conv_w[0])

    wr = w_router[0]
    wr_hi = wr.astype(BF16)
    wr_lo = (wr - wr_hi.astype(F32)).astype(BF16)
    x2d = x.reshape(n_tok, d)
    base, x1p, e_idx, rank, gates, counts = _post_call(
        mix.reshape(n_tok, d), x2d, w_o[0].astype(BF16), ln1_g, ln1_b,
        wr_hi.T, wr_lo.T, router_bias[0].reshape(N_EXPERTS, 1),
        sh_w_gate[0].astype(BF16), sh_w_up[0].astype(BF16), sh_w_down[0].astype(BF16))

    counts = counts[:, 0]
    padded = (counts + MOE_BLOCK - 1) // MOE_BLOCK * MOE_BLOCK
    pends = jnp.cumsum(padded)
    pstarts = (pends - padded).astype(I32)
    n_blocks = n_tok * TOP_K // MOE_BLOCK + N_EXPERTS
    n_blk = (padded // MOE_BLOCK).astype(I32)
    pos, row_idx = _pos_call(e_idx, rank, pstarts.reshape(N_EXPERTS, 1))

    xs = _sc_row_scatter(x1p, row_idx, n_blocks * MOE_BLOCK * ROW_TILE)
    xs = _pad_fill_call(counts, pstarts, xs)
    y = _experts_call(pstarts, n_blk, xs, exp_w_gate[0], exp_w_up[0], exp_w_down[0])

    gates_t = gates.T
    out = None
    for c in range(COMBINE_CHUNKS):
        y_chunk = _sc_row_gather(y, row_idx, c, COMBINE_CHUNKS)
        out = _combine_call(c, COMBINE_CHUNKS, gates_t, base, y_chunk, ln2_g, ln2_b, out)
    return out.reshape(batch, seq, d)
```

```python
import functools
import math

import numpy as np
import jax
import jax.numpy as jnp
from jax import lax
from jax.experimental import pallas as pl
from jax.experimental.pallas import tpu as pltpu
from jax.experimental.pallas import tpu_sc as plsc

F32 = jnp.float32
BF16 = jnp.bfloat16
I32 = jnp.int32
U32 = jnp.uint32

D_MODEL = 1024
HEAD_DIM = 64
N_ATTN_HEADS = 12
ATTN_WIDTH = N_ATTN_HEADS * HEAD_DIM
CONV_WIDTH = D_MODEL - ATTN_WIDTH
CONV_K = 3
DILATED_BRANCHES = ((128, 1), (512, 4), (2048, 16))
ATTN_BLOCK = 128
N_REL_BUCKETS = 32
REL_MAX_DISTANCE = 2048
N_EXPERTS = 256
TOP_K = 8
N_EXPERT_GROUPS = 8
TOPK_GROUPS = 4
GROUP_SIZE = N_EXPERTS // N_EXPERT_GROUPS
EXPERT_DIM = 256
SHARED_DIM = 256
ROUTED_SCALE = 2.5
DEPTH = 1
DEEPNORM_ALPHA = (2.0 * DEPTH) ** 0.25
LN_EPS = 1e-5

LANES = 128
GROUP_WIDTH = 2 * LANES
HEADS_PER_GROUP = GROUP_WIDTH // HEAD_DIM
N_GROUPS = D_MODEL // GROUP_WIDTH
N_ATTN_GROUPS = ATTN_WIDTH // GROUP_WIDTH
VMEM_LIMIT_BYTES = 56 * 1024 * 1024

MASK_VALUE = -1e30

ROW_CHUNK = 2048
ATTN_UNROLL = 8
CONV_PAD = 8
ROW_TILE = 4
POS_TILE = 512
POST_TILE = 1024
MOE_BLOCK = 512
EXPERT_SLOTS = 4
COMBINE_TILE = 512
SC_GATHER_WINDOW = 128
COMBINE_CHUNKS = 8


def _t5_bucket(dist):
    max_exact = N_REL_BUCKETS // 2
    dist = np.asarray(dist)
    log_part = np.log(np.maximum(dist, 1) / max_exact) / math.log(REL_MAX_DISTANCE / max_exact)
    large = max_exact + (log_part * (N_REL_BUCKETS - max_exact)).astype(np.int32)
    large = np.minimum(large, N_REL_BUCKETS - 1)
    return np.where(dist < max_exact, dist, large).astype(np.int32)


def _branch_bucket_and_band(window, dilation):
    w_sub = window // dilation
    qi = np.arange(ATTN_BLOCK)[:, None]
    ki = np.arange(2 * ATTN_BLOCK)[None, :]
    steps = qi + ATTN_BLOCK - ki
    band = (steps >= 0) & (steps <= w_sub)
    bucket = _t5_bucket(np.clip(steps, 0, w_sub) * dilation)
    return bucket, band


def _bias_table(rel_bias):
    tabs = []
    for window, dilation in DILATED_BRANCHES:
        bucket, band = _branch_bucket_and_band(window, dilation)
        onehot = (bucket[:, :, None] == np.arange(N_REL_BUCKETS)).astype(np.float32)
        b = jnp.einsum("qkb,bh->qkh", onehot, rel_bias.astype(F32), precision=lax.Precision.HIGHEST)
        b = jnp.where(band[:, :, None], b, MASK_VALUE)
        tabs.append(b.transpose(2, 0, 1))
    t = jnp.stack(tabs, axis=1)
    t = t.reshape(N_ATTN_GROUPS, HEADS_PER_GROUP, len(DILATED_BRANCHES), ATTN_BLOCK, 2 * ATTN_BLOCK)
    t = t.transpose(0, 2, 1, 3, 4)
    return t.reshape(N_ATTN_GROUPS, len(DILATED_BRANCHES), HEADS_PER_GROUP * ATTN_BLOCK, 2 * ATTN_BLOCK)


def _lane_head():
    return lax.shift_right_logical(lax.broadcasted_iota(I32, (ATTN_BLOCK, GROUP_WIDTH), 1), 6)


def _attn_steps(its, bi, dilation, first, q_s, k_s, v_s, bias_ref, o_s, m_s, l_s):
    logd = int(math.log2(dilation))

    def rows(st):
        return pl.ds(st, ATTN_BLOCK) if dilation == 1 else pl.ds(st, ATTN_BLOCK, stride=dilation)

    def ld(ref, st):
        return jnp.concatenate([ref[0, rows(st), :], ref[1, rows(st), :]], axis=1)

    lane_head = _lane_head()
    bias = bias_ref[0, bi, :, ATTN_BLOCK:] if first else bias_ref[0, bi]
    starts, operands = [], []
    for it in its:
        r = jnp.bitwise_and(it, dilation - 1)
        n = lax.shift_right_logical(it, logd)
        start = r + (dilation * ATTN_BLOCK) * n
        qf = ld(q_s, start)
        if first:
            kk = ld(k_s, start).astype(BF16)
            vv = ld(v_s, start).astype(BF16)
        else:
            prev = start - dilation * ATTN_BLOCK
            kk = jnp.concatenate([ld(k_s, prev), ld(k_s, start)], axis=0).astype(BF16)
            vv = jnp.concatenate([ld(v_s, prev), ld(v_s, start)], axis=0).astype(BF16)
        starts.append(start)
        operands.append((qf, kk, vv))

    results = []
    for qf, kk, vv in operands:
        q4 = jnp.concatenate(
            [jnp.where(lane_head == h, qf, 0.0) for h in range(HEADS_PER_GROUP)], axis=0).astype(BF16)
        s = lax.dot_general(q4, kk, (((1,), (1,)), ((), ())), preferred_element_type=F32) + bias
        m = jnp.max(s, axis=-1, keepdims=True)
        p = jnp.exp(s - m)
        l = jnp.sum(p, axis=-1, keepdims=True)
        pv = jnp.dot(p.astype(BF16), vv, preferred_element_type=F32)
        o = jnp.zeros((ATTN_BLOCK, GROUP_WIDTH), F32)
        mb = jnp.zeros((ATTN_BLOCK, GROUP_WIDTH), F32)
        lb = jnp.zeros((ATTN_BLOCK, GROUP_WIDTH), F32)
        for h in range(HEADS_PER_GROUP):
            sel = lane_head == h
            hr = slice(h * ATTN_BLOCK, (h + 1) * ATTN_BLOCK)
            o = jnp.where(sel, pv[hr], o)
            mb = jnp.where(sel, m[hr], mb)
            lb = jnp.where(sel, l[hr], lb)
        results.append((o, mb, lb))

    for start, (o, mb, lb) in zip(starts, results):
        for sl in range(2):
            lanes = slice(sl * LANES, (sl + 1) * LANES)
            o_s[bi, sl, rows(start), :] = o[:, lanes]
            m_s[bi, sl, rows(start), :] = mb[:, lanes]
            l_s[bi, sl, rows(start), :] = lb[:, lanes]


def _attn_range(lo, hi, **kw):
    n_groups = (hi - lo) // ATTN_UNROLL

    def body(j, carry):
        base = lo + j * ATTN_UNROLL
        _attn_steps([base + u for u in range(ATTN_UNROLL)], **kw)
        return carry
    if n_groups > 0:
        lax.fori_loop(0, n_groups, body, 0)
    tail = list(range(lo + n_groups * ATTN_UNROLL, hi))
    if tail:
        _attn_steps([jnp.int32(t) for t in tail], **kw)


def _mix_kernel(x_ref, w_ref, bias_ref, cw_ref, out_ref, q_s, k_s, v_s, o_s, m_s, l_s):
    g = pl.program_id(1)
    seq = x_ref.shape[1]
    n_chunks = seq // ROW_CHUNK

    def proj(c, carry):
        rows = pl.ds(pl.multiple_of(c * ROW_CHUNK, ROW_CHUNK), ROW_CHUNK)
        res = jnp.dot(x_ref[0, rows, :].astype(BF16), w_ref[0], preferred_element_type=F32)
        for j, dst in enumerate((q_s, k_s, v_s)):
            for sl in range(2):
                lo = j * GROUP_WIDTH + sl * LANES
                dst[sl, rows, :] = res[:, lo:lo + LANES]
        return carry
    lax.fori_loop(0, n_chunks, proj, 0)

    @pl.when(g < N_ATTN_GROUPS)
    def _attention():
        for bi, (window, dilation) in enumerate(DILATED_BRANCHES):
            n_blocks = seq // dilation // ATTN_BLOCK
            n_steps = n_blocks * dilation
            kw = dict(bi=bi, dilation=dilation, q_s=q_s, k_s=k_s, v_s=v_s,
                      bias_ref=bias_ref, o_s=o_s, m_s=m_s, l_s=l_s)
            _attn_range(0, dilation, first=True, **kw)
            _attn_range(dilation, n_steps, first=False, **kw)

        def combine(c, carry):
            rows = pl.ds(pl.multiple_of(c * ROW_CHUNK, ROW_CHUNK), ROW_CHUNK)
            for sl in range(2):
                ms = [m_s[bi, sl, rows, :] for bi in range(len(DILATED_BRANCHES))]
                mx = jnp.maximum(jnp.maximum(ms[0], ms[1]), ms[2])
                es = [jnp.exp(v - mx) for v in ms]
                den = es[0] * l_s[0, sl, rows, :] + es[1] * l_s[1, sl, rows, :] + es[2] * l_s[2, sl, rows, :]
                num = es[0] * o_s[0, sl, rows, :] + es[1] * o_s[1, sl, rows, :] + es[2] * o_s[2, sl, rows, :]
                out_ref[0, rows, sl * LANES:(sl + 1) * LANES] = (num / den).astype(BF16)
            return carry
        lax.fori_loop(0, n_chunks, combine, 0)

    @pl.when(g == N_ATTN_GROUPS)
    def _short_conv():
        pad = CONV_PAD
        u_s = o_s.at[0]
        for sl in range(2):
            u_s[sl, 0:pad, :] = jnp.zeros((pad, LANES), F32)
            for c in range(n_chunks):
                lo = c * ROW_CHUNK
                u_s[sl, pad + lo:pad + lo + ROW_CHUNK, :] = v_s[sl, lo:lo + ROW_CHUNK, :] * q_s[sl, lo:lo + ROW_CHUNK, :]
            w = [cw_ref[kk:kk + 1, sl * LANES:(sl + 1) * LANES] for kk in range(CONV_K)]
            for c in range(n_chunks):
                lo = c * ROW_CHUNK
                y = w[2] * u_s[sl, pad + lo:pad + lo + ROW_CHUNK, :]
                y = y + w[1] * u_s[sl, pad + lo - 1:pad + lo - 1 + ROW_CHUNK, :]
                y = y + w[0] * u_s[sl, pad + lo - 2:pad + lo - 2 + ROW_CHUNK, :]
                out_ref[0, lo:lo + ROW_CHUNK, sl * LANES:(sl + 1) * LANES] = (
                    k_s[sl, lo:lo + ROW_CHUNK, :] * y).astype(BF16)


def _mix_call(x, w_groups, bias_tbl, conv_w):
    batch, seq, d = x.shape
    return pl.pallas_call(
        _mix_kernel,
        out_shape=jax.ShapeDtypeStruct((batch, seq, d), BF16),
        grid=(batch, N_GROUPS),
        in_specs=[
            pl.BlockSpec((1, seq, d), lambda b, g: (b, 0, 0)),
            pl.BlockSpec((1, d, 3 * GROUP_WIDTH), lambda b, g: (g, 0, 0)),
            pl.BlockSpec((1,) + bias_tbl.shape[1:], lambda b, g: (jnp.minimum(g, N_ATTN_GROUPS - 1), 0, 0, 0)),
            pl.BlockSpec(conv_w.shape, lambda b, g: (0, 0)),
        ],
        out_specs=pl.BlockSpec((1, seq, GROUP_WIDTH), lambda b, g: (b, 0, g)),
        scratch_shapes=[
            pltpu.VMEM((2, seq, LANES), F32),
            pltpu.VMEM((2, seq, LANES), F32),
            pltpu.VMEM((2, seq, LANES), F32),
            pltpu.VMEM((len(DILATED_BRANCHES), 2, seq + CONV_PAD, LANES), F32),
            pltpu.VMEM((len(DILATED_BRANCHES), 2, seq, LANES), F32),
            pltpu.VMEM((len(DILATED_BRANCHES), 2, seq, LANES), F32),
        ],
        compiler_params=pltpu.CompilerParams(
            dimension_semantics=("arbitrary", "arbitrary"), vmem_limit_bytes=VMEM_LIMIT_BYTES),
        name="mix",
    )(x, w_groups, bias_tbl, conv_w)


def _layer_norm(h, g, b):
    mu = jnp.mean(h, axis=-1, keepdims=True)
    c = h - mu
    var = jnp.mean(c * c, axis=-1, keepdims=True)
    return c * lax.rsqrt(var + LN_EPS) * g + b


def _bf16_bits(v):
    return lax.bitcast_convert_type(v.astype(BF16).astype(F32), U32)


def _store_row_tiles(ref, val):
    n = val.shape[0]
    for j in range(ROW_TILE):
        lo = _bf16_bits(val[:, (2 * j) * LANES:(2 * j + 1) * LANES])
        hi = _bf16_bits(val[:, (2 * j + 1) * LANES:(2 * j + 2) * LANES])
        ref[pl.ds(j, n, stride=ROW_TILE), :] = jnp.bitwise_or(lax.shift_right_logical(lo, jnp.uint32(16)), hi)


def _load_row_tiles(ref, n):
    pieces = []
    for j in range(ROW_TILE):
        w = ref[pl.ds(j, n, stride=ROW_TILE), :]
        pieces.append(lax.bitcast_convert_type(lax.shift_left(w, jnp.uint32(16)), F32))
        pieces.append(lax.bitcast_convert_type(jnp.bitwise_and(w, jnp.uint32(0xFFFF0000)), F32))
    return jnp.concatenate(pieces, axis=1)


def _first_argmax_rows(v, row_ids, n_rows):
    m = jnp.max(v, axis=0, keepdims=True)
    idx = jnp.min(jnp.where(v == m, row_ids, n_rows), axis=0, keepdims=True)
    return m, idx


def _post_kernel(mix_ref, x_ref, wo_ref, g_ref, b_ref, wrh_ref, wrl_ref, rb_ref, sg_ref, su_ref, sd_ref,
                 base_ref, x1p_ref, e_ref, rank_ref, gate_ref, cnt_ref, carry_s):
    i = pl.program_id(0)
    tm = x_ref.shape[0]

    @pl.when(i == 0)
    def _init():
        carry_s[...] = jnp.zeros_like(carry_s)

    h = DEEPNORM_ALPHA * x_ref[...] + jnp.dot(mix_ref[...], wo_ref[...], preferred_element_type=F32)
    x1 = _layer_norm(h, g_ref[...], b_ref[...])
    _store_row_tiles(x1p_ref, x1)

    x_hi = x1.astype(BF16)
    hg = jnp.dot(x_hi, sg_ref[...], preferred_element_type=F32)
    hu = jnp.dot(x_hi, su_ref[...], preferred_element_type=F32)
    hh = (hg * jax.nn.sigmoid(hg) * hu).astype(BF16)
    base_ref[...] = DEEPNORM_ALPHA * x1 + jnp.dot(hh, sd_ref[...], preferred_element_type=F32)

    x_lo = (x1 - x_hi.astype(F32)).astype(BF16)
    dn = (((1,), (1,)), ((), ()))
    logits = lax.dot_general(wrh_ref[...], x_hi, dn, preferred_element_type=F32)
    logits = logits + lax.dot_general(wrh_ref[...], x_lo, dn, preferred_element_type=F32)
    logits = logits + lax.dot_general(wrl_ref[...], x_hi, dn, preferred_element_type=F32)
    scores = jax.nn.sigmoid(logits)
    biased = scores + rb_ref[...]

    neg_inf = -jnp.inf
    sub_ids = lax.broadcasted_iota(I32, (GROUP_SIZE, tm), 0)
    gs_rows = []
    for gi in range(N_EXPERT_GROUPS):
        bg = biased[gi * GROUP_SIZE:(gi + 1) * GROUP_SIZE]
        m1, i1 = _first_argmax_rows(bg, sub_ids, GROUP_SIZE)
        m2 = jnp.max(jnp.where(sub_ids == i1, neg_inf, bg), axis=0, keepdims=True)
        gs_rows.append(m1 + m2)
    gscore = jnp.concatenate(gs_rows, axis=0)
    grp_ids = lax.broadcasted_iota(I32, (N_EXPERT_GROUPS, tm), 0)
    keep_g = jnp.zeros((N_EXPERT_GROUPS, tm), jnp.bool_)
    for _ in range(TOPK_GROUPS):
        _, gi1 = _first_argmax_rows(gscore, grp_ids, N_EXPERT_GROUPS)
        hit = grp_ids == gi1
        keep_g = jnp.logical_or(keep_g, hit)
        gscore = jnp.where(hit, neg_inf, gscore)
    keep_f = keep_g.astype(F32)
    keep_e = jnp.concatenate(
        [jnp.broadcast_to(keep_f[gi:gi + 1], (GROUP_SIZE, tm)) for gi in range(N_EXPERT_GROUPS)], axis=0)
    masked = jnp.where(keep_e > 0.5, biased, neg_inf)

    exp_ids = lax.broadcasted_iota(I32, (N_EXPERTS, tm), 0)
    sel = jnp.zeros((N_EXPERTS, tm), F32)
    e_rows, g_rows = [], []
    for _ in range(TOP_K):
        _, e1 = _first_argmax_rows(masked, exp_ids, N_EXPERTS)
        hit = exp_ids == e1
        sel = jnp.where(hit, 1.0, sel)
        g_rows.append(jnp.sum(jnp.where(hit, scores, 0.0), axis=0, keepdims=True))
        e_rows.append(e1)
        masked = jnp.where(hit, neg_inf, masked)
    gates = jnp.concatenate(g_rows, axis=0)
    gates = gates / jnp.sum(gates, axis=0, keepdims=True) * ROUTED_SCALE
    gate_ref[...] = gates
    e_ref[...] = jnp.concatenate(e_rows, axis=0)

    t_row = lax.broadcasted_iota(I32, (tm, tm), 0)
    t_col = lax.broadcasted_iota(I32, (tm, tm), 1)
    upper = (t_row < t_col).astype(BF16)
    excl = jnp.dot(sel.astype(BF16), upper, preferred_element_type=F32)
    base = carry_s[...] + excl
    r_rows = [jnp.sum(jnp.where(exp_ids == e1, base, 0.0), axis=0, keepdims=True) for e1 in e_rows]
    rank_ref[...] = jnp.concatenate(r_rows, axis=0).astype(I32)
    carry = carry_s[...] + jnp.sum(sel, axis=1, keepdims=True)
    carry_s[...] = carry
    cnt_ref[...] = carry.astype(I32)


def _post_call(mix2d, x2d, wo, ln_g, ln_b, wr_hi, wr_lo, rb, sg, su, sd):
    n_tok, d = x2d.shape
    tm = POST_TILE
    tok_spec = pl.BlockSpec((tm, d), lambda i: (i, 0))
    k_spec = pl.BlockSpec((TOP_K, tm), lambda i: (0, i))
    full = lambda a: pl.BlockSpec(a.shape, lambda i: (0,) * a.ndim)
    return pl.pallas_call(
        _post_kernel,
        out_shape=(
            jax.ShapeDtypeStruct((n_tok, d), F32),
            jax.ShapeDtypeStruct((n_tok * ROW_TILE, LANES), U32),
            jax.ShapeDtypeStruct((TOP_K, n_tok), I32),
            jax.ShapeDtypeStruct((TOP_K, n_tok), I32),
            jax.ShapeDtypeStruct((TOP_K, n_tok), F32),
            jax.ShapeDtypeStruct((N_EXPERTS, 1), I32),
        ),
        grid=(n_tok // tm,),
        in_specs=[tok_spec, tok_spec, full(wo), full(ln_g), full(ln_b), full(wr_hi), full(wr_lo), full(rb),
                  full(sg), full(su), full(sd)],
        out_specs=(tok_spec, pl.BlockSpec((tm * ROW_TILE, LANES), lambda i: (i, 0)), k_spec, k_spec, k_spec,
                   pl.BlockSpec((N_EXPERTS, 1), lambda i: (0, 0))),
        scratch_shapes=[pltpu.VMEM((N_EXPERTS, 1), F32)],
        compiler_params=pltpu.CompilerParams(
            dimension_semantics=("arbitrary",), vmem_limit_bytes=VMEM_LIMIT_BYTES),
        name="post",
    )(mix2d, x2d, wo, ln_g, ln_b, wr_hi, wr_lo, rb, sg, su, sd)


def _pos_kernel(e_ref, rank_ref, ps_ref, pos_ref, rows_ref):
    tp = e_ref.shape[1]
    exp_ids = lax.broadcasted_iota(I32, (N_EXPERTS, tp), 0)
    ps = ps_ref[...]
    rows = []
    for k in range(TOP_K):
        start = jnp.sum(jnp.where(exp_ids == e_ref[k:k + 1, :], ps, 0), axis=0, keepdims=True)
        rows.append((start + rank_ref[k:k + 1, :]) * ROW_TILE)
    pos = jnp.concatenate(rows, axis=0)
    pos_ref[...] = pos
    src = lax.broadcasted_iota(I32, (tp, tp * ROW_TILE), 0)
    dst = lax.broadcasted_iota(I32, (tp, tp * ROW_TILE), 1)
    spread = (lax.shift_right_logical(dst, int(math.log2(ROW_TILE))) == src).astype(BF16)
    out = lax.broadcasted_iota(I32, (TOP_K, tp * ROW_TILE), 1) & (ROW_TILE - 1)
    for byte in range(3):
        part = jnp.bitwise_and(lax.shift_right_logical(pos, 8 * byte), 255).astype(F32).astype(BF16)
        wide = jnp.dot(part, spread, preferred_element_type=F32).astype(I32)
        out = out + lax.shift_left(wide, 8 * byte)
    rows_ref[...] = out


def _pos_call(e_idx, rank, pstarts):
    n_tok = e_idx.shape[1]
    tp = POS_TILE
    k_spec = pl.BlockSpec((TOP_K, tp), lambda i: (0, i))
    return pl.pallas_call(
        _pos_kernel,
        out_shape=(jax.ShapeDtypeStruct((TOP_K, n_tok), I32),
                   jax.ShapeDtypeStruct((TOP_K, n_tok * ROW_TILE), I32)),
        grid=(n_tok // tp,),
        in_specs=[k_spec, k_spec, pl.BlockSpec((N_EXPERTS, 1), lambda i: (0, 0))],
        out_specs=(k_spec, pl.BlockSpec((TOP_K, tp * ROW_TILE), lambda i: (0, i))),
        compiler_params=pltpu.CompilerParams(dimension_semantics=("arbitrary",)),
        name="positions",
    )(e_idx, rank, pstarts)


def _sc_row_scatter(rows, row_idx, n_out_rows):
    n_k = row_idx.shape[0]
    windows = row_idx.shape[1] // SC_GATHER_WINDOW
    mesh = plsc.VectorSubcoreMesh(core_axis_name="c", subcore_axis_name="s")

    @functools.partial(
        pl.kernel, out_type=jax.ShapeDtypeStruct((n_out_rows, LANES), rows.dtype),
        mesh=mesh, compiler_params=pltpu.CompilerParams(use_tc_tiling_on_sc=True), name="sc_row_scatter")
    def scatter(rows_hbm, idx_hbm, out_hbm):
        def body(rows_vmem, idx_vmem):
            pltpu.sync_copy(rows_vmem, out_hbm.at[idx_vmem.at[0]])
        pltpu.emit_pipeline(
            body, grid=(windows, n_k),
            in_specs=[pl.BlockSpec((SC_GATHER_WINDOW, LANES), lambda i, k: (i, 0)),
                      pl.BlockSpec((1, SC_GATHER_WINDOW), lambda i, k: (k, i))],
            out_specs=[],
            core_axis_name=("c", "s"), dimension_semantics=(pltpu.PARALLEL, pltpu.PARALLEL),
        )(rows_hbm, idx_hbm)
    return scatter(rows, row_idx)


def _pad_fill_kernel(cnt_ref, ps_ref, xs_in_ref, xs_ref, zbuf, zsem):
    del xs_in_ref

    def pad_copies(e, fn):
        cnt = cnt_ref[e]
        pad = jnp.bitwise_and(-cnt, MOE_BLOCK - 1)
        off = ps_ref[e] + cnt
        for b in range(int(math.log2(MOE_BLOCK))):
            size = 1 << b
            hit = jnp.bitwise_and(pad, size)

            @pl.when(hit != 0)
            def _(off=off, size=size):
                fn(pltpu.make_async_copy(
                    zbuf.at[pl.ds(0, size * ROW_TILE)],
                    xs_ref.at[pl.ds(pl.multiple_of(off * ROW_TILE, ROW_TILE), size * ROW_TILE)], zsem))
            off = off + hit

    zbuf[...] = jnp.zeros_like(zbuf)

    def start(e, carry):
        pad_copies(e, lambda cp: cp.start())
        return carry

    def wait(e, carry):
        pad_copies(e, lambda cp: cp.wait())
        return carry
    lax.fori_loop(0, N_EXPERTS, start, 0)
    lax.fori_loop(0, N_EXPERTS, wait, 0)


def _pad_fill_call(counts, pstarts, xs):
    grid_spec = pltpu.PrefetchScalarGridSpec(
        num_scalar_prefetch=2,
        grid=(1,),
        in_specs=[pl.BlockSpec(memory_space=pl.ANY)],
        out_specs=pl.BlockSpec(memory_space=pl.ANY),
        scratch_shapes=[
            pltpu.VMEM((MOE_BLOCK // 2 * ROW_TILE, LANES), U32),
            pltpu.SemaphoreType.DMA(()),
        ],
    )
    return pl.pallas_call(
        _pad_fill_kernel,
        out_shape=jax.ShapeDtypeStruct(xs.shape, xs.dtype),
        grid_spec=grid_spec,
        input_output_aliases={2: 0},
        compiler_params=pltpu.CompilerParams(dimension_semantics=("arbitrary",)),
        name="pad_fill",
    )(counts, pstarts, xs)


def _experts_kernel(ps_ref, nb_ref, xs_ref, wg_ref, wu_ref, wd_ref, y_ref,
                    xbuf, ybuf, wg_b, wu_b, wd_b, sem_in, sem_out):
    e = pl.program_id(0)
    nb = nb_ref[e]
    blk_rows = MOE_BLOCK * ROW_TILE
    g0 = ps_ref[e] // MOE_BLOCK
    n_total = (ps_ref[N_EXPERTS - 1] // MOE_BLOCK) + nb_ref[N_EXPERTS - 1]
    ahead = EXPERT_SLOTS - 1

    def rows_of(g):
        return pl.ds(pl.multiple_of(g * blk_rows, blk_rows), blk_rows)

    def slot_of(g):
        return jnp.bitwise_and(g, EXPERT_SLOTS - 1)

    def in_copy(g):
        s = slot_of(g)
        return pltpu.make_async_copy(xs_ref.at[rows_of(g)], xbuf.at[s], sem_in.at[s])

    def out_copy(g):
        s = slot_of(g)
        return pltpu.make_async_copy(ybuf.at[s], y_ref.at[rows_of(g)], sem_out.at[s])

    @pl.when(e == 0)
    def _prime():
        for u in range(ahead):
            @pl.when(u < n_total)
            def _(u=u):
                in_copy(u).start()

    @pl.when(nb > 0)
    def _run():
        wg_b[...] = wg_ref[0].astype(BF16)
        wu_b[...] = wu_ref[0].astype(BF16)
        wd_b[...] = wd_ref[0].astype(BF16)

        def block(j, carry):
            g = g0 + j
            s = slot_of(g)
            in_copy(g).wait()

            @pl.when(g + ahead < n_total)
            def _prefetch():
                in_copy(g + ahead).start()

            @pl.when(g >= EXPERT_SLOTS)
            def _reclaim():
                out_copy(g - EXPERT_SLOTS).wait()

            xb = _load_row_tiles(xbuf.at[s], MOE_BLOCK).astype(BF16)
            hg = jnp.dot(xb, wg_b[...], preferred_element_type=F32)
            hu = jnp.dot(xb, wu_b[...], preferred_element_type=F32)
            hh = (hg * jax.nn.sigmoid(hg) * hu).astype(BF16)
            _store_row_tiles(ybuf.at[s], jnp.dot(hh, wd_b[...], preferred_element_type=F32))
            out_copy(g).start()
            return carry
        lax.fori_loop(0, nb, block, 0)

    @pl.when(e == N_EXPERTS - 1)
    def _drain():
        for u in range(EXPERT_SLOTS):
            @pl.when(u < n_total)
            def _(u=u):
                out_copy(u).wait()


def _experts_call(pstarts, n_blk, xs, w_gate, w_up, w_down):
    d = D_MODEL

    def wsel(e, ps, nb):
        return (e, 0, 0)
    grid_spec = pltpu.PrefetchScalarGridSpec(
        num_scalar_prefetch=2,
        grid=(N_EXPERTS,),
        in_specs=[
            pl.BlockSpec(memory_space=pl.ANY),
            pl.BlockSpec((1, d, EXPERT_DIM), wsel),
            pl.BlockSpec((1, d, EXPERT_DIM), wsel),
            pl.BlockSpec((1, EXPERT_DIM, d), wsel),
        ],
        out_specs=pl.BlockSpec(memory_space=pl.ANY),
        scratch_shapes=[
            pltpu.VMEM((EXPERT_SLOTS, MOE_BLOCK * ROW_TILE, LANES), U32),
            pltpu.VMEM((EXPERT_SLOTS, MOE_BLOCK * ROW_TILE, LANES), U32),
            pltpu.VMEM((d, EXPERT_DIM), BF16),
            pltpu.VMEM((d, EXPERT_DIM), BF16),
            pltpu.VMEM((EXPERT_DIM, d), BF16),
            pltpu.SemaphoreType.DMA((EXPERT_SLOTS,)),
            pltpu.SemaphoreType.DMA((EXPERT_SLOTS,)),
        ],
    )
    return pl.pallas_call(
        _experts_kernel,
        out_shape=jax.ShapeDtypeStruct(xs.shape, U32),
        grid_spec=grid_spec,
        compiler_params=pltpu.CompilerParams(
            dimension_semantics=("arbitrary",), vmem_limit_bytes=VMEM_LIMIT_BYTES),
        name="experts",
    )(pstarts, n_blk, xs, w_gate, w_up, w_down)


def _sc_row_gather(table, row_idx, chunk, n_chunks):
    n_k = row_idx.shape[0]
    windows = row_idx.shape[1] // n_chunks // SC_GATHER_WINDOW
    mesh = plsc.VectorSubcoreMesh(core_axis_name="c", subcore_axis_name="s")

    @functools.partial(
        pl.kernel, out_type=jax.ShapeDtypeStruct((n_k * windows * SC_GATHER_WINDOW, LANES), table.dtype),
        mesh=mesh, compiler_params=pltpu.CompilerParams(use_tc_tiling_on_sc=True), name="sc_row_gather")
    def gather(table_hbm, idx_hbm, out_hbm):
        def body(idx_vmem, out_vmem):
            pltpu.sync_copy(table_hbm.at[idx_vmem.at[0]], out_vmem)
        pltpu.emit_pipeline(
            body, grid=(n_k, windows),
            in_specs=[pl.BlockSpec((1, SC_GATHER_WINDOW), lambda k, i: (k, chunk * windows + i))],
            out_specs=[pl.BlockSpec((SC_GATHER_WINDOW, LANES), lambda k, i: (k * windows + i, 0))],
            core_axis_name=("c", "s"), dimension_semantics=(pltpu.PARALLEL, pltpu.PARALLEL),
        )(idx_hbm, out_hbm)
    return gather(table, row_idx)


def _combine_kernel(gate_ref, base_ref, *refs):
    yk_refs = refs[:TOP_K]
    g_ref, b_ref = refs[TOP_K:TOP_K + 2]
    out_ref = refs[-1]
    tt = base_ref.shape[0]
    acc = base_ref[...]
    gates = gate_ref[...]
    for k in range(TOP_K):
        acc = acc + gates[:, k:k + 1] * _load_row_tiles(yk_refs[k], tt)
    out_ref[...] = _layer_norm(acc, g_ref[...], b_ref[...])


def _combine_call(chunk, n_chunks, gates_t, base, y_chunk, ln_g, ln_b, carry):
    n_tok, d = base.shape
    tt = COMBINE_TILE
    tiles = n_tok // n_chunks // tt
    first = chunk * tiles
    full = lambda a: pl.BlockSpec(a.shape, lambda i: (0,) * a.ndim)
    y_specs = [pl.BlockSpec((tt * ROW_TILE, LANES), lambda i, k=k: (k * tiles + i, 0)) for k in range(TOP_K)]
    in_specs = [
        pl.BlockSpec((tt, TOP_K), lambda i: (first + i, 0)),
        pl.BlockSpec((tt, d), lambda i: (first + i, 0)),
        *y_specs,
        full(ln_g), full(ln_b),
    ]
    operands = [gates_t, base] + [y_chunk] * TOP_K + [ln_g, ln_b]
    aliases = {}
    if carry is not None:
        in_specs.append(pl.BlockSpec(memory_space=pl.ANY))
        operands.append(carry)
        aliases = {len(operands) - 1: 0}
    return pl.pallas_call(
        _combine_kernel,
        out_shape=jax.ShapeDtypeStruct((n_tok, d), F32),
        grid=(tiles,),
        in_specs=in_specs,
        out_specs=pl.BlockSpec((tt, d), lambda i: (first + i, 0)),
        input_output_aliases=aliases,
        compiler_params=pltpu.CompilerParams(
            dimension_semantics=("arbitrary",), vmem_limit_bytes=VMEM_LIMIT_BYTES),
        name="combine",
    )(*operands)


def _group_weights(w_in):
    blocks = w_in.reshape(D_MODEL, 12, GROUP_WIDTH)
    scale = HEAD_DIM ** -0.5
    groups = []
    for g in range(N_ATTN_GROUPS):
        groups.append(jnp.concatenate(
            [blocks[:, g] * scale, blocks[:, N_ATTN_GROUPS + g], blocks[:, 2 * N_ATTN_GROUPS + g]], axis=1))
    groups.append(jnp.concatenate([blocks[:, 9], blocks[:, 10], blocks[:, 11]], axis=1))
    return jnp.stack(groups).astype(BF16)


def kernel(x, w_in, conv_w, w_o, ln1_g, ln1_b, rel_bias, w_router, router_bias, exp_w_gate, exp_w_up,
           exp_w_down, sh_w_gate, sh_w_up, sh_w_down, ln2_g, ln2_b):
    batch, seq, d = x.shape
    assert d == D_MODEL and seq == DILATED_BRANCHES[-1][0] and w_in.shape[0] == DEPTH
    n_tok = batch * seq
    assert n_tok % POST_TILE == 0 and n_tok % POS_TILE == 0
    assert n_tok % (COMBINE_CHUNKS * COMBINE_TILE) == 0 and n_tok * ROW_TILE % SC_GATHER_WINDOW == 0

    mix = _mix_call(x, _group_weights(w_in[0]), _bias_table(rel_bias), conv_w[0])

    wr = w_router[0]
    wr_hi = wr.astype(BF16)
    wr_lo = (wr - wr_hi.astype(F32)).astype(BF16)
    x2d = x.reshape(n_tok, d)
    base, x1p, e_idx, rank, gates, counts = _post_call(
        mix.reshape(n_tok, d), x2d, w_o[0].astype(BF16), ln1_g, ln1_b,
        wr_hi.T, wr_lo.T, router_bias[0].reshape(N_EXPERTS, 1),
        sh_w_gate[0].astype(BF16), sh_w_up[0].astype(BF16), sh_w_down[0].astype(BF16))

    counts = counts[:, 0]
    padded = (counts + MOE_BLOCK - 1) // MOE_BLOCK * MOE_BLOCK
    pends = jnp.cumsum(padded)
    pstarts = (pends - padded).astype(I32)
    n_blocks = n_tok * TOP_K // MOE_BLOCK + N_EXPERTS
    n_blk = (padded // MOE_BLOCK).astype(I32)
    pos, row_idx = _pos_call(e_idx, rank, pstarts.reshape(N_EXPERTS, 1))

    xs = _sc_row_scatter(x1p, row_idx, n_blocks * MOE_BLOCK * ROW_TILE)
    xs = _pad_fill_call(counts, pstarts, xs)
    y = _experts_call(pstarts, n_blk, xs, exp_w_gate[0], exp_w_up[0], exp_w_down[0])

    gates_t = gates.T
    out = None
    for c in range(COMBINE_CHUNKS):
        y_chunk = _sc_row_gather(y, row_idx, c, COMBINE_CHUNKS)
        out = _combine_call(c, COMBINE_CHUNKS, gates_t, base, y_chunk, ln2_g, ln2_b, out)
    return out.reshape(batch, seq, d)
```
